```python
import math
import jax, jax.numpy as jnp
from jax import lax
import numpy as np

D_MODEL = 1024
BATCH = 32
SEQ = 256
DEPTH = 2
DEC_BATCH = 4
DEC_SEQ = 2048
PAST_LEN = 512

GRID_W = 64
D_HY = 512
HY_ORDER = 2
HY_EMB = 33
HY_FFN = 64
HY_FAST_DECAY = 0.3
HY_SLOW_DECAY = 1.5
HY_TARGET = 1e-2
AT_HEADS = 4
AT_DH = 64
AT_DV = 2 * AT_DH
D_AT = AT_HEADS * AT_DV
Q_BLOCK = 128
ROPE_BASE = 10000.0
D_PL = 512
POOL_WINDOWS = (2, 4, 8, 16)
POOL_GC = D_PL // 4
N_BRANCH = 3
NORM_EPS = 1e-6
IN_SPLITS = (3 * D_HY, D_HY, AT_HEADS * 2 * AT_DH, AT_HEADS * 2 * AT_DH, AT_HEADS * AT_DV, D_AT, D_PL, D_PL, N_BRANCH * D_MODEL)
D_IN = 3 * D_HY + D_HY + 2 * AT_HEADS * 2 * AT_DH + AT_HEADS * AT_DV + D_AT + 2 * D_PL + N_BRANCH * D_MODEL

kernel_name = "hybrid_hyena_diffattn_pool_dit_step"

F32 = jnp.float32


def rms_norm(x, g):
    xf = x.astype(F32)
    y = xf * lax.rsqrt(jnp.mean(xf * xf, axis=-1, keepdims=True) + NORM_EPS)
    return (y * g.astype(F32)).astype(x.dtype)


def short_conv_centred(u, w, b):
    L = u.shape[1]
    up = jnp.pad(u, ((0, 0), (1, 1), (0, 0)))
    return up[:, :L] * w[0] + up[:, 1:L + 1] * w[1] + up[:, 2:L + 2] * w[2] + b


def hyena_filter_spectrum(L, p):
    t = jnp.linspace(0.0, 1.0, L, dtype=F32)[:, None]
    bands = (HY_EMB - 1) // 2
    w = 2.0 * math.pi * jnp.arange(L, dtype=F32)[:, None] / L
    f = jnp.linspace(1e-4, bands - 1, bands, dtype=F32)[None, :]
    z = jnp.concatenate([t, jnp.cos(f * w), -jnp.sin(f * w)], axis=-1)
    fq = p['hy_freq'].astype(F32)
    h = jnp.sin(fq[0] * (z @ p['hy_f_w1'].astype(F32) + p['hy_f_b1'].astype(F32)))
    h = jnp.sin(fq[1] * (h @ p['hy_f_w2'].astype(F32) + p['hy_f_b2'].astype(F32)))
    h = (h @ p['hy_f_w3'].astype(F32)).reshape(L, 2, HY_ORDER, D_HY)
    max_decay = math.log(HY_TARGET) / HY_FAST_DECAY
    min_decay = math.log(HY_TARGET) / HY_SLOW_DECAY
    deltas = jnp.abs(jnp.linspace(min_decay, max_decay, D_HY, dtype=F32))
    h = h * jnp.exp(-t * deltas)[:, None, None, :]
    h_f, h_b = h[:, 0], h[:, 1]
    h_full = jnp.concatenate([h_f, jnp.zeros((1, HY_ORDER, D_HY), F32), h_b[1:][::-1]], axis=0)
    h_full = h_full / (jnp.sum(jnp.abs(h_full), axis=0, keepdims=True) + NORM_EPS)
    return jnp.fft.rfft(h_full, axis=0)


def long_conv(u, Hf, bias):
    L = u.shape[1]
    uf = u.astype(F32)
    U = jnp.fft.rfft(uf, n=2 * L, axis=1)
    y = jnp.fft.irfft(U * Hf[None], n=2 * L, axis=1)[:, :L]
    return (y + uf * bias.astype(F32)).astype(u.dtype)


def hyena_branch(proj, p):
    L = proj.shape[1]
    u = short_conv_centred(proj, p['hy_conv_w'], p['hy_conv_b'])
    x1, x2, v = jnp.split(u, 3, axis=-1)
    Hf = hyena_filter_spectrum(L, p)
    y = x1 * long_conv(v, Hf[:, 0], p['hy_bias'][0])
    y = x2 * long_conv(y, Hf[:, 1], p['hy_bias'][1])
    return y


def axial_rope_tables(L):
    rows = L // GRID_W
    row = jnp.repeat(jnp.arange(rows), GRID_W).astype(F32)
    col = jnp.tile(jnp.arange(GRID_W), rows).astype(F32)
    half = AT_DH // 2
    inv = 1.0 / (ROPE_BASE ** (jnp.arange(0, half, 2, dtype=F32) / half))
    ang = jnp.concatenate([row[:, None] * inv, col[:, None] * inv], axis=-1)
    return jnp.cos(ang), jnp.sin(ang)


def apply_axial_rope(x, cos, sin):
    xf = x.astype(F32)
    xp = xf.reshape(x.shape[:-1] + (AT_DH // 2, 2))
    a, b = xp[..., 0], xp[..., 1]
    c = cos[None, :, None, None, :]
    s = sin[None, :, None, None, :]
    out = jnp.stack([a * c - b * s, a * s + b * c], axis=-1).reshape(x.shape)
    return out.astype(x.dtype)


def diff_attention(q, k, v, lam):
    B, Lq = q.shape[:2]
    nb = Lq // Q_BLOCK
    qb = q.astype(F32).reshape(B, nb, Q_BLOCK, AT_HEADS, 2, AT_DH).transpose(1, 0, 2, 3, 4, 5)
    kf = k.astype(F32)
    vf = v.astype(F32)
    scale = AT_DH ** -0.5

    def one_block(qblk):
        s = jnp.einsum('bqhcd,bkhcd->bhcqk', qblk, kf) * scale
        pr = jax.nn.softmax(s, axis=-1)
        w = pr[:, :, 0] - lam * pr[:, :, 1]
        return jnp.einsum('bhqk,bkhe->bqhe', w, vf)

    o = lax.map(one_block, qb)
    return o.transpose(1, 0, 2, 3, 4).reshape(B, Lq, AT_HEADS, AT_DV).astype(v.dtype)


def attn_branch(q, k, v, p, lam, lam_init, rope, ctx_k, ctx_v):
    B, L = q.shape[:2]
    q = rms_norm(q.reshape(B, L, AT_HEADS, 2, AT_DH), p['q_norm_g'])
    k = rms_norm(k.reshape(B, L, AT_HEADS, 2, AT_DH), p['k_norm_g'])
    v = v.reshape(B, L, AT_HEADS, AT_DV)
    kv_out = (k, v)
    if rope is None:
        keys, vals = k, v
    else:
        q = apply_axial_rope(q, rope[0], rope[1])
        k = apply_axial_rope(k, rope[0], rope[1])
        keys = jnp.concatenate([k, ctx_k.astype(k.dtype)], axis=1)
        vals = jnp.concatenate([v, ctx_v.astype(v.dtype)], axis=1)
    o = diff_attention(q, keys, vals, lam)
    o = rms_norm(o, p['subln_g']) * (1.0 - lam_init)
    return o.reshape(B, L, D_AT), kv_out


def pool_branch(u, p):
    B, L, _ = u.shape
    uf = u.astype(F32)
    cs = jnp.concatenate([jnp.zeros((B, 1, D_PL), F32), jnp.cumsum(uf, axis=1)], axis=1)
    t = jnp.arange(L)
    means = []
    for g, w in enumerate(POOL_WINDOWS):
        lo = jnp.clip(t - w // 2, 0, L)
        hi = jnp.clip(t + w // 2, 0, L)
        csg = cs[:, :, g * POOL_GC:(g + 1) * POOL_GC]
        means.append((csg[:, hi] - csg[:, lo]) / (hi - lo).astype(F32)[None, :, None])
    m = (jnp.concatenate(means, axis=-1) - uf).reshape(B, L, 4, POOL_GC)
    y = jnp.einsum('blgc,gcd->blgd', m, p['pool_w'].astype(F32)).reshape(B, L, D_PL)
    return (y * p['pool_scale'].astype(F32)).astype(u.dtype)


def trunk_layer(x, cond, p, lam, lam_init, rope, ctx_k, ctx_v):
    B, L, _ = x.shape
    mod = jax.nn.silu(cond) @ p['w_ada'] + p['b_ada']
    shift, scale, gate = jnp.split(mod[:, None, :], 3, axis=-1)
    h = rms_norm(x, p['norm_g']) * (1.0 + scale) + shift
    proj = h @ p['w_in']
    cuts = np.cumsum(IN_SPLITS)[:-1].tolist()
    hy_u, hy_z, q, k, v, at_z, pl_u, pl_z, mg = jnp.split(proj, cuts, axis=-1)
    y_hy = hyena_branch(hy_u, p) * jax.nn.silu(hy_z)
    y_at, kv = attn_branch(q, k, v, p, lam, lam_init, rope, ctx_k, ctx_v)
    y_at = y_at * jax.nn.silu(at_z)
    y_pl = pool_branch(pl_u, p) * jax.nn.silu(pl_z)
    r = jax.nn.sigmoid(mg).reshape(B, L, N_BRANCH, D_MODEL)
    merged = (r[:, :, 0] * (y_hy @ p['w_hy_o'])
              + r[:, :, 1] * (y_at @ p['w_at_o'])
              + r[:, :, 2] * (y_pl @ p['w_pl_o']))
    out = merged @ p['w_out']
    return x + gate * out, kv


def setup_inputs(seed: int = 0) -> dict:
    key = jax.random.key(seed)
    ks = iter(jax.random.split(key, 40))

    def nrm(shape, s):
        return jax.random.normal(next(ks), shape, F32) * s

    D = D_MODEL
    return {
        'x_prompt': nrm((BATCH, SEQ, D), 1.0),
        'x_sample': nrm((DEC_BATCH, DEC_SEQ, D), 1.0),
        'cache_k': nrm((DEC_BATCH, DEPTH, PAST_LEN, AT_HEADS, 2, AT_DH), 1.0),
        'cache_v': nrm((DEC_BATCH, DEPTH, PAST_LEN, AT_HEADS, AT_DV), 1.0),
        'c': nrm((DEC_BATCH, D), 1.0),
        'c_ctx': nrm((D,), 1.0),
        'norm_g': 1.0 + nrm((DEPTH, D), 0.05),
        'w_ada': nrm((DEPTH, D, 3 * D), 0.5 * D ** -0.5),
        'b_ada': nrm((DEPTH, 3 * D), 0.02),
        'w_in': nrm((DEPTH, D, D_IN), D ** -0.5),
        'hy_conv_w': nrm((DEPTH, 3, 3 * D_HY), 0.5),
        'hy_conv_b': nrm((DEPTH, 3 * D_HY), 0.02),
        'hy_f_w1': nrm((DEPTH, HY_EMB, HY_FFN), HY_EMB ** -0.5),
        'hy_f_b1': nrm((DEPTH, HY_FFN), 0.1),
        'hy_f_w2': nrm((DEPTH, HY_FFN, HY_FFN), HY_FFN ** -0.5),
        'hy_f_b2': nrm((DEPTH, HY_FFN), 0.1),
        'hy_f_w3': nrm((DEPTH, HY_FFN, 2 * HY_ORDER * D_HY), HY_FFN ** -0.5),
        'hy_freq': 1.0 + nrm((DEPTH, 2, HY_FFN), 0.1),
        'hy_bias': nrm((DEPTH, HY_ORDER, D_HY), 0.5),
        'q_norm_g': 1.0 + nrm((DEPTH, AT_DH), 0.05),
        'k_norm_g': 1.0 + nrm((DEPTH, AT_DH), 0.05),
        'lam_q': nrm((DEPTH, 2, AT_DH), 0.1),
        'lam_k': nrm((DEPTH, 2, AT_DH), 0.1),
        'subln_g': 1.0 + nrm((DEPTH, AT_DV), 0.05),
        'pool_w': nrm((DEPTH, 4, POOL_GC, POOL_GC), POOL_GC ** -0.5),
        'pool_scale': 1.0 + nrm((DEPTH, D_PL), 0.1),
        'w_hy_o': nrm((DEPTH, D_HY, D), D_HY ** -0.5),
        'w_at_o': nrm((DEPTH, D_AT, D), D_AT ** -0.5),
        'w_pl_o': nrm((DEPTH, D_PL, D), D_PL ** -0.5),
        'w_out': nrm((DEPTH, D, D), D ** -0.5),
    }


def reference(x_prompt, x_sample, cache_k, cache_v, c, c_ctx, norm_g, w_ada, b_ada, w_in,
              hy_conv_w, hy_conv_b, hy_f_w1, hy_f_b1, hy_f_w2, hy_f_b2, hy_f_w3, hy_freq, hy_bias,
              q_norm_g, k_norm_g, lam_q, lam_k, subln_g, pool_w, pool_scale,
              w_hy_o, w_at_o, w_pl_o, w_out):
    rope = axial_rope_tables(x_sample.shape[1])
    yp = x_prompt
    ys = x_sample
    new_ks = []
    new_vs = []
    for l in range(DEPTH):
        p = dict(norm_g=norm_g[l], w_ada=w_ada[l], b_ada=b_ada[l], w_in=w_in[l],
                 hy_conv_w=hy_conv_w[l], hy_conv_b=hy_conv_b[l], hy_f_w1=hy_f_w1[l], hy_f_b1=hy_f_b1[l],
                 hy_f_w2=hy_f_w2[l], hy_f_b2=hy_f_b2[l], hy_f_w3=hy_f_w3[l], hy_freq=hy_freq[l],
                 hy_bias=hy_bias[l], q_norm_g=q_norm_g[l], k_norm_g=k_norm_g[l], subln_g=subln_g[l],
                 pool_w=pool_w[l], pool_scale=pool_scale[l], w_hy_o=w_hy_o[l], w_at_o=w_at_o[l],
                 w_pl_o=w_pl_o[l], w_out=w_out[l])
        lam_init = 0.8 - 0.6 * math.exp(-0.3 * l)
        lq = lam_q[l].astype(F32)
        lk = lam_k[l].astype(F32)
        lam = jnp.exp(jnp.sum(lq[0] * lk[0])) - jnp.exp(jnp.sum(lq[1] * lk[1])) + lam_init
        yp, (k_l, v_l) = trunk_layer(yp, c_ctx[None, :], p, lam, lam_init, None, None, None)
        new_ks.append(k_l)
        new_vs.append(v_l)
        ys, _ = trunk_layer(ys, c, p, lam, lam_init, rope, cache_k[:, l], cache_v[:, l])
    new_k = jnp.stack(new_ks, axis=1)
    new_v = jnp.stack(new_vs, axis=1)
    return (yp, ys, new_k, new_v)
```

```python
import functools
import math

import jax
import jax.numpy as jnp
from jax import lax
from jax.experimental import pallas as pl
from jax.experimental.pallas import tpu as pltpu

F32 = jnp.float32
BF16 = jnp.bfloat16
HIGHEST = lax.Precision.HIGHEST

D_MODEL = 1024
DEPTH = 2
GRID_W = 64
GRID_W_LOG2 = GRID_W.bit_length() - 1
assert GRID_W == 1 << GRID_W_LOG2
D_HY = 512
HY_ORDER = 2
HY_EMB = 33
HY_BANDS = (HY_EMB - 1) // 2
HY_FFN = 64
HY_FAST_DECAY = 0.3
HY_SLOW_DECAY = 1.5
HY_TARGET = 1e-2
AT_HEADS = 4
AT_DH = 64
AT_DV = 2 * AT_DH
D_AT = AT_HEADS * AT_DV
ROPE_BASE = 10000.0
D_PL = 512
POOL_WINDOWS = (2, 4, 8, 16)
POOL_GC = D_PL // 4
POOL_PAD = 16
NORM_EPS = 1e-6
D_IN = 8192
COND_ROWS = 8

CB = 512
COL_HY_X1, COL_HY_X2, COL_HY_V, COL_HY_Z = 0, 1, 2, 3
COL_Q, COL_K, COL_V, COL_AT_Z, COL_PL_U, COL_PL_Z = 4, 5, 6, 7, 8, 9
COL_MG = 10

VMEM_LIMIT_BYTES = 56 * 1024 * 1024


def _params(*sem):
    return pltpu.CompilerParams(dimension_semantics=sem, vmem_limit_bytes=VMEM_LIMIT_BYTES)


def _const_spec(shape):
    return pl.BlockSpec(shape, lambda *_: (0,) * len(shape), pipeline_mode=pl.Buffered(1))


def _silu(x):
    return x * jax.nn.sigmoid(x)


def _bdot(a, b):
    return jnp.dot(a.astype(BF16), b.astype(BF16), preferred_element_type=F32)


def _fdot(a, b):
    return jnp.dot(a, b, precision=HIGHEST, preferred_element_type=F32)


def _mod_kernel(cond_ref, w_ref, b_ref, o_ref):
    o_ref[...] = _fdot(_silu(cond_ref[...]), w_ref[...]) + b_ref[...]


def _modulation(cond, w_ada, b_ada):
    d = D_MODEL
    return pl.pallas_call(
        _mod_kernel,
        grid=(DEPTH, 3),
        in_specs=[
            pl.BlockSpec((COND_ROWS, d), lambda l, n: (0, 0)),
            pl.BlockSpec((None, d, d), lambda l, n: (l, 0, n)),
            pl.BlockSpec((None, 1, d), lambda l, n: (l, 0, n)),
        ],
        out_specs=pl.BlockSpec((None, COND_ROWS, d), lambda l, n: (l, 0, n)),
        out_shape=jax.ShapeDtypeStruct((DEPTH, COND_ROWS, 3 * d), F32),
        compiler_params=_params("parallel", "parallel"),
        name="modulation",
    )(cond, w_ada, b_ada.reshape(DEPTH, 1, 3 * d))


def _in_kernel(x_ref, mod_ref, g_ref, w_ref, o_ref, h_ref):
    @pl.when(pl.program_id(1) == 0)
    def _():
        x = x_ref[...]
        y = x * lax.rsqrt(jnp.mean(x * x, axis=-1, keepdims=True) + NORM_EPS) * g_ref[...]
        shift = mod_ref[:, 0:D_MODEL]
        scale = mod_ref[:, D_MODEL:2 * D_MODEL]
        h_ref[...] = (y * (1.0 + scale) + shift).astype(BF16)

    o_ref[...] = jnp.dot(h_ref[...], w_ref[...], preferred_element_type=F32)


def _in_proj(x, mod, mod_row0, rows_per_seq, norm_g, w_in_bf16, tm=512, tn=2048):
    t = x.shape[0]
    blocks_per_seq = rows_per_seq // tm
    return pl.pallas_call(
        _in_kernel,
        grid=(t // tm, D_IN // tn),
        in_specs=[
            pl.BlockSpec((tm, D_MODEL), lambda i, j: (i, 0)),
            pl.BlockSpec((None, 1, 3 * D_MODEL), lambda i, j: (mod_row0 + i // blocks_per_seq, 0, 0)),
            pl.BlockSpec((1, D_MODEL), lambda i, j: (0, 0)),
            pl.BlockSpec((D_MODEL, tn), lambda i, j: (0, j)),
        ],
        out_specs=pl.BlockSpec((tm, tn), lambda i, j: (i, j)),
        out_shape=jax.ShapeDtypeStruct((t, D_IN), F32),
        scratch_shapes=[pltpu.VMEM((tm, D_MODEL), BF16)],
        compiler_params=_params("parallel", "arbitrary"),
        name="in_proj",
    )(x, mod, norm_g, w_in_bf16)


def _dft_kernel(c_ref, s_ref, *, seq, tr):
    k = lax.broadcasted_iota(jnp.int32, (tr, seq), 0) + pl.program_id(0) * tr
    n = lax.broadcasted_iota(jnp.int32, (tr, seq), 1)
    ang = ((k * n) & (2 * seq - 1)).astype(F32) * (math.pi / seq)
    c_ref[...] = jnp.cos(ang).astype(BF16)
    s_ref[...] = jnp.sin(ang).astype(BF16)


def _dft_tables(seq):
    tr = min(seq, 256)
    spec = pl.BlockSpec((tr, seq), lambda i: (i, 0))
    shape = jax.ShapeDtypeStruct((seq, seq), BF16)
    return pl.pallas_call(
        functools.partial(_dft_kernel, seq=seq, tr=tr),
        grid=(seq // tr,),
        out_specs=[spec, spec],
        out_shape=[shape, shape],
        compiler_params=_params("parallel"),
        name=f"dft_tables_{seq}",
    )()


FILTER_CB = 256


def _filter_kernel(w1t_ref, w1c_ref, w1s_ref, b1_ref, w2_ref, b2_ref, fq_ref, w3f_ref, w3b_ref,
                   c_ref, s_ref, a_ref, b_ref, nyq_ref, *, seq):
    row = lax.broadcasted_iota(jnp.int32, (seq, 1), 0)
    n = row.astype(F32)
    t = n * (1.0 / (seq - 1))
    band = lax.broadcasted_iota(jnp.int32, (1, HY_BANDS), 1).astype(F32)
    f = 1e-4 + band * ((HY_BANDS - 1 - 1e-4) / (HY_BANDS - 1))
    fw = f * (n * (2.0 * math.pi / seq))
    pre = t * w1t_ref[...] + _fdot(jnp.cos(fw), w1c_ref[...]) - _fdot(jnp.sin(fw), w1s_ref[...]) + b1_ref[...]
    h = jnp.sin(fq_ref[0:1, :] * pre)
    h = jnp.sin(fq_ref[1:2, :] * (_fdot(h, w2_ref[...]) + b2_ref[...]))
    hf = _fdot(h, w3f_ref[...])
    hb = _fdot(h, w3b_ref[...])

    chan = lax.broadcasted_iota(jnp.int32, (1, FILTER_CB), 1) + (pl.program_id(0) % (D_HY // FILTER_CB)) * FILTER_CB
    max_decay = math.log(HY_TARGET) / HY_FAST_DECAY
    min_decay = math.log(HY_TARGET) / HY_SLOW_DECAY
    deltas = jnp.abs(min_decay + chan.astype(F32) * ((max_decay - min_decay) / (D_HY - 1)))
    decay = jnp.exp(-t * deltas)
    hf = hf * decay
    hb = jnp.where(row == 0, 0.0, hb * decay)
    inv = 1.0 / (jnp.sum(jnp.abs(hf), axis=0, keepdims=True) + jnp.sum(jnp.abs(hb), axis=0, keepdims=True) + NORM_EPS)
    hs = (hf + hb) * inv
    hd = (hf - hb) * inv
    h_re = _bdot(c_ref[...], hs)
    h_im = -_bdot(s_ref[...], hd)
    alt = (1 - 2 * (row & 1)).astype(F32)
    wk = jnp.where(row == 0, 1.0, 2.0) * (0.5 / seq)
    a_ref[...] = wk * h_re
    b_ref[...] = wk * h_im
    nyq_ref[...] = jnp.sum(alt * hs, axis=0, keepdims=True) * (0.5 / seq)


def _hyena_filters(seq, w1, b1, w2, b2, w3, freq, cmat, smat):
    nc = HY_ORDER * D_HY
    nblk = nc // FILTER_CB
    small = lambda a: pl.BlockSpec(a.shape, lambda j: (0,) * a.ndim)
    w1t, w1c, w1s = w1[0:1], w1[1:1 + HY_BANDS], w1[1 + HY_BANDS:]
    b1 = b1.reshape(1, HY_FFN)
    b2 = b2.reshape(1, HY_FFN)
    out_blk = pl.BlockSpec((seq, FILTER_CB), lambda j: (0, j))
    return pl.pallas_call(
        functools.partial(_filter_kernel, seq=seq),
        grid=(nblk,),
        in_specs=[small(w1t), small(w1c), small(w1s), small(b1), small(w2), small(b2), small(freq),
                  pl.BlockSpec((HY_FFN, FILTER_CB), lambda j: (0, j)),
                  pl.BlockSpec((HY_FFN, FILTER_CB), lambda j: (0, nblk + j)),
                  _const_spec((seq, seq)), _const_spec((seq, seq))],
        out_specs=[out_blk, out_blk, pl.BlockSpec((1, FILTER_CB), lambda j: (0, j))],
        out_shape=[jax.ShapeDtypeStruct((seq, nc), F32), jax.ShapeDtypeStruct((seq, nc), F32),
                   jax.ShapeDtypeStruct((1, nc), F32)],
        compiler_params=_params("parallel"),
        name=f"hyena_filters_{seq}",
    )(w1t, w1c, w1s, b1, w2, b2, freq, w3, w3, cmat, smat)


def _hyena_kernel(x1_ref, x2_ref, v_ref, z_ref, w1_ref, w2_ref, wv_ref, c1_ref, c2_ref, cv_ref,
                  a0_ref, b0_ref, a1_ref, b1_ref, n0_ref, n1_ref, s0_ref, s1_ref, cm_ref, sm_ref,
                  o_ref, u_ref, g_ref, ub_ref, p_ref, q_ref, *, seq, rb):
    row = lax.broadcasted_iota(jnp.int32, (seq, 1), 0)
    alt = (1 - 2 * (row & 1)).astype(F32)

    def short_conv(x_ref, w_ref, c_ref):
        x = x_ref[...]
        prev = jnp.where(row == 0, 0.0, pltpu.roll(x, 1, 0))
        nxt = jnp.where(row == seq - 1, 0.0, pltpu.roll(x, seq - 1, 0))
        return prev * w_ref[0:1, :] + x * w_ref[1:2, :] + nxt * w_ref[2:3, :] + c_ref[...]

    def long_conv(a_ref, b_ref, nyq_ref, skip_ref, finish):
        nyq = jnp.sum(u_ref[...] * alt, axis=0, keepdims=True) * nyq_ref[...]
        for r in range(0, seq, rb):
            rows = pl.ds(r, rb)
            uc = jnp.dot(cm_ref[rows, :], ub_ref[...], preferred_element_type=F32)
            us = jnp.dot(sm_ref[rows, :], ub_ref[...], preferred_element_type=F32)
            a = a_ref[rows, :]
            b = b_ref[rows, :]
            p_ref[rows, :] = (uc * a + us * b).astype(BF16)
            q_ref[rows, :] = (us * a - uc * b).astype(BF16)
        for r in range(0, seq, rb):
            rows = pl.ds(r, rb)
            y = (jnp.dot(cm_ref[rows, :], p_ref[...], preferred_element_type=F32)
                 + jnp.dot(sm_ref[rows, :], q_ref[...], preferred_element_type=F32)
                 + alt[r:r + rb] * nyq + u_ref[rows, :] * skip_ref[...])
            finish(rows, g_ref[rows, :] * y)

    def keep(rows, y):
        u_ref[rows, :] = y
        ub_ref[rows, :] = y.astype(BF16)

    def emit(rows, y):
        o_ref[rows, :] = (y * _silu(z_ref[rows, :])).astype(o_ref.dtype)

    v = short_conv(v_ref, wv_ref, cv_ref)
    u_ref[...] = v
    ub_ref[...] = v.astype(BF16)
    g_ref[...] = short_conv(x1_ref, w1_ref, c1_ref)
    long_conv(a0_ref, b0_ref, n0_ref, s0_ref, keep)
    g_ref[...] = short_conv(x2_ref, w2_ref, c2_ref)
    long_conv(a1_ref, b1_ref, n1_ref, s1_ref, emit)


def _hyena_branch(proj, batch, seq, conv_w, conv_b, hy_bias, filt_a, filt_b, filt_nyq, cmat, smat):
    cb = 256 if seq > 512 else D_HY
    rb = min(seq, 512)
    nj = D_HY // cb
    t = batch * seq
    conv_b = conv_b.reshape(1, 3 * D_HY)
    skip = hy_bias.reshape(1, HY_ORDER * D_HY)

    def proj_spec(col512):
        off = col512 * (CB // cb)
        return pl.BlockSpec((seq, cb), lambda j, b: (b, off + j))

    def chan_spec(rows, part):
        off = part * nj
        return pl.BlockSpec((rows, cb), lambda j, b: (0, off + j), pipeline_mode=pl.Buffered(1))

    return pl.pallas_call(
        functools.partial(_hyena_kernel, seq=seq, rb=rb),
        grid=(nj, batch),
        in_specs=[proj_spec(COL_HY_X1), proj_spec(COL_HY_X2), proj_spec(COL_HY_V), proj_spec(COL_HY_Z),
                  chan_spec(3, 0), chan_spec(3, 1), chan_spec(3, 2),
                  chan_spec(1, 0), chan_spec(1, 1), chan_spec(1, 2),
                  chan_spec(seq, 0), chan_spec(seq, 0), chan_spec(seq, 1), chan_spec(seq, 1),
                  chan_spec(1, 0), chan_spec(1, 1), chan_spec(1, 0), chan_spec(1, 1),
                  _const_spec((seq, seq)), _const_spec((seq, seq))],
        out_specs=pl.BlockSpec((seq, cb), lambda j, b: (b, j)),
        out_shape=jax.ShapeDtypeStruct((t, D_HY), BF16),
        scratch_shapes=[pltpu.VMEM((seq, cb), F32), pltpu.VMEM((seq, cb), F32), pltpu.VMEM((seq, cb), BF16),
                        pltpu.VMEM((seq, cb), BF16), pltpu.VMEM((seq, cb), BF16)],
        compiler_params=_params("parallel", "parallel"),
        name=f"hyena_{seq}",
    )(proj, proj, proj, proj, conv_w, conv_w, conv_w, conv_b, conv_b, conv_b,
      filt_a, filt_b, filt_a, filt_b, filt_nyq, filt_nyq, skip, skip, cmat, smat)


def _rope_kernel(cos_ref, sin_ref, *, tr):
    lane = lax.broadcasted_iota(jnp.int32, (1, D_AT), 1)
    dim = lane & (AT_DH - 1)
    pair = ((dim & (AT_DH // 2 - 1)) >> 1).astype(F32)
    inv = jnp.exp(pair * (-(2.0 / (AT_DH // 2)) * math.log(ROPE_BASE)))
    pos = lax.broadcasted_iota(jnp.int32, (tr, 1), 0) + pl.program_id(0) * tr
    coord = jnp.where(dim < AT_DH // 2, pos >> GRID_W_LOG2, pos & (GRID_W - 1)).astype(F32)
    ang = coord * inv
    cos_ref[...] = jnp.cos(ang)
    sin_ref[...] = jnp.where((lane & 1) == 0, -jnp.sin(ang), jnp.sin(ang))


def _rope_tables(seq):
    tr = 256
    spec = pl.BlockSpec((tr, D_AT), lambda i: (i, 0))
    shape = jax.ShapeDtypeStruct((seq, D_AT), F32)
    return pl.pallas_call(
        functools.partial(_rope_kernel, tr=tr),
        grid=(seq // tr,),
        out_specs=[spec, spec],
        out_shape=[shape, shape],
        compiler_params=_params("parallel"),
        name="rope_tables",
    )()


def _group_mean_matrix():
    r = lax.broadcasted_iota(jnp.int32, (D_AT, D_AT), 0) // AT_DH
    c = lax.broadcasted_iota(jnp.int32, (D_AT, D_AT), 1) // AT_DH
    return jnp.where(r == c, 1.0 / AT_DH, 0.0).astype(BF16)


def _group_rms_norm(x, g, mean_mat):
    x2 = x * x
    hi = x2.astype(BF16)
    lo = (x2 - hi.astype(F32)).astype(BF16)
    ms = (jnp.dot(hi, mean_mat, preferred_element_type=F32)
          + jnp.dot(lo, mean_mat, preferred_element_type=F32))
    return x * lax.rsqrt(ms + NORM_EPS) * g


def _rotate(x, cos, sin_signed):
    lane = lax.broadcasted_iota(jnp.int32, (1, D_AT), 1)
    partner = jnp.where((lane & 1) == 0, pltpu.roll(x, D_AT - 1, 1), pltpu.roll(x, 1, 1))
    return x * cos + partner * sin_signed


def _prep_rope_kernel(q_ref, k_ref, v_ref, gq_ref, gk_ref, cos_ref, sin_ref, qo_ref, ko_ref, vo_ref):
    mean_mat = _group_mean_matrix()
    cos = cos_ref[...]
    sin = sin_ref[...]
    q = _rotate(_group_rms_norm(q_ref[...], gq_ref[...], mean_mat), cos, sin)
    k = _rotate(_group_rms_norm(k_ref[...], gk_ref[...], mean_mat), cos, sin)
    qo_ref[...] = (q * AT_DH ** -0.5).astype(BF16)
    ko_ref[...] = k.astype(BF16)
    vo_ref[...] = v_ref[...].astype(BF16)


def _prep_plain_kernel(q_ref, k_ref, v_ref, gq_ref, gk_ref, qo_ref, ko_ref, vo_ref, kn_ref):
    mean_mat = _group_mean_matrix()
    q = _group_rms_norm(q_ref[...], gq_ref[...], mean_mat)
    k = _group_rms_norm(k_ref[...], gk_ref[...], mean_mat)
    qo_ref[...] = (q * AT_DH ** -0.5).astype(BF16)
    ko_ref[...] = k.astype(BF16)
    kn_ref[...] = k
    vo_ref[...] = v_ref[...].astype(BF16)


def _attn_prep(proj, seq, gq, gk, rope):
    t = proj.shape[0]
    tp = 256
    col = lambda c: pl.BlockSpec((tp, CB), lambda i: (i, c))
    vec = pl.BlockSpec((1, D_AT), lambda i: (0, 0))
    out = pl.BlockSpec((tp, D_AT), lambda i: (i, 0))
    bshape = jax.ShapeDtypeStruct((t, D_AT), BF16)
    if rope is None:
        return pl.pallas_call(
            _prep_plain_kernel,
            grid=(t // tp,),
            in_specs=[col(COL_Q), col(COL_K), col(COL_V), vec, vec],
            out_specs=[out, out, out, out],
            out_shape=[bshape, bshape, bshape, jax.ShapeDtypeStruct((t, D_AT), F32)],
            compiler_params=_params("parallel"),
            name="attn_prep_ctx",
        )(proj, proj, proj, gq, gk)
    per_seq = seq // tp
    tab = pl.BlockSpec((tp, D_AT), lambda i: (i % per_seq, 0))
    q, k, v = pl.pallas_call(
        _prep_rope_kernel,
        grid=(t // tp,),
        in_specs=[col(COL_Q), col(COL_K), col(COL_V), vec, vec, tab, tab],
        out_specs=[out, out, out],
        out_shape=[bshape, bshape, bshape],
        compiler_params=_params("parallel"),
        name="attn_prep_latent",
    )(proj, proj, proj, gq, gk, rope[0], rope[1])
    return q, k, v, None


def _nt_dot(a, b):
    return lax.dot_general(a, b, (((1,), (1,)), ((), ())), preferred_element_type=F32)


def _attn_kernel(*refs, lam_init, has_ctx):
    if has_ctx:
        q_ref, k_ref, v_ref, z_ref, lq_ref, lk_ref, sg_ref, ck_ref, cv_ref, o_ref = refs
    else:
        q_ref, k_ref, v_ref, z_ref, lq_ref, lk_ref, sg_ref, o_ref = refs
    lqk = lq_ref[...] * lk_ref[...]
    lam = (jnp.exp(jnp.sum(lqk[0:1, :], axis=-1, keepdims=True))
           - jnp.exp(jnp.sum(lqk[1:2, :], axis=-1, keepdims=True)) + lam_init)

    for h in range(AT_HEADS):
        def softmax_parts(c):
            cols = slice(h * AT_DV + c * AT_DH, h * AT_DV + (c + 1) * AT_DH)
            q = q_ref[:, cols]
            s = _nt_dot(q, k_ref[:, cols])
            m = jnp.max(s, axis=-1, keepdims=True)
            if has_ctx:
                sc = _nt_dot(q, ck_ref[:, cols].astype(BF16))
                m = jnp.maximum(m, jnp.max(sc, axis=-1, keepdims=True))
                ec = jnp.exp(sc - m)
            e = jnp.exp(s - m)
            den = jnp.sum(e, axis=-1, keepdims=True)
            if has_ctx:
                den = den + jnp.sum(ec, axis=-1, keepdims=True)
                return e, ec, 1.0 / den
            return e, None, 1.0 / den

        e1, ec1, r1 = softmax_parts(0)
        e2, ec2, r2 = softmax_parts(1)
        r2 = r2 * lam
        hv = slice(h * AT_DV, (h + 1) * AT_DV)
        o = jnp.dot((e1 * r1 - e2 * r2).astype(BF16), v_ref[:, hv], preferred_element_type=F32)
        if has_ctx:
            o = o + jnp.dot((ec1 * r1 - ec2 * r2).astype(BF16), cv_ref[:, hv].astype(BF16),
                            preferred_element_type=F32)
        o = o * lax.rsqrt(jnp.mean(o * o, axis=-1, keepdims=True) + NORM_EPS) * sg_ref[...]
        o_ref[:, hv] = (o * (1.0 - lam_init) * _silu(z_ref[:, hv])).astype(o_ref.dtype)


def _attention(q, k, v, proj, batch, seq, lam_q, lam_k, subln_g, lam_init, ctx, layer):
    t = batch * seq
    tq = 256
    nq = seq // tq
    small = lambda a: pl.BlockSpec(a.shape, lambda b, i: (0,) * a.ndim)
    in_specs = [pl.BlockSpec((tq, D_AT), lambda b, i: (b * nq + i, 0)),
                pl.BlockSpec((seq, D_AT), lambda b, i: (b, 0)),
                pl.BlockSpec((seq, D_AT), lambda b, i: (b, 0)),
                pl.BlockSpec((tq, CB), lambda b, i: (b * nq + i, COL_AT_Z)),
                small(lam_q), small(lam_k), small(subln_g)]
    args = [q, k, v, proj, lam_q, lam_k, subln_g]
    if ctx is not None:
        ck, cv = ctx
        past = ck.shape[2]
        cspec = pl.BlockSpec((None, None, past, D_AT), lambda b, i: (b, layer, 0, 0))
        in_specs += [cspec, cspec]
        args += [ck, cv]
    return pl.pallas_call(
        functools.partial(_attn_kernel, lam_init=lam_init, has_ctx=ctx is not None),
        grid=(batch, nq),
        in_specs=in_specs,
        out_specs=pl.BlockSpec((tq, D_AT), lambda b, i: (b * nq + i, 0)),
        out_shape=jax.ShapeDtypeStruct((t, D_AT), BF16),
        compiler_params=_params("parallel", "parallel"),
        name=f"diff_attn_{seq}",
    )(*args)


def _pool_kernel(u_ref, z_ref, w_ref, sc_ref, o_ref, *, seq):
    padded = seq + 2 * POOL_PAD
    t = lax.broadcasted_iota(jnp.int32, (seq, 1), 0)
    zeros = jnp.zeros((POOL_PAD, POOL_GC), F32)
    for g, win in enumerate(POOL_WINDOWS):
        cols = slice(g * POOL_GC, (g + 1) * POOL_GC)
        u = u_ref[:, cols]
        s = jnp.concatenate([zeros, u, zeros], axis=0)
        s = s + pltpu.roll(s, 1, 0)
        half = 1
        while 2 * half < win:
            s = pltpu.roll(s, padded - half, 0) + pltpu.roll(s, half, 0)
            half *= 2
        lo = jnp.maximum(t - win // 2, 0)
        hi = jnp.minimum(t + win // 2, seq)
        m = s[POOL_PAD:POOL_PAD + seq] / (hi - lo).astype(F32) - u
        y = jnp.dot(m.astype(BF16), w_ref[g].astype(BF16), preferred_element_type=F32)
        o_ref[:, cols] = (y * sc_ref[:, cols] * _silu(z_ref[:, cols])).astype(o_ref.dtype)


def _pool_branch(proj, batch, seq, pool_w, pool_scale):
    t = batch * seq
    return pl.pallas_call(
        functools.partial(_pool_kernel, seq=seq),
        grid=(batch,),
        in_specs=[pl.BlockSpec((seq, CB), lambda b: (b, COL_PL_U)),
                  pl.BlockSpec((seq, CB), lambda b: (b, COL_PL_Z)),
                  pl.BlockSpec(pool_w.shape, lambda b: (0, 0, 0)),
                  pl.BlockSpec((1, D_PL), lambda b: (0, 0))],
        out_specs=pl.BlockSpec((seq, D_PL), lambda b: (b, 0)),
        out_shape=jax.ShapeDtypeStruct((t, D_PL), BF16),
        compiler_params=_params("parallel"),
        name=f"pool_{seq}",
    )(proj, proj, pool_w, pool_scale)


def _merge_kernel(x_ref, mod_ref, yh_ref, ya_ref, yp_ref, g0_ref, g1_ref, g2_ref,
                  wh_ref, wa_ref, wp_ref, wo_ref, o_ref):
    merged = (jax.nn.sigmoid(g0_ref[...]) * jnp.dot(yh_ref[...], wh_ref[...], preferred_element_type=F32)
              + jax.nn.sigmoid(g1_ref[...]) * jnp.dot(ya_ref[...], wa_ref[...], preferred_element_type=F32)
              + jax.nn.sigmoid(g2_ref[...]) * jnp.dot(yp_ref[...], wp_ref[...], preferred_element_type=F32))
    out = jnp.dot(merged.astype(BF16), wo_ref[...], preferred_element_type=F32)
    o_ref[...] = x_ref[...] + mod_ref[:, 2 * D_MODEL:3 * D_MODEL] * out


def _merge(x, mod, mod_row0, rows_per_seq, proj, y_hy, y_at, y_pl, w_hy_o, w_at_o, w_pl_o, w_out, tm=512):
    t = x.shape[0]
    blocks_per_seq = rows_per_seq // tm
    d = D_MODEL
    branch = pl.BlockSpec((tm, CB), lambda i: (i, 0))
    gate = lambda g: pl.BlockSpec((tm, d), lambda i: (i, COL_MG // 2 + g))
    return pl.pallas_call(
        _merge_kernel,
        grid=(t // tm,),
        in_specs=[pl.BlockSpec((tm, d), lambda i: (i, 0)),
                  pl.BlockSpec((None, 1, 3 * d), lambda i: (mod_row0 + i // blocks_per_seq, 0, 0)),
                  branch, branch, branch, gate(0), gate(1), gate(2),
                  _const_spec((CB, d)), _const_spec((CB, d)), _const_spec((CB, d)), _const_spec((d, d))],
        out_specs=pl.BlockSpec((tm, d), lambda i: (i, 0)),
        out_shape=jax.ShapeDtypeStruct((t, d), F32),
        compiler_params=_params("parallel"),
        name="merge",
    )(x, mod, y_hy, y_at, y_pl, proj, proj, proj, w_hy_o, w_at_o, w_pl_o, w_out)


def _trunk_layer(x, batch, seq, mod, mod_row0, p, filters, dft, rope, ctx, layer, lam_init):
    mod_rows = seq if mod_row0 else batch * seq
    proj = _in_proj(x, mod, mod_row0, mod_rows, p["norm_g"], p["w_in"])
    y_hy = _hyena_branch(proj, batch, seq, p["hy_conv_w"], p["hy_conv_b"], p["hy_bias"], *filters, *dft)
    q, k, v, k_normed = _attn_prep(proj, seq, p["gq"], p["gk"], rope)
    y_at = _attention(q, k, v, proj, batch, seq, p["lam_q"], p["lam_k"], p["subln_g"], lam_init, ctx, layer)
    y_pl = _pool_branch(proj, batch, seq, p["pool_w"], p["pool_scale"])
    out = _merge(x, mod, mod_row0, mod_rows, proj, y_hy, y_at, y_pl,
                 p["w_hy_o"], p["w_at_o"], p["w_pl_o"], p["w_out"])
    return out, k_normed, proj[:, COL_V * CB:(COL_V + 1) * CB]


def kernel(x_prompt, x_sample, cache_k, cache_v, c, c_ctx, norm_g, w_ada, b_ada, w_in, hy_conv_w, hy_conv_b,
           hy_f_w1, hy_f_b1, hy_f_w2, hy_f_b2, hy_f_w3, hy_freq, hy_bias, q_norm_g, k_norm_g, lam_q, lam_k,
           subln_g, pool_w, pool_scale, w_hy_o, w_at_o, w_pl_o, w_out):
    batch, seq, d = x_prompt.shape
    dec_batch, dec_seq, _ = x_sample.shape
    past = cache_k.shape[2]
    assert d == D_MODEL and dec_batch + 1 <= COND_ROWS and dec_seq % GRID_W == 0

    cond = jnp.concatenate([c_ctx[None, :], c, jnp.zeros((COND_ROWS - 1 - dec_batch, d), F32)], axis=0)
    mod = _modulation(cond, w_ada, b_ada).reshape(DEPTH, COND_ROWS, 1, 3 * d)
    dft = {s: _dft_tables(s) for s in (seq, dec_seq)}
    rope = _rope_tables(dec_seq)
    ctx = (cache_k.reshape(dec_batch, DEPTH, past, D_AT), cache_v.reshape(dec_batch, DEPTH, past, D_AT))

    yp = x_prompt.reshape(batch * seq, d)
    ys = x_sample.reshape(dec_batch * dec_seq, d)
    new_ks, new_vs = [], []
    for l in range(DEPTH):
        p = dict(norm_g=norm_g[l][None, :], w_in=w_in[l].astype(BF16),
                 hy_conv_w=hy_conv_w[l], hy_conv_b=hy_conv_b[l], hy_bias=hy_bias[l],
                 gq=jnp.tile(q_norm_g[l], D_AT // AT_DH)[None, :], gk=jnp.tile(k_norm_g[l], D_AT // AT_DH)[None, :],
                 lam_q=lam_q[l], lam_k=lam_k[l], subln_g=subln_g[l][None, :],
                 pool_w=pool_w[l], pool_scale=pool_scale[l][None, :],
                 w_hy_o=w_hy_o[l].astype(BF16), w_at_o=w_at_o[l].astype(BF16),
                 w_pl_o=w_pl_o[l].astype(BF16), w_out=w_out[l].astype(BF16))
        filters = {s: _hyena_filters(s, hy_f_w1[l], hy_f_b1[l], hy_f_w2[l], hy_f_b2[l], hy_f_w3[l], hy_freq[l],
                                     *dft[s]) for s in (seq, dec_seq)}
        lam_init = 0.8 - 0.6 * math.exp(-0.3 * l)
        yp, k_l, v_l = _trunk_layer(yp, batch, seq, mod[l], 0, p, filters[seq], dft[seq], None, None, l, lam_init)
        new_ks.append(k_l.reshape(batch, seq, AT_HEADS, 2, AT_DH))
        new_vs.append(v_l.reshape(batch, seq, AT_HEADS, AT_DV))
        ys, _, _ = _trunk_layer(ys, dec_batch, dec_seq, mod[l], 1, p, filters[dec_seq], dft[dec_seq], rope, ctx, l,
                                lam_init)
    return (yp.reshape(batch, seq, d), ys.reshape(dec_batch, dec_seq, d),
            jnp.stack(new_ks, axis=1), jnp.stack(new_vs, axis=1))
```

```python
import functools
import math

import jax
import jax.numpy as jnp
from jax import lax
from jax.experimental import pallas as pl
from jax.experimental.pallas import tpu as pltpu

F32 = jnp.float32
BF16 = jnp.bfloat16
HIGHEST = lax.Precision.HIGHEST

D_MODEL = 1024
DEPTH = 2
GRID_W = 64
GRID_W_LOG2 = GRID_W.bit_length() - 1
assert GRID_W == 1 << GRID_W_LOG2
D_HY = 512
HY_ORDER = 2
HY_EMB = 33
HY_BANDS = (HY_EMB - 1) // 2
HY_FFN = 64
HY_FAST_DECAY = 0.3
HY_SLOW_DECAY = 1.5
HY_TARGET = 1e-2
AT_HEADS = 4
AT_DH = 64
AT_DV = 2 * AT_DH
D_AT = AT_HEADS * AT_DV
ROPE_BASE = 10000.0
D_PL = 512
POOL_WINDOWS = (2, 4, 8, 16)
POOL_GC = D_PL // 4
POOL_PAD = 16
NORM_EPS = 1e-6
D_IN = 8192
COND_ROWS = 8

CB = 512
COL_HY_X1, COL_HY_X2, COL_HY_V, COL_HY_Z = 0, 1, 2, 3
COL_Q, COL_K, COL_V, COL_AT_Z, COL_PL_U, COL_PL_Z = 4, 5, 6, 7, 8, 9
COL_MG = 10

VMEM_LIMIT_BYTES = 56 * 1024 * 1024


def _params(*sem):
    return pltpu.CompilerParams(dimension_semantics=sem, vmem_limit_bytes=VMEM_LIMIT_BYTES)


def _const_spec(shape):
    return pl.BlockSpec(shape, lambda *_: (0,) * len(shape), pipeline_mode=pl.Buffered(1))


def _silu(x):
    return x * jax.nn.sigmoid(x)


def _bdot(a, b):
    return jnp.dot(a.astype(BF16), b.astype(BF16), preferred_element_type=F32)


def _fdot(a, b):
    return jnp.dot(a, b, precision=HIGHEST, preferred_element_type=F32)


def _mod_kernel(cond_ref, w_ref, b_ref, o_ref):
    o_ref[...] = _fdot(_silu(cond_ref[...]), w_ref[...]) + b_ref[...]


def _modulation(cond, w_ada, b_ada):
    d = D_MODEL
    return pl.pallas_call(
        _mod_kernel,
        grid=(DEPTH, 3),
        in_specs=[
            pl.BlockSpec((COND_ROWS, d), lambda l, n: (0, 0)),
            pl.BlockSpec((None, d, d), lambda l, n: (l, 0, n)),
            pl.BlockSpec((None, 1, d), lambda l, n: (l, 0, n)),
        ],
        out_specs=pl.BlockSpec((None, COND_ROWS, d), lambda l, n: (l, 0, n)),
        out_shape=jax.ShapeDtypeStruct((DEPTH, COND_ROWS, 3 * d), F32),
        compiler_params=_params("parallel", "parallel"),
        name="modulation",
    )(cond, w_ada, b_ada.reshape(DEPTH, 1, 3 * d))


def _group_mean_matrix():
    r = lax.broadcasted_iota(jnp.int32, (D_AT, D_AT), 0) // AT_DH
    c = lax.broadcasted_iota(jnp.int32, (D_AT, D_AT), 1) // AT_DH
    return jnp.where(r == c, 1.0 / AT_DH, 0.0).astype(BF16)


def _group_rms_norm(x, g, mean_mat):
    x2 = x * x
    hi = x2.astype(BF16)
    lo = (x2 - hi.astype(F32)).astype(BF16)
    ms = (jnp.dot(hi, mean_mat, preferred_element_type=F32)
          + jnp.dot(lo, mean_mat, preferred_element_type=F32))
    return x * lax.rsqrt(ms + NORM_EPS) * g


def _rotate(x, cos, sin_signed):
    lane = lax.broadcasted_iota(jnp.int32, (1, D_AT), 1)
    partner = jnp.where((lane & 1) == 0, pltpu.roll(x, D_AT - 1, 1), pltpu.roll(x, 1, 1))
    return x * cos + partner * sin_signed


def _in_kernel(*refs, rope, keep_kv):
    x_ref, mod_ref, g_ref, w_ref, gq_ref, gk_ref = refs[:6]
    refs = refs[6:]
    if rope:
        cos_ref, sin_ref = refs[:2]
        refs = refs[2:]
    o_ref = refs[0]
    h_ref = refs[-1]

    x = x_ref[...]
    y = x * lax.rsqrt(jnp.mean(x * x, axis=-1, keepdims=True) + NORM_EPS) * g_ref[...]
    h_ref[...] = (y * (1.0 + mod_ref[:, D_MODEL:2 * D_MODEL]) + mod_ref[:, 0:D_MODEL]).astype(BF16)

    def chunk(c):
        return jnp.dot(h_ref[...], w_ref[:, c * CB:(c + 1) * CB], preferred_element_type=F32)

    def put(c, val):
        o_ref[:, c * CB:(c + 1) * CB] = val.astype(o_ref.dtype)

    for c in (COL_HY_X1, COL_HY_X2, COL_HY_V, COL_PL_U):
        put(c, chunk(c))
    for c in (COL_HY_Z, COL_AT_Z, COL_PL_Z):
        put(c, _silu(chunk(c)))
    for c in range(COL_MG, D_IN // CB):
        put(c, jax.nn.sigmoid(chunk(c)))

    mean_mat = _group_mean_matrix()
    q = _group_rms_norm(chunk(COL_Q), gq_ref[...], mean_mat)
    k = _group_rms_norm(chunk(COL_K), gk_ref[...], mean_mat)
    v = chunk(COL_V)
    if keep_kv:
        kv_ref = refs[1]
        kv_ref[:, 0:D_AT] = k
        kv_ref[:, D_AT:2 * D_AT] = v
    if rope:
        q = _rotate(q, cos_ref[...], sin_ref[...])
        k = _rotate(k, cos_ref[...], sin_ref[...])
    put(COL_Q, q * AT_DH ** -0.5)
    put(COL_K, k)
    put(COL_V, v)


def _in_proj(x, mod, mod_row0, rows_per_seq, norm_g, w_in_bf16, gq, gk, rope, seq, keep_kv, tm=256):
    t = x.shape[0]
    blocks_per_seq = rows_per_seq // tm
    vec = pl.BlockSpec((1, D_AT), lambda i: (0, 0))
    in_specs = [pl.BlockSpec((tm, D_MODEL), lambda i: (i, 0)),
                pl.BlockSpec((None, 1, 3 * D_MODEL), lambda i: (mod_row0 + i // blocks_per_seq, 0, 0)),
                pl.BlockSpec((1, D_MODEL), lambda i: (0, 0)),
                _const_spec((D_MODEL, D_IN)), vec, vec]
    args = [x, mod, norm_g, w_in_bf16, gq, gk]
    if rope is not None:
        per_seq = seq // tm
        tab = pl.BlockSpec((tm, D_AT), lambda i: (i % per_seq, 0))
        in_specs += [tab, tab]
        args += list(rope)
    out_specs = [pl.BlockSpec((tm, D_IN), lambda i: (i, 0))]
    out_shape = [jax.ShapeDtypeStruct((t, D_IN), BF16)]
    if keep_kv:
        out_specs.append(pl.BlockSpec((tm, 2 * D_AT), lambda i: (i, 0)))
        out_shape.append(jax.ShapeDtypeStruct((t, 2 * D_AT), F32))
    out = pl.pallas_call(
        functools.partial(_in_kernel, rope=rope is not None, keep_kv=keep_kv),
        grid=(t // tm,),
        in_specs=in_specs,
        out_specs=out_specs,
        out_shape=out_shape,
        scratch_shapes=[pltpu.VMEM((tm, D_MODEL), BF16)],
        compiler_params=_params("parallel"),
        name="in_proj_latent" if rope is not None else "in_proj_ctx",
    )(*args)
    return out if keep_kv else (out[0], None)


def _dft_kernel(c_ref, s_ref, *, seq, tr):
    k = lax.broadcasted_iota(jnp.int32, (tr, seq), 0) + pl.program_id(0) * tr
    n = lax.broadcasted_iota(jnp.int32, (tr, seq), 1)
    ang = ((k * n) & (2 * seq - 1)).astype(F32) * (math.pi / seq)
    c_ref[...] = jnp.cos(ang).astype(BF16)
    s_ref[...] = jnp.sin(ang).astype(BF16)


def _dft_tables(seq):
    tr = min(seq, 256)
    spec = pl.BlockSpec((tr, seq), lambda i: (i, 0))
    shape = jax.ShapeDtypeStruct((seq, seq), BF16)
    return pl.pallas_call(
        functools.partial(_dft_kernel, seq=seq, tr=tr),
        grid=(seq // tr,),
        out_specs=[spec, spec],
        out_shape=[shape, shape],
        compiler_params=_params("parallel"),
        name=f"dft_tables_{seq}",
    )()


FILTER_CB = 256


def _filter_kernel(w1t_ref, w1c_ref, w1s_ref, b1_ref, w2_ref, b2_ref, fq_ref, w3f_ref, w3b_ref,
                   c_ref, s_ref, a_ref, b_ref, nyq_ref, h_ref, *, seq):
    row = lax.broadcasted_iota(jnp.int32, (seq, 1), 0)
    t = row.astype(F32) * (1.0 / (seq - 1))

    @pl.when(pl.program_id(0) == 0)
    def _():
        pos = lax.broadcasted_iota(jnp.int32, (1, seq), 1).astype(F32)
        band = lax.broadcasted_iota(jnp.int32, (HY_BANDS, 1), 0).astype(F32)
        f = 1e-4 + band * ((HY_BANDS - 1 - 1e-4) / (HY_BANDS - 1))
        fw = f * (pos * (2.0 * math.pi / seq))
        pre = (w1t_ref[...] * (pos * (1.0 / (seq - 1))) + _fdot(w1c_ref[...], jnp.cos(fw))
               - _fdot(w1s_ref[...], jnp.sin(fw)) + b1_ref[...])
        h = jnp.sin(fq_ref[:, 0:1] * pre)
        h = jnp.sin(fq_ref[:, 1:2] * (_fdot(w2_ref[...], h) + b2_ref[...]))
        h_ref[...] = h.T

    hf = _fdot(h_ref[...], w3f_ref[...])
    hb = _fdot(h_ref[...], w3b_ref[...])

    chan = lax.broadcasted_iota(jnp.int32, (1, FILTER_CB), 1) + (pl.program_id(0) % (D_HY // FILTER_CB)) * FILTER_CB
    max_decay = math.log(HY_TARGET) / HY_FAST_DECAY
    min_decay = math.log(HY_TARGET) / HY_SLOW_DECAY
    deltas = jnp.abs(min_decay + chan.astype(F32) * ((max_decay - min_decay) / (D_HY - 1)))
    decay = jnp.exp(-t * deltas)
    hf = hf * decay
    hb = jnp.where(row == 0, 0.0, hb * decay)
    inv = 1.0 / (jnp.sum(jnp.abs(hf), axis=0, keepdims=True) + jnp.sum(jnp.abs(hb), axis=0, keepdims=True) + NORM_EPS)
    hs = (hf + hb) * inv
    hd = (hf - hb) * inv
    h_re = _bdot(c_ref[...], hs)
    h_im = -_bdot(s_ref[...], hd)
    alt = (1 - 2 * (row & 1)).astype(F32)
    wk = jnp.where(row == 0, 1.0, 2.0) * (0.5 / seq)
    a_ref[...] = wk * h_re
    b_ref[...] = wk * h_im
    nyq_ref[...] = jnp.sum(alt * hs, axis=0, keepdims=True) * (0.5 / seq)


def _hyena_filters(seq, w1, b1, w2, b2, w3, freq, cmat, smat):
    nc = HY_ORDER * D_HY
    nblk = nc // FILTER_CB
    small = lambda a: pl.BlockSpec(a.shape, lambda j: (0,) * a.ndim)
    w1t, w1c, w1s = w1[0:1].T, w1[1:1 + HY_BANDS].T, w1[1 + HY_BANDS:].T
    b1 = b1.reshape(HY_FFN, 1)
    b2 = b2.reshape(HY_FFN, 1)
    w2 = w2.T
    freq = freq.T
    out_blk = pl.BlockSpec((seq, FILTER_CB), lambda j: (0, j))
    return pl.pallas_call(
        functools.partial(_filter_kernel, seq=seq),
        grid=(nblk,),
        in_specs=[small(w1t), small(w1c), small(w1s), small(b1), small(w2), small(b2), small(freq),
                  pl.BlockSpec((HY_FFN, FILTER_CB), lambda j: (0, j)),
                  pl.BlockSpec((HY_FFN, FILTER_CB), lambda j: (0, nblk + j)),
                  _const_spec((seq, seq)), _const_spec((seq, seq))],
        out_specs=[out_blk, out_blk, pl.BlockSpec((1, FILTER_CB), lambda j: (0, j))],
        out_shape=[jax.ShapeDtypeStruct((seq, nc), F32), jax.ShapeDtypeStruct((seq, nc), F32),
                   jax.ShapeDtypeStruct((1, nc), F32)],
        scratch_shapes=[pltpu.VMEM((seq, HY_FFN), F32)],
        compiler_params=_params("arbitrary"),
        name=f"hyena_filters_{seq}",
    )(w1t, w1c, w1s, b1, w2, b2, freq, w3, w3, cmat, smat)


def _hyena_kernel(x1_ref, x2_ref, v_ref, z_ref, w1_ref, w2_ref, wv_ref, c1_ref, c2_ref, cv_ref,
                  a0_ref, b0_ref, a1_ref, b1_ref, n0_ref, n1_ref, s0_ref, s1_ref, cm_ref, sm_ref,
                  o_ref, u_ref, g_ref, ub_ref, p_ref, q_ref, *, seq, rb):
    row = lax.broadcasted_iota(jnp.int32, (seq, 1), 0)
    alt = (1 - 2 * (row & 1)).astype(F32)

    def short_conv(x_ref, w_ref, c_ref):
        x = x_ref[...].astype(F32)
        prev = jnp.where(row == 0, 0.0, pltpu.roll(x, 1, 0))
        nxt = jnp.where(row == seq - 1, 0.0, pltpu.roll(x, seq - 1, 0))
        return prev * w_ref[0:1, :] + x * w_ref[1:2, :] + nxt * w_ref[2:3, :] + c_ref[...]

    def long_conv(a_ref, b_ref, nyq_ref, skip_ref, finish):
        nyq = jnp.sum(u_ref[...] * alt, axis=0, keepdims=True) * nyq_ref[...]
        for r in range(0, seq, rb):
            rows = pl.ds(r, rb)
            uc = jnp.dot(cm_ref[rows, :], ub_ref[...], preferred_element_type=F32)
            us = jnp.dot(sm_ref[rows, :], ub_ref[...], preferred_element_type=F32)
            a = a_ref[rows, :]
            b = b_ref[rows, :]
            p_ref[rows, :] = (uc * a + us * b).astype(BF16)
            q_ref[rows, :] = (us * a - uc * b).astype(BF16)
        for r in range(0, seq, rb):
            rows = pl.ds(r, rb)
            y = (jnp.dot(cm_ref[rows, :], p_ref[...], preferred_element_type=F32)
                 + jnp.dot(sm_ref[rows, :], q_ref[...], preferred_element_type=F32)
                 + alt[r:r + rb] * nyq + u_ref[rows, :] * skip_ref[...])
            finish(rows, g_ref[rows, :] * y)

    def keep(rows, y):
        u_ref[rows, :] = y
        ub_ref[rows, :] = y.astype(BF16)

    def emit(rows, y):
        o_ref[rows, :] = (y * z_ref[rows, :].astype(F32)).astype(o_ref.dtype)

    v = short_conv(v_ref, wv_ref, cv_ref)
    u_ref[...] = v
    ub_ref[...] = v.astype(BF16)
    g_ref[...] = short_conv(x1_ref, w1_ref, c1_ref)
    long_conv(a0_ref, b0_ref, n0_ref, s0_ref, keep)
    g_ref[...] = short_conv(x2_ref, w2_ref, c2_ref)
    long_conv(a1_ref, b1_ref, n1_ref, s1_ref, emit)


def _hyena_branch(proj, batch, seq, conv_w, conv_b, hy_bias, filt_a, filt_b, filt_nyq, cmat, smat):
    cb = 256 if seq > 512 else D_HY
    rb = min(seq, 512)
    nj = D_HY // cb
    t = batch * seq
    conv_b = conv_b.reshape(1, 3 * D_HY)
    skip = hy_bias.reshape(1, HY_ORDER * D_HY)

    def proj_spec(col512):
        off = col512 * (CB // cb)
        return pl.BlockSpec((seq, cb), lambda j, b: (b, off + j))

    def chan_spec(rows, part):
        off = part * nj
        return pl.BlockSpec((rows, cb), lambda j, b: (0, off + j), pipeline_mode=pl.Buffered(1))

    return pl.pallas_call(
        functools.partial(_hyena_kernel, seq=seq, rb=rb),
        grid=(nj, batch),
        in_specs=[proj_spec(COL_HY_X1), proj_spec(COL_HY_X2), proj_spec(COL_HY_V), proj_spec(COL_HY_Z),
                  chan_spec(3, 0), chan_spec(3, 1), chan_spec(3, 2),
                  chan_spec(1, 0), chan_spec(1, 1), chan_spec(1, 2),
                  chan_spec(seq, 0), chan_spec(seq, 0), chan_spec(seq, 1), chan_spec(seq, 1),
                  chan_spec(1, 0), chan_spec(1, 1), chan_spec(1, 0), chan_spec(1, 1),
                  _const_spec((seq, seq)), _const_spec((seq, seq))],
        out_specs=pl.BlockSpec((seq, cb), lambda j, b: (b, j)),
        out_shape=jax.ShapeDtypeStruct((t, D_HY), BF16),
        scratch_shapes=[pltpu.VMEM((seq, cb), F32), pltpu.VMEM((seq, cb), F32), pltpu.VMEM((seq, cb), BF16),
                        pltpu.VMEM((seq, cb), BF16), pltpu.VMEM((seq, cb), BF16)],
        compiler_params=_params("parallel", "parallel"),
        name=f"hyena_{seq}",
    )(proj, proj, proj, proj, conv_w, conv_w, conv_w, conv_b, conv_b, conv_b,
      filt_a, filt_b, filt_a, filt_b, filt_nyq, filt_nyq, skip, skip, cmat, smat)


def _rope_kernel(cos_ref, sin_ref, *, tr):
    lane = lax.broadcasted_iota(jnp.int32, (1, D_AT), 1)
    dim = lane & (AT_DH - 1)
    pair = ((dim & (AT_DH // 2 - 1)) >> 1).astype(F32)
    inv = jnp.exp(pair * (-(2.0 / (AT_DH // 2)) * math.log(ROPE_BASE)))
    pos = lax.broadcasted_iota(jnp.int32, (tr, 1), 0) + pl.program_id(0) * tr
    coord = jnp.where(dim < AT_DH // 2, pos >> GRID_W_LOG2, pos & (GRID_W - 1)).astype(F32)
    ang = coord * inv
    cos_ref[...] = jnp.cos(ang)
    sin_ref[...] = jnp.where((lane & 1) == 0, -jnp.sin(ang), jnp.sin(ang))


def _rope_tables(seq):
    tr = 256
    spec = pl.BlockSpec((tr, D_AT), lambda i: (i, 0))
    shape = jax.ShapeDtypeStruct((seq, D_AT), F32)
    return pl.pallas_call(
        functools.partial(_rope_kernel, tr=tr),
        grid=(seq // tr,),
        out_specs=[spec, spec],
        out_shape=[shape, shape],
        compiler_params=_params("parallel"),
        name="rope_tables",
    )()


def _nt_dot(a, b):
    return lax.dot_general(a, b, (((1,), (1,)), ((), ())), preferred_element_type=F32)


def _attn_kernel(*refs, lam_init, has_ctx):
    if has_ctx:
        q_ref, k_ref, v_ref, z_ref, lq_ref, lk_ref, sg_ref, ck_ref, cv_ref, o_ref = refs
    else:
        q_ref, k_ref, v_ref, z_ref, lq_ref, lk_ref, sg_ref, o_ref = refs
    lqk = lq_ref[...] * lk_ref[...]
    lam = (jnp.exp(jnp.sum(lqk[0:1, :], axis=-1, keepdims=True))
           - jnp.exp(jnp.sum(lqk[1:2, :], axis=-1, keepdims=True)) + lam_init)

    for h in range(AT_HEADS):
        def softmax_parts(c):
            cols = slice(h * AT_DV + c * AT_DH, h * AT_DV + (c + 1) * AT_DH)
            q = q_ref[:, cols]
            s = _nt_dot(q, k_ref[:, cols])
            m = jnp.max(s, axis=-1, keepdims=True)
            if has_ctx:
                sc = _nt_dot(q, ck_ref[:, cols].astype(BF16))
                m = jnp.maximum(m, jnp.max(sc, axis=-1, keepdims=True))
                ec = jnp.exp(sc - m)
            e = jnp.exp(s - m)
            den = jnp.sum(e, axis=-1, keepdims=True)
            if has_ctx:
                den = den + jnp.sum(ec, axis=-1, keepdims=True)
                return e, ec, 1.0 / den
            return e, None, 1.0 / den

        e1, ec1, r1 = softmax_parts(0)
        e2, ec2, r2 = softmax_parts(1)
        r2 = r2 * lam
        hv = slice(h * AT_DV, (h + 1) * AT_DV)
        o = jnp.dot((e1 * r1 - e2 * r2).astype(BF16), v_ref[:, hv], preferred_element_type=F32)
        if has_ctx:
            o = o + jnp.dot((ec1 * r1 - ec2 * r2).astype(BF16), cv_ref[:, hv].astype(BF16),
                            preferred_element_type=F32)
        o = o * lax.rsqrt(jnp.mean(o * o, axis=-1, keepdims=True) + NORM_EPS) * sg_ref[...]
        o_ref[:, hv] = (o * (1.0 - lam_init) * z_ref[:, hv].astype(F32)).astype(o_ref.dtype)


def _attention(act, batch, seq, lam_q, lam_k, subln_g, lam_init, ctx, layer):
    t = batch * seq
    tq = 256
    nq = seq // tq
    small = lambda a: pl.BlockSpec(a.shape, lambda b, i: (0,) * a.ndim)
    in_specs = [pl.BlockSpec((tq, CB), lambda b, i: (b * nq + i, COL_Q)),
                pl.BlockSpec((seq, CB), lambda b, i: (b, COL_K)),
                pl.BlockSpec((seq, CB), lambda b, i: (b, COL_V)),
                pl.BlockSpec((tq, CB), lambda b, i: (b * nq + i, COL_AT_Z)),
                small(lam_q), small(lam_k), small(subln_g)]
    args = [act, act, act, act, lam_q, lam_k, subln_g]
    if ctx is not None:
        ck, cv = ctx
        past = ck.shape[2]
        cspec = pl.BlockSpec((None, None, past, D_AT), lambda b, i: (b, layer, 0, 0))
        in_specs += [cspec, cspec]
        args += [ck, cv]
    return pl.pallas_call(
        functools.partial(_attn_kernel, lam_init=lam_init, has_ctx=ctx is not None),
        grid=(batch, nq),
        in_specs=in_specs,
        out_specs=pl.BlockSpec((tq, D_AT), lambda b, i: (b * nq + i, 0)),
        out_shape=jax.ShapeDtypeStruct((t, D_AT), BF16),
        compiler_params=_params("parallel", "parallel"),
        name=f"diff_attn_{seq}",
    )(*args)


def _pool_kernel(u_ref, z_ref, w_ref, sc_ref, o_ref, *, seq):
    padded = seq + 2 * POOL_PAD
    t = lax.broadcasted_iota(jnp.int32, (seq, 1), 0)
    zeros = jnp.zeros((POOL_PAD, POOL_GC), F32)
    for g, win in enumerate(POOL_WINDOWS):
        cols = slice(g * POOL_GC, (g + 1) * POOL_GC)
        u = u_ref[:, cols].astype(F32)
        s = jnp.concatenate([zeros, u, zeros], axis=0)
        s = s + pltpu.roll(s, 1, 0)
        half = 1
        while 2 * half < win:
            s = pltpu.roll(s, padded - half, 0) + pltpu.roll(s, half, 0)
            half *= 2
        lo = jnp.maximum(t - win // 2, 0)
        hi = jnp.minimum(t + win // 2, seq)
        m = s[POOL_PAD:POOL_PAD + seq] / (hi - lo).astype(F32) - u
        y = jnp.dot(m.astype(BF16), w_ref[g].astype(BF16), preferred_element_type=F32)
        o_ref[:, cols] = (y * sc_ref[:, cols] * z_ref[:, cols].astype(F32)).astype(o_ref.dtype)


def _pool_branch(proj, batch, seq, pool_w, pool_scale):
    t = batch * seq
    return pl.pallas_call(
        functools.partial(_pool_kernel, seq=seq),
        grid=(batch,),
        in_specs=[pl.BlockSpec((seq, CB), lambda b: (b, COL_PL_U)),
                  pl.BlockSpec((seq, CB), lambda b: (b, COL_PL_Z)),
                  pl.BlockSpec(pool_w.shape, lambda b: (0, 0, 0)),
                  pl.BlockSpec((1, D_PL), lambda b: (0, 0))],
        out_specs=pl.BlockSpec((seq, D_PL), lambda b: (b, 0)),
        out_shape=jax.ShapeDtypeStruct((t, D_PL), BF16),
        compiler_params=_params("parallel"),
        name=f"pool_{seq}",
    )(proj, proj, pool_w, pool_scale)


def _merge_kernel(x_ref, mod_ref, yh_ref, ya_ref, yp_ref, g0_ref, g1_ref, g2_ref,
                  wh_ref, wa_ref, wp_ref, wo_ref, o_ref):
    merged = (g0_ref[...].astype(F32) * jnp.dot(yh_ref[...], wh_ref[...], preferred_element_type=F32)
              + g1_ref[...].astype(F32) * jnp.dot(ya_ref[...], wa_ref[...], preferred_element_type=F32)
              + g2_ref[...].astype(F32) * jnp.dot(yp_ref[...], wp_ref[...], preferred_element_type=F32))
    out = jnp.dot(merged.astype(BF16), wo_ref[...], preferred_element_type=F32)
    o_ref[...] = x_ref[...] + mod_ref[:, 2 * D_MODEL:3 * D_MODEL] * out


def _merge(x, mod, mod_row0, rows_per_seq, proj, y_hy, y_at, y_pl, w_hy_o, w_at_o, w_pl_o, w_out, tm=512):
    t = x.shape[0]
    blocks_per_seq = rows_per_seq // tm
    d = D_MODEL
    branch = pl.BlockSpec((tm, CB), lambda i: (i, 0))
    gate = lambda g: pl.BlockSpec((tm, d), lambda i: (i, COL_MG // 2 + g))
    return pl.pallas_call(
        _merge_kernel,
        grid=(t // tm,),
        in_specs=[pl.BlockSpec((tm, d), lambda i: (i, 0)),
                  pl.BlockSpec((None, 1, 3 * d), lambda i: (mod_row0 + i // blocks_per_seq, 0, 0)),
                  branch, branch, branch, gate(0), gate(1), gate(2),
                  _const_spec((CB, d)), _const_spec((CB, d)), _const_spec((CB, d)), _const_spec((d, d))],
        out_specs=pl.BlockSpec((tm, d), lambda i: (i, 0)),
        out_shape=jax.ShapeDtypeStruct((t, d), F32),
        compiler_params=_params("parallel"),
        name="merge",
    )(x, mod, y_hy, y_at, y_pl, proj, proj, proj, w_hy_o, w_at_o, w_pl_o, w_out)


def _trunk_layer(x, batch, seq, mod, mod_row0, p, filters, dft, rope, ctx, layer, lam_init):
    mod_rows = seq if mod_row0 else batch * seq
    act, kv = _in_proj(x, mod, mod_row0, mod_rows, p["norm_g"], p["w_in"], p["gq"], p["gk"], rope, seq,
                       keep_kv=rope is None)
    y_hy = _hyena_branch(act, batch, seq, p["hy_conv_w"], p["hy_conv_b"], p["hy_bias"], *filters, *dft)
    y_at = _attention(act, batch, seq, p["lam_q"], p["lam_k"], p["subln_g"], lam_init, ctx, layer)
    y_pl = _pool_branch(act, batch, seq, p["pool_w"], p["pool_scale"])
    out = _merge(x, mod, mod_row0, mod_rows, act, y_hy, y_at, y_pl,
                 p["w_hy_o"], p["w_at_o"], p["w_pl_o"], p["w_out"])
    return out, kv


def kernel(x_prompt, x_sample, cache_k, cache_v, c, c_ctx, norm_g, w_ada, b_ada, w_in, hy_conv_w, hy_conv_b,
           hy_f_w1, hy_f_b1, hy_f_w2, hy_f_b2, hy_f_w3, hy_freq, hy_bias, q_norm_g, k_norm_g, lam_q, lam_k,
           subln_g, pool_w, pool_scale, w_hy_o, w_at_o, w_pl_o, w_out):
    batch, seq, d = x_prompt.shape
    dec_batch, dec_seq, _ = x_sample.shape
    past = cache_k.shape[2]
    assert d == D_MODEL and dec_batch + 1 <= COND_ROWS and dec_seq % GRID_W == 0

    cond = jnp.concatenate([c_ctx[None, :], c, jnp.zeros((COND_ROWS - 1 - dec_batch, d), F32)], axis=0)
    mod = _modulation(cond, w_ada, b_ada).reshape(DEPTH, COND_ROWS, 1, 3 * d)
    dft = {s: _dft_tables(s) for s in (seq, dec_seq)}
    rope = _rope_tables(dec_seq)
    ctx = (cache_k.reshape(dec_batch, DEPTH, past, D_AT), cache_v.reshape(dec_batch, DEPTH, past, D_AT))

    yp = x_prompt.reshape(batch * seq, d)
    ys = x_sample.reshape(dec_batch * dec_seq, d)
    new_ks, new_vs = [], []
    for l in range(DEPTH):
        p = dict(norm_g=norm_g[l][None, :], w_in=w_in[l].astype(BF16),
                 hy_conv_w=hy_conv_w[l], hy_conv_b=hy_conv_b[l], hy_bias=hy_bias[l],
                 gq=jnp.tile(q_norm_g[l], D_AT // AT_DH)[None, :], gk=jnp.tile(k_norm_g[l], D_AT // AT_DH)[None, :],
                 lam_q=lam_q[l], lam_k=lam_k[l], subln_g=subln_g[l][None, :],
                 pool_w=pool_w[l], pool_scale=pool_scale[l][None, :],
                 w_hy_o=w_hy_o[l].astype(BF16), w_at_o=w_at_o[l].astype(BF16),
                 w_pl_o=w_pl_o[l].astype(BF16), w_out=w_out[l].astype(BF16))
        filters = {s: _hyena_filters(s, hy_f_w1[l], hy_f_b1[l], hy_f_w2[l], hy_f_b2[l], hy_f_w3[l], hy_freq[l],
                                     *dft[s]) for s in (seq, dec_seq)}
        lam_init = 0.8 - 0.6 * math.exp(-0.3 * l)
        yp, kv = _trunk_layer(yp, batch, seq, mod[l], 0, p, filters[seq], dft[seq], None, None, l, lam_init)
        new_ks.append(kv[:, :D_AT].reshape(batch, seq, AT_HEADS, 2, AT_DH))
        new_vs.append(kv[:, D_AT:].reshape(batch, seq, AT_HEADS, AT_DV))
        ys, _ = _trunk_layer(ys, dec_batch, dec_seq, mod[l], 1, p, filters[dec_seq], dft[dec_seq], rope, ctx, l,
                             lam_init)
    return (yp.reshape(batch, seq, d), ys.reshape(dec_batch, dec_seq, d),
            jnp.stack(new_ks, axis=1), jnp.stack(new_vs, axis=1))
```

```python
import functools
import math

import jax
import jax.numpy as jnp
from jax import lax
from jax.experimental import pallas as pl
from jax.experimental.pallas import tpu as pltpu

F32 = jnp.float32
BF16 = jnp.bfloat16
HIGHEST = lax.Precision.HIGHEST

D_MODEL = 1024
DEPTH = 2
GRID_W = 64
GRID_W_LOG2 = GRID_W.bit_length() - 1
assert GRID_W == 1 << GRID_W_LOG2
D_HY = 512
HY_ORDER = 2
HY_EMB = 33
HY_BANDS = (HY_EMB - 1) // 2
HY_FFN = 64
HY_FAST_DECAY = 0.3
HY_SLOW_DECAY = 1.5
HY_TARGET = 1e-2
AT_HEADS = 4
AT_DH = 64
AT_DV = 2 * AT_DH
D_AT = AT_HEADS * AT_DV
ROPE_BASE = 10000.0
D_PL = 512
POOL_WINDOWS = (2, 4, 8, 16)
POOL_GC = D_PL // 4
POOL_PAD = 16
NORM_EPS = 1e-6
D_IN = 8192
COND_ROWS = 8

CB = 512
COL_HY_X1, COL_HY_X2, COL_HY_V, COL_HY_Z = 0, 1, 2, 3
COL_Q, COL_K, COL_V, COL_AT_Z, COL_PL_U, COL_PL_Z = 4, 5, 6, 7, 8, 9
COL_MG = 10

VMEM_LIMIT_BYTES = 56 * 1024 * 1024


def _params(*sem):
    return pltpu.CompilerParams(dimension_semantics=sem, vmem_limit_bytes=VMEM_LIMIT_BYTES)


def _const_spec(shape):
    return pl.BlockSpec(shape, lambda *_: (0,) * len(shape), pipeline_mode=pl.Buffered(1))


def _silu(x):
    return x * jax.nn.sigmoid(x)


def _bdot(a, b):
    return jnp.dot(a.astype(BF16), b.astype(BF16), preferred_element_type=F32)


def _fdot(a, b):
    return jnp.dot(a, b, precision=HIGHEST, preferred_element_type=F32)


def _mod_kernel(cond_ref, w_ref, b_ref, o_ref):
    o_ref[...] = _fdot(_silu(cond_ref[...]), w_ref[...]) + b_ref[...]


def _modulation(cond, w_ada, b_ada):
    d = D_MODEL
    return pl.pallas_call(
        _mod_kernel,
        grid=(DEPTH, 3),
        in_specs=[
            pl.BlockSpec((COND_ROWS, d), lambda l, n: (0, 0)),
            pl.BlockSpec((None, d, d), lambda l, n: (l, 0, n)),
            pl.BlockSpec((None, 1, d), lambda l, n: (l, 0, n)),
        ],
        out_specs=pl.BlockSpec((None, COND_ROWS, d), lambda l, n: (l, 0, n)),
        out_shape=jax.ShapeDtypeStruct((DEPTH, COND_ROWS, 3 * d), F32),
        compiler_params=_params("parallel", "parallel"),
        name="modulation",
    )(cond, w_ada, b_ada.reshape(DEPTH, 1, 3 * d))


def _group_mean_matrix():
    r = lax.broadcasted_iota(jnp.int32, (D_AT, D_AT), 0) // AT_DH
    c = lax.broadcasted_iota(jnp.int32, (D_AT, D_AT), 1) // AT_DH
    return jnp.where(r == c, 1.0 / AT_DH, 0.0).astype(BF16)


def _group_rms_norm(x, g, mean_mat):
    x2 = x * x
    hi = x2.astype(BF16)
    lo = (x2 - hi.astype(F32)).astype(BF16)
    ms = (jnp.dot(hi, mean_mat, preferred_element_type=F32)
          + jnp.dot(lo, mean_mat, preferred_element_type=F32))
    return x * lax.rsqrt(ms + NORM_EPS) * g


def _rotate(x, cos, sin_signed):
    lane = lax.broadcasted_iota(jnp.int32, (1, D_AT), 1)
    partner = jnp.where((lane & 1) == 0, pltpu.roll(x, D_AT - 1, 1), pltpu.roll(x, 1, 1))
    return x * cos + partner * sin_signed


def _in_kernel(*refs, rope, keep_kv):
    x_ref, mod_ref, g_ref, w_ref, gq_ref, gk_ref = refs[:6]
    refs = refs[6:]
    if rope:
        cos_ref, sin_ref = refs[:2]
        refs = refs[2:]
    o_ref = refs[0]
    h_ref = refs[-1]

    x = x_ref[...]
    y = x * lax.rsqrt(jnp.mean(x * x, axis=-1, keepdims=True) + NORM_EPS) * g_ref[...]
    h_ref[...] = (y * (1.0 + mod_ref[:, D_MODEL:2 * D_MODEL]) + mod_ref[:, 0:D_MODEL]).astype(BF16)

    def chunk(c):
        return jnp.dot(h_ref[...], w_ref[:, c * CB:(c + 1) * CB], preferred_element_type=F32)

    def put(c, val):
        o_ref[:, c * CB:(c + 1) * CB] = val.astype(o_ref.dtype)

    for c in (COL_HY_X1, COL_HY_X2, COL_HY_V, COL_PL_U):
        put(c, chunk(c))
    for c in (COL_HY_Z, COL_AT_Z, COL_PL_Z):
        put(c, _silu(chunk(c)))
    for c in range(COL_MG, D_IN // CB):
        put(c, jax.nn.sigmoid(chunk(c)))

    mean_mat = _group_mean_matrix()
    q = _group_rms_norm(chunk(COL_Q), gq_ref[...], mean_mat)
    k = _group_rms_norm(chunk(COL_K), gk_ref[...], mean_mat)
    v = chunk(COL_V)
    if keep_kv:
        kv_ref = refs[1]
        kv_ref[:, 0:D_AT] = k
        kv_ref[:, D_AT:2 * D_AT] = v
    if rope:
        q = _rotate(q, cos_ref[...], sin_ref[...])
        k = _rotate(k, cos_ref[...], sin_ref[...])
    put(COL_Q, q * AT_DH ** -0.5)
    put(COL_K, k)
    put(COL_V, v)


def _in_proj(x, mod, mod_row0, rows_per_seq, norm_g, w_in_bf16, gq, gk, rope, seq, keep_kv, tm=256):
    t = x.shape[0]
    blocks_per_seq = rows_per_seq // tm
    vec = pl.BlockSpec((1, D_AT), lambda i: (0, 0))
    in_specs = [pl.BlockSpec((tm, D_MODEL), lambda i: (i, 0)),
                pl.BlockSpec((None, 1, 3 * D_MODEL), lambda i: (mod_row0 + i // blocks_per_seq, 0, 0)),
                pl.BlockSpec((1, D_MODEL), lambda i: (0, 0)),
                _const_spec((D_MODEL, D_IN)), vec, vec]
    args = [x, mod, norm_g, w_in_bf16, gq, gk]
    if rope is not None:
        per_seq = seq // tm
        tab = pl.BlockSpec((tm, D_AT), lambda i: (i % per_seq, 0))
        in_specs += [tab, tab]
        args += list(rope)
    out_specs = [pl.BlockSpec((tm, D_IN), lambda i: (i, 0))]
    out_shape = [jax.ShapeDtypeStruct((t, D_IN), BF16)]
    if keep_kv:
        out_specs.append(pl.BlockSpec((tm, 2 * D_AT), lambda i: (i, 0)))
        out_shape.append(jax.ShapeDtypeStruct((t, 2 * D_AT), F32))
    out = pl.pallas_call(
        functools.partial(_in_kernel, rope=rope is not None, keep_kv=keep_kv),
        grid=(t // tm,),
        in_specs=in_specs,
        out_specs=out_specs,
        out_shape=out_shape,
        scratch_shapes=[pltpu.VMEM((tm, D_MODEL), BF16)],
        compiler_params=_params("parallel"),
        name="in_proj_latent" if rope is not None else "in_proj_ctx",
    )(*args)
    return out if keep_kv else (out[0], None)


def _dft_kernel(c_ref, s_ref, *, seq, tr):
    k = lax.broadcasted_iota(jnp.int32, (tr, seq), 0) + pl.program_id(0) * tr
    n = lax.broadcasted_iota(jnp.int32, (tr, seq), 1)
    ang = ((k * n) & (2 * seq - 1)).astype(F32) * (math.pi / seq)
    c_ref[...] = jnp.cos(ang).astype(BF16)
    s_ref[...] = jnp.sin(ang).astype(BF16)


def _dft_tables(seq):
    tr = min(seq, 256)
    spec = pl.BlockSpec((tr, seq), lambda i: (i, 0))
    shape = jax.ShapeDtypeStruct((seq, seq), BF16)
    return pl.pallas_call(
        functools.partial(_dft_kernel, seq=seq, tr=tr),
        grid=(seq // tr,),
        out_specs=[spec, spec],
        out_shape=[shape, shape],
        compiler_params=_params("parallel"),
        name=f"dft_tables_{seq}",
    )()


FILTER_CB = 256


def _filter_kernel(w1t_ref, w1c_ref, w1s_ref, b1_ref, w2_ref, b2_ref, fq_ref, w3f_ref, w3b_ref,
                   c_ref, s_ref, a_ref, b_ref, nyq_ref, h_ref, *, seq):
    row = lax.broadcasted_iota(jnp.int32, (seq, 1), 0)
    t = row.astype(F32) * (1.0 / (seq - 1))

    @pl.when(pl.program_id(0) == 0)
    def _():
        pos = lax.broadcasted_iota(jnp.int32, (1, seq), 1).astype(F32)
        band = lax.broadcasted_iota(jnp.int32, (HY_BANDS, 1), 0).astype(F32)
        f = 1e-4 + band * ((HY_BANDS - 1 - 1e-4) / (HY_BANDS - 1))
        fw = f * (pos * (2.0 * math.pi / seq))
        pre = (w1t_ref[...] * (pos * (1.0 / (seq - 1))) + _fdot(w1c_ref[...], jnp.cos(fw))
               - _fdot(w1s_ref[...], jnp.sin(fw)) + b1_ref[...])
        h = jnp.sin(fq_ref[:, 0:1] * pre)
        h = jnp.sin(fq_ref[:, 1:2] * (_fdot(w2_ref[...], h) + b2_ref[...]))
        h_ref[...] = h.T

    hf = _fdot(h_ref[...], w3f_ref[...])
    hb = _fdot(h_ref[...], w3b_ref[...])

    chan = lax.broadcasted_iota(jnp.int32, (1, FILTER_CB), 1) + (pl.program_id(0) % (D_HY // FILTER_CB)) * FILTER_CB
    max_decay = math.log(HY_TARGET) / HY_FAST_DECAY
    min_decay = math.log(HY_TARGET) / HY_SLOW_DECAY
    deltas = jnp.abs(min_decay + chan.astype(F32) * ((max_decay - min_decay) / (D_HY - 1)))
    decay = jnp.exp(-t * deltas)
    hf = hf * decay
    hb = jnp.where(row == 0, 0.0, hb * decay)
    inv = 1.0 / (jnp.sum(jnp.abs(hf), axis=0, keepdims=True) + jnp.sum(jnp.abs(hb), axis=0, keepdims=True) + NORM_EPS)
    hs = (hf + hb) * inv
    hd = (hf - hb) * inv
    h_re = _bdot(c_ref[...], hs)
    h_im = -_bdot(s_ref[...], hd)
    alt = (1 - 2 * (row & 1)).astype(F32)
    wk = jnp.where(row == 0, 1.0, 2.0) * (0.5 / seq)
    a_ref[...] = wk * h_re
    b_ref[...] = wk * h_im
    nyq_ref[...] = jnp.sum(alt * hs, axis=0, keepdims=True) * (0.5 / seq)


def _hyena_filters(seq, w1, b1, w2, b2, w3, freq, cmat, smat):
    nc = HY_ORDER * D_HY
    nblk = nc // FILTER_CB
    small = lambda a: pl.BlockSpec(a.shape, lambda j: (0,) * a.ndim)
    w1t, w1c, w1s = w1[0:1].T, w1[1:1 + HY_BANDS].T, w1[1 + HY_BANDS:].T
    b1 = b1.reshape(HY_FFN, 1)
    b2 = b2.reshape(HY_FFN, 1)
    w2 = w2.T
    freq = freq.T
    out_blk = pl.BlockSpec((seq, FILTER_CB), lambda j: (0, j))
    return pl.pallas_call(
        functools.partial(_filter_kernel, seq=seq),
        grid=(nblk,),
        in_specs=[small(w1t), small(w1c), small(w1s), small(b1), small(w2), small(b2), small(freq),
                  pl.BlockSpec((HY_FFN, FILTER_CB), lambda j: (0, j)),
                  pl.BlockSpec((HY_FFN, FILTER_CB), lambda j: (0, nblk + j)),
                  _const_spec((seq, seq)), _const_spec((seq, seq))],
        out_specs=[out_blk, out_blk, pl.BlockSpec((1, FILTER_CB), lambda j: (0, j))],
        out_shape=[jax.ShapeDtypeStruct((seq, nc), F32), jax.ShapeDtypeStruct((seq, nc), F32),
                   jax.ShapeDtypeStruct((1, nc), F32)],
        scratch_shapes=[pltpu.VMEM((seq, HY_FFN), F32)],
        compiler_params=_params("arbitrary"),
        name=f"hyena_filters_{seq}",
    )(w1t, w1c, w1s, b1, w2, b2, freq, w3, w3, cmat, smat)


def _hyena_kernel(x1_ref, x2_ref, v_ref, z_ref, w1_ref, w2_ref, wv_ref, c1_ref, c2_ref, cv_ref,
                  a0_ref, b0_ref, a1_ref, b1_ref, n0_ref, n1_ref, s0_ref, s1_ref, cm_ref, sm_ref,
                  o_ref, u_ref, g_ref, ub_ref, p_ref, q_ref, *, seq, rb):
    row = lax.broadcasted_iota(jnp.int32, (seq, 1), 0)
    alt = (1 - 2 * (row & 1)).astype(F32)

    def short_conv(x_ref, w_ref, c_ref):
        x = x_ref[...].astype(F32)
        prev = jnp.where(row == 0, 0.0, pltpu.roll(x, 1, 0))
        nxt = jnp.where(row == seq - 1, 0.0, pltpu.roll(x, seq - 1, 0))
        return prev * w_ref[0:1, :] + x * w_ref[1:2, :] + nxt * w_ref[2:3, :] + c_ref[...]

    def long_conv(a_ref, b_ref, nyq_ref, skip_ref, finish):
        nyq = jnp.sum(u_ref[...] * alt, axis=0, keepdims=True) * nyq_ref[...]
        for r in range(0, seq, rb):
            rows = pl.ds(r, rb)
            uc = jnp.dot(cm_ref[rows, :], ub_ref[...], preferred_element_type=F32)
            us = jnp.dot(sm_ref[rows, :], ub_ref[...], preferred_element_type=F32)
            a = a_ref[rows, :]
            b = b_ref[rows, :]
            p_ref[rows, :] = (uc * a + us * b).astype(BF16)
            q_ref[rows, :] = (us * a - uc * b).astype(BF16)
        for r in range(0, seq, rb):
            rows = pl.ds(r, rb)
            y = (jnp.dot(cm_ref[rows, :], p_ref[...], preferred_element_type=F32)
                 + jnp.dot(sm_ref[rows, :], q_ref[...], preferred_element_type=F32)
                 + alt[r:r + rb] * nyq + u_ref[rows, :] * skip_ref[...])
            finish(rows, g_ref[rows, :] * y)

    def keep(rows, y):
        u_ref[rows, :] = y
        ub_ref[rows, :] = y.astype(BF16)

    def emit(rows, y):
        o_ref[rows, :] = (y * z_ref[rows, :].astype(F32)).astype(o_ref.dtype)

    v = short_conv(v_ref, wv_ref, cv_ref)
    u_ref[...] = v
    ub_ref[...] = v.astype(BF16)
    g_ref[...] = short_conv(x1_ref, w1_ref, c1_ref)
    long_conv(a0_ref, b0_ref, n0_ref, s0_ref, keep)
    g_ref[...] = short_conv(x2_ref, w2_ref, c2_ref)
    long_conv(a1_ref, b1_ref, n1_ref, s1_ref, emit)


def _hyena_branch(proj, batch, seq, conv_w, conv_b, hy_bias, filt_a, filt_b, filt_nyq, cmat, smat):
    cb = 256 if seq > 512 else D_HY
    rb = min(seq, 512)
    nj = D_HY // cb
    t = batch * seq
    conv_b = conv_b.reshape(1, 3 * D_HY)
    skip = hy_bias.reshape(1, HY_ORDER * D_HY)

    def proj_spec(col512):
        off = col512 * (CB // cb)
        return pl.BlockSpec((seq, cb), lambda j, b: (b, off + j))

    def chan_spec(rows, part):
        off = part * nj
        return pl.BlockSpec((rows, cb), lambda j, b: (0, off + j), pipeline_mode=pl.Buffered(1))

    return pl.pallas_call(
        functools.partial(_hyena_kernel, seq=seq, rb=rb),
        grid=(nj, batch),
        in_specs=[proj_spec(COL_HY_X1), proj_spec(COL_HY_X2), proj_spec(COL_HY_V), proj_spec(COL_HY_Z),
                  chan_spec(3, 0), chan_spec(3, 1), chan_spec(3, 2),
                  chan_spec(1, 0), chan_spec(1, 1), chan_spec(1, 2),
                  chan_spec(seq, 0), chan_spec(seq, 0), chan_spec(seq, 1), chan_spec(seq, 1),
                  chan_spec(1, 0), chan_spec(1, 1), chan_spec(1, 0), chan_spec(1, 1),
                  _const_spec((seq, seq)), _const_spec((seq, seq))],
        out_specs=pl.BlockSpec((seq, cb), lambda j, b: (b, j)),
        out_shape=jax.ShapeDtypeStruct((t, D_HY), BF16),
        scratch_shapes=[pltpu.VMEM((seq, cb), F32), pltpu.VMEM((seq, cb), F32), pltpu.VMEM((seq, cb), BF16),
                        pltpu.VMEM((seq, cb), BF16), pltpu.VMEM((seq, cb), BF16)],
        compiler_params=_params("parallel", "parallel"),
        name=f"hyena_{seq}",
    )(proj, proj, proj, proj, conv_w, conv_w, conv_w, conv_b, conv_b, conv_b,
      filt_a, filt_b, filt_a, filt_b, filt_nyq, filt_nyq, skip, skip, cmat, smat)


def _rope_kernel(cos_ref, sin_ref, *, tr):
    lane = lax.broadcasted_iota(jnp.int32, (1, D_AT), 1)
    dim = lane & (AT_DH - 1)
    pair = ((dim & (AT_DH // 2 - 1)) >> 1).astype(F32)
    inv = jnp.exp(pair * (-(2.0 / (AT_DH // 2)) * math.log(ROPE_BASE)))
    pos = lax.broadcasted_iota(jnp.int32, (tr, 1), 0) + pl.program_id(0) * tr
    coord = jnp.where(dim < AT_DH // 2, pos >> GRID_W_LOG2, pos & (GRID_W - 1)).astype(F32)
    ang = coord * inv
    cos_ref[...] = jnp.cos(ang)
    sin_ref[...] = jnp.where((lane & 1) == 0, -jnp.sin(ang), jnp.sin(ang))


def _rope_tables(seq):
    tr = 256
    spec = pl.BlockSpec((tr, D_AT), lambda i: (i, 0))
    shape = jax.ShapeDtypeStruct((seq, D_AT), F32)
    return pl.pallas_call(
        functools.partial(_rope_kernel, tr=tr),
        grid=(seq // tr,),
        out_specs=[spec, spec],
        out_shape=[shape, shape],
        compiler_params=_params("parallel"),
        name="rope_tables",
    )()


def _nt_dot(a, b):
    return lax.dot_general(a, b, (((1,), (1,)), ((), ())), preferred_element_type=F32)


ONES_ROWS = 8


def _attn_kernel(*refs, lam_init, has_ctx, seq):
    if has_ctx:
        q_ref, k_ref, v_ref, z_ref, lq_ref, lk_ref, sg_ref, ck_ref, cv_ref, o_ref, vt_ref, kc_ref = refs
    else:
        q_ref, k_ref, v_ref, z_ref, lq_ref, lk_ref, sg_ref, o_ref, vt_ref = refs

    @pl.when(pl.program_id(1) == 0)
    def _():
        vt_ref[0:D_AT, 0:seq] = v_ref[...].astype(F32).T.astype(BF16)
        vt_ref[D_AT:D_AT + ONES_ROWS, :] = jnp.ones((ONES_ROWS, vt_ref.shape[1]), BF16)
        if has_ctx:
            vt_ref[0:D_AT, seq:] = cv_ref[...].T.astype(BF16)
            kc_ref[...] = ck_ref[...].astype(BF16)

    lqk = lq_ref[...] * lk_ref[...]
    lam = (jnp.exp(jnp.sum(lqk[0:1, :], axis=-1, keepdims=True))
           - jnp.exp(jnp.sum(lqk[1:2, :], axis=-1, keepdims=True)) + lam_init)
    ones = vt_ref[D_AT:D_AT + ONES_ROWS, :]

    for h in range(AT_HEADS):
        hv = slice(h * AT_DV, (h + 1) * AT_DV)

        def unnormalised(c):
            cols = slice(h * AT_DV + c * AT_DH, h * AT_DV + (c + 1) * AT_DH)
            q = q_ref[:, cols]
            s = _nt_dot(k_ref[:, cols], q)
            m = jnp.max(s, axis=0, keepdims=True)
            if has_ctx:
                sc = _nt_dot(kc_ref[:, cols], q)
                m = jnp.maximum(m, jnp.max(sc, axis=0, keepdims=True))
            e = jnp.exp(s - m).astype(BF16)
            o = jnp.dot(vt_ref[hv, 0:seq], e, preferred_element_type=F32)
            den = jnp.dot(ones[:, 0:seq], e, preferred_element_type=F32)
            if has_ctx:
                ec = jnp.exp(sc - m).astype(BF16)
                o = o + jnp.dot(vt_ref[hv, seq:], ec, preferred_element_type=F32)
                den = den + jnp.dot(ones[:, seq:], ec, preferred_element_type=F32)
            return o, den[0:1, :]

        o1, den1 = unnormalised(0)
        o2, den2 = unnormalised(1)
        o = o1 * (1.0 / den1) - o2 * (lam / den2)
        o = o * lax.rsqrt(jnp.mean(o * o, axis=0, keepdims=True) + NORM_EPS)
        o = o.T * sg_ref[...] * (1.0 - lam_init)
        o_ref[:, hv] = (o * z_ref[:, hv].astype(F32)).astype(o_ref.dtype)


def _attention(act, batch, seq, lam_q, lam_k, subln_g, lam_init, ctx, layer):
    t = batch * seq
    tq = 256
    nq = seq // tq
    small = lambda a: pl.BlockSpec(a.shape, lambda b, i: (0,) * a.ndim)
    in_specs = [pl.BlockSpec((tq, CB), lambda b, i: (b * nq + i, COL_Q)),
                pl.BlockSpec((seq, CB), lambda b, i: (b, COL_K)),
                pl.BlockSpec((seq, CB), lambda b, i: (b, COL_V)),
                pl.BlockSpec((tq, CB), lambda b, i: (b * nq + i, COL_AT_Z)),
                small(lam_q), small(lam_k), small(subln_g)]
    args = [act, act, act, act, lam_q, lam_k, subln_g]
    past = 0
    scratch = []
    if ctx is not None:
        ck, cv = ctx
        past = ck.shape[2]
        cspec = pl.BlockSpec((None, None, past, D_AT), lambda b, i: (b, layer, 0, 0))
        in_specs += [cspec, cspec]
        args += [ck, cv]
        scratch = [pltpu.VMEM((past, D_AT), BF16)]
    return pl.pallas_call(
        functools.partial(_attn_kernel, lam_init=lam_init, has_ctx=ctx is not None, seq=seq),
        grid=(batch, nq),
        in_specs=in_specs,
        out_specs=pl.BlockSpec((tq, D_AT), lambda b, i: (b * nq + i, 0)),
        out_shape=jax.ShapeDtypeStruct((t, D_AT), BF16),
        scratch_shapes=[pltpu.VMEM((D_AT + ONES_ROWS, seq + past), BF16)] + scratch,
        compiler_params=_params("parallel", "arbitrary"),
        name=f"diff_attn_{seq}",
    )(*args)


def _pool_kernel(u_ref, z_ref, w_ref, sc_ref, o_ref, *, seq):
    padded = seq + 2 * POOL_PAD
    t = lax.broadcasted_iota(jnp.int32, (seq, 1), 0)
    zeros = jnp.zeros((POOL_PAD, POOL_GC), F32)
    for g, win in enumerate(POOL_WINDOWS):
        cols = slice(g * POOL_GC, (g + 1) * POOL_GC)
        u = u_ref[:, cols].astype(F32)
        s = jnp.concatenate([zeros, u, zeros], axis=0)
        s = s + pltpu.roll(s, 1, 0)
        half = 1
        while 2 * half < win:
            s = pltpu.roll(s, padded - half, 0) + pltpu.roll(s, half, 0)
            half *= 2
        lo = jnp.maximum(t - win // 2, 0)
        hi = jnp.minimum(t + win // 2, seq)
        m = s[POOL_PAD:POOL_PAD + seq] / (hi - lo).astype(F32) - u
        y = jnp.dot(m.astype(BF16), w_ref[g].astype(BF16), preferred_element_type=F32)
        o_ref[:, cols] = (y * sc_ref[:, cols] * z_ref[:, cols].astype(F32)).astype(o_ref.dtype)


def _pool_branch(proj, batch, seq, pool_w, pool_scale):
    t = batch * seq
    return pl.pallas_call(
        functools.partial(_pool_kernel, seq=seq),
        grid=(batch,),
        in_specs=[pl.BlockSpec((seq, CB), lambda b: (b, COL_PL_U)),
                  pl.BlockSpec((seq, CB), lambda b: (b, COL_PL_Z)),
                  pl.BlockSpec(pool_w.shape, lambda b: (0, 0, 0)),
                  pl.BlockSpec((1, D_PL), lambda b: (0, 0))],
        out_specs=pl.BlockSpec((seq, D_PL), lambda b: (b, 0)),
        out_shape=jax.ShapeDtypeStruct((t, D_PL), BF16),
        compiler_params=_params("parallel"),
        name=f"pool_{seq}",
    )(proj, proj, pool_w, pool_scale)


def _merge_kernel(x_ref, mod_ref, yh_ref, ya_ref, yp_ref, g0_ref, g1_ref, g2_ref,
                  wh_ref, wa_ref, wp_ref, wo_ref, o_ref):
    merged = (g0_ref[...].astype(F32) * jnp.dot(yh_ref[...], wh_ref[...], preferred_element_type=F32)
              + g1_ref[...].astype(F32) * jnp.dot(ya_ref[...], wa_ref[...], preferred_element_type=F32)
              + g2_ref[...].astype(F32) * jnp.dot(yp_ref[...], wp_ref[...], preferred_element_type=F32))
    out = jnp.dot(merged.astype(BF16), wo_ref[...], preferred_element_type=F32)
    o_ref[...] = x_ref[...] + mod_ref[:, 2 * D_MODEL:3 * D_MODEL] * out


def _merge(x, mod, mod_row0, rows_per_seq, proj, y_hy, y_at, y_pl, w_hy_o, w_at_o, w_pl_o, w_out, tm=512):
    t = x.shape[0]
    blocks_per_seq = rows_per_seq // tm
    d = D_MODEL
    branch = pl.BlockSpec((tm, CB), lambda i: (i, 0))
    gate = lambda g: pl.BlockSpec((tm, d), lambda i: (i, COL_MG // 2 + g))
    return pl.pallas_call(
        _merge_kernel,
        grid=(t // tm,),
        in_specs=[pl.BlockSpec((tm, d), lambda i: (i, 0)),
                  pl.BlockSpec((None, 1, 3 * d), lambda i: (mod_row0 + i // blocks_per_seq, 0, 0)),
                  branch, branch, branch, gate(0), gate(1), gate(2),
                  _const_spec((CB, d)), _const_spec((CB, d)), _const_spec((CB, d)), _const_spec((d, d))],
        out_specs=pl.BlockSpec((tm, d), lambda i: (i, 0)),
        out_shape=jax.ShapeDtypeStruct((t, d), F32),
        compiler_params=_params("parallel"),
        name="merge",
    )(x, mod, y_hy, y_at, y_pl, proj, proj, proj, w_hy_o, w_at_o, w_pl_o, w_out)


def _trunk_layer(x, batch, seq, mod, mod_row0, p, filters, dft, rope, ctx, layer, lam_init):
    mod_rows = seq if mod_row0 else batch * seq
    act, kv = _in_proj(x, mod, mod_row0, mod_rows, p["norm_g"], p["w_in"], p["gq"], p["gk"], rope, seq,
                       keep_kv=rope is None)
    y_hy = _hyena_branch(act, batch, seq, p["hy_conv_w"], p["hy_conv_b"], p["hy_bias"], *filters, *dft)
    y_at = _attention(act, batch, seq, p["lam_q"], p["lam_k"], p["subln_g"], lam_init, ctx, layer)
    y_pl = _pool_branch(act, batch, seq, p["pool_w"], p["pool_scale"])
    out = _merge(x, mod, mod_row0, mod_rows, act, y_hy, y_at, y_pl,
                 p["w_hy_o"], p["w_at_o"], p["w_pl_o"], p["w_out"])
    return out, kv


def kernel(x_prompt, x_sample, cache_k, cache_v, c, c_ctx, norm_g, w_ada, b_ada, w_in, hy_conv_w, hy_conv_b,
           hy_f_w1, hy_f_b1, hy_f_w2, hy_f_b2, hy_f_w3, hy_freq, hy_bias, q_norm_g, k_norm_g, lam_q, lam_k,
           subln_g, pool_w, pool_scale, w_hy_o, w_at_o, w_pl_o, w_out):
    batch, seq, d = x_prompt.shape
    dec_batch, dec_seq, _ = x_sample.shape
    past = cache_k.shape[2]
    assert d == D_MODEL and dec_batch + 1 <= COND_ROWS and dec_seq % GRID_W == 0

    cond = jnp.concatenate([c_ctx[None, :], c, jnp.zeros((COND_ROWS - 1 - dec_batch, d), F32)], axis=0)
    mod = _modulation(cond, w_ada, b_ada).reshape(DEPTH, COND_ROWS, 1, 3 * d)
    dft = {s: _dft_tables(s) for s in (seq, dec_seq)}
    rope = _rope_tables(dec_seq)
    ctx = (cache_k.reshape(dec_batch, DEPTH, past, D_AT), cache_v.reshape(dec_batch, DEPTH, past, D_AT))

    yp = x_prompt.reshape(batch * seq, d)
    ys = x_sample.reshape(dec_batch * dec_seq, d)
    new_ks, new_vs = [], []
    for l in range(DEPTH):
        p = dict(norm_g=norm_g[l][None, :], w_in=w_in[l].astype(BF16),
                 hy_conv_w=hy_conv_w[l], hy_conv_b=hy_conv_b[l], hy_bias=hy_bias[l],
                 gq=jnp.tile(q_norm_g[l], D_AT // AT_DH)[None, :], gk=jnp.tile(k_norm_g[l], D_AT // AT_DH)[None, :],
                 lam_q=lam_q[l], lam_k=lam_k[l], subln_g=subln_g[l][None, :],
                 pool_w=pool_w[l], pool_scale=pool_scale[l][None, :],
                 w_hy_o=w_hy_o[l].astype(BF16), w_at_o=w_at_o[l].astype(BF16),
                 w_pl_o=w_pl_o[l].astype(BF16), w_out=w_out[l].astype(BF16))
        filters = {s: _hyena_filters(s, hy_f_w1[l], hy_f_b1[l], hy_f_w2[l], hy_f_b2[l], hy_f_w3[l], hy_freq[l],
                                     *dft[s]) for s in (seq, dec_seq)}
        lam_init = 0.8 - 0.6 * math.exp(-0.3 * l)
        yp, kv = _trunk_layer(yp, batch, seq, mod[l], 0, p, filters[seq], dft[seq], None, None, l, lam_init)
        new_ks.append(kv[:, :D_AT].reshape(batch, seq, AT_HEADS, 2, AT_DH))
        new_vs.append(kv[:, D_AT:].reshape(batch, seq, AT_HEADS, AT_DV))
        ys, _ = _trunk_layer(ys, dec_batch, dec_seq, mod[l], 1, p, filters[dec_seq], dft[dec_seq], rope, ctx, l,
                             lam_init)
    return (yp.reshape(batch, seq, d), ys.reshape(dec_batch, dec_seq, d),
            jnp.stack(new_ks, axis=1), jnp.stack(new_vs, axis=1))
```

```python
import functools
import math

import jax
import jax.numpy as jnp
from jax import lax
from jax.experimental import pallas as pl
from jax.experimental.pallas import tpu as pltpu

F32 = jnp.float32
BF16 = jnp.bfloat16
HIGHEST = lax.Precision.HIGHEST

D_MODEL = 1024
DEPTH = 2
GRID_W = 64
GRID_W_LOG2 = GRID_W.bit_length() - 1
assert GRID_W == 1 << GRID_W_LOG2
D_HY = 512
HY_ORDER = 2
HY_EMB = 33
HY_BANDS = (HY_EMB - 1) // 2
HY_FFN = 64
HY_FAST_DECAY = 0.3
HY_SLOW_DECAY = 1.5
HY_TARGET = 1e-2
AT_HEADS = 4
AT_DH = 64
AT_DV = 2 * AT_DH
D_AT = AT_HEADS * AT_DV
ROPE_BASE = 10000.0
Q_SCALE = AT_DH ** -0.5 * math.log2(math.e)
D_PL = 512
POOL_WINDOWS = (2, 4, 8, 16)
POOL_GC = D_PL // 4
POOL_PAD = 16
NORM_EPS = 1e-6
D_IN = 8192
COND_ROWS = 8

CB = 512
COL_HY_X1, COL_HY_X2, COL_HY_V, COL_HY_Z = 0, 1, 2, 3
COL_Q, COL_K, COL_V, COL_AT_Z, COL_PL_U, COL_PL_Z = 4, 5, 6, 7, 8, 9
COL_MG = 10

VMEM_LIMIT_BYTES = 56 * 1024 * 1024


def _params(*sem):
    return pltpu.CompilerParams(dimension_semantics=sem, vmem_limit_bytes=VMEM_LIMIT_BYTES)


def _const_spec(shape):
    return pl.BlockSpec(shape, lambda *_: (0,) * len(shape), pipeline_mode=pl.Buffered(1))


def _silu(x):
    return x * jax.nn.sigmoid(x)


def _bdot(a, b):
    return jnp.dot(a.astype(BF16), b.astype(BF16), preferred_element_type=F32)


def _fdot(a, b):
    return jnp.dot(a, b, precision=HIGHEST, preferred_element_type=F32)


def _mod_kernel(cond_ref, w_ref, b_ref, o_ref):
    o_ref[...] = _fdot(_silu(cond_ref[...]), w_ref[...]) + b_ref[...]


def _modulation(cond, w_ada, b_ada):
    d = D_MODEL
    return pl.pallas_call(
        _mod_kernel,
        grid=(DEPTH, 3),
        in_specs=[
            pl.BlockSpec((COND_ROWS, d), lambda l, n: (0, 0)),
            pl.BlockSpec((None, d, d), lambda l, n: (l, 0, n)),
            pl.BlockSpec((None, 1, d), lambda l, n: (l, 0, n)),
        ],
        out_specs=pl.BlockSpec((None, COND_ROWS, d), lambda l, n: (l, 0, n)),
        out_shape=jax.ShapeDtypeStruct((DEPTH, COND_ROWS, 3 * d), F32),
        compiler_params=_params("parallel", "parallel"),
        name="modulation",
    )(cond, w_ada, b_ada.reshape(DEPTH, 1, 3 * d))


def _group_mean_matrix():
    r = lax.broadcasted_iota(jnp.int32, (D_AT, D_AT), 0) // AT_DH
    c = lax.broadcasted_iota(jnp.int32, (D_AT, D_AT), 1) // AT_DH
    return jnp.where(r == c, 1.0 / AT_DH, 0.0).astype(BF16)


def _group_rms_norm(x, g, mean_mat):
    x2 = x * x
    hi = x2.astype(BF16)
    lo = (x2 - hi.astype(F32)).astype(BF16)
    ms = (jnp.dot(hi, mean_mat, preferred_element_type=F32)
          + jnp.dot(lo, mean_mat, preferred_element_type=F32))
    return x * lax.rsqrt(ms + NORM_EPS) * g


def _rotate(x, cos, sin_signed):
    lane = lax.broadcasted_iota(jnp.int32, (1, D_AT), 1)
    partner = jnp.where((lane & 1) == 0, pltpu.roll(x, D_AT - 1, 1), pltpu.roll(x, 1, 1))
    return x * cos + partner * sin_signed


def _in_kernel(*refs, rope, keep_kv):
    x_ref, mod_ref, g_ref, w_ref, gq_ref, gk_ref = refs[:6]
    refs = refs[6:]
    if rope:
        cos_ref, sin_ref = refs[:2]
        refs = refs[2:]
    o_ref = refs[0]
    h_ref = refs[-1]

    x = x_ref[...]
    y = x * lax.rsqrt(jnp.mean(x * x, axis=-1, keepdims=True) + NORM_EPS) * g_ref[...]
    h_ref[...] = (y * (1.0 + mod_ref[:, D_MODEL:2 * D_MODEL]) + mod_ref[:, 0:D_MODEL]).astype(BF16)

    def chunk(c):
        return jnp.dot(h_ref[...], w_ref[:, c * CB:(c + 1) * CB], preferred_element_type=F32)

    def put(c, val):
        o_ref[:, c * CB:(c + 1) * CB] = val.astype(o_ref.dtype)

    for c in (COL_HY_X1, COL_HY_X2, COL_HY_V, COL_PL_U):
        put(c, chunk(c))
    for c in (COL_HY_Z, COL_AT_Z, COL_PL_Z):
        put(c, _silu(chunk(c)))
    for c in range(COL_MG, D_IN // CB):
        put(c, jax.nn.sigmoid(chunk(c)))

    mean_mat = _group_mean_matrix()
    q = _group_rms_norm(chunk(COL_Q), gq_ref[...], mean_mat)
    k = _group_rms_norm(chunk(COL_K), gk_ref[...], mean_mat)
    v = chunk(COL_V)
    if keep_kv:
        refs[1][...] = k
        refs[2][...] = v
    if rope:
        q = _rotate(q, cos_ref[...], sin_ref[...])
        k = _rotate(k, cos_ref[...], sin_ref[...])
    put(COL_Q, q * Q_SCALE)
    put(COL_K, k)
    put(COL_V, v)


def _in_proj(x, mod, mod_row0, rows_per_seq, norm_g, w_in_bf16, gq, gk, rope, seq, keep_kv, tm=512):
    t = x.shape[0]
    blocks_per_seq = rows_per_seq // tm
    vec = pl.BlockSpec((1, D_AT), lambda i: (0, 0))
    in_specs = [pl.BlockSpec((tm, D_MODEL), lambda i: (i, 0)),
                pl.BlockSpec((None, 1, 3 * D_MODEL), lambda i: (mod_row0 + i // blocks_per_seq, 0, 0)),
                pl.BlockSpec((1, D_MODEL), lambda i: (0, 0)),
                _const_spec((D_MODEL, D_IN)), vec, vec]
    args = [x, mod, norm_g, w_in_bf16, gq, gk]
    if rope is not None:
        per_seq = seq // tm
        tab = pl.BlockSpec((tm, D_AT), lambda i: (i % per_seq, 0))
        in_specs += [tab, tab]
        args += list(rope)
    out_specs = [pl.BlockSpec((tm, D_IN), lambda i: (i, 0))]
    out_shape = [jax.ShapeDtypeStruct((t, D_IN), BF16)]
    if keep_kv:
        out_specs += [pl.BlockSpec((tm, D_AT), lambda i: (i, 0))] * 2
        out_shape += [jax.ShapeDtypeStruct((t, D_AT), F32)] * 2
    out = pl.pallas_call(
        functools.partial(_in_kernel, rope=rope is not None, keep_kv=keep_kv),
        grid=(t // tm,),
        in_specs=in_specs,
        out_specs=out_specs,
        out_shape=out_shape,
        scratch_shapes=[pltpu.VMEM((tm, D_MODEL), BF16)],
        compiler_params=_params("parallel"),
        name="in_proj_latent" if rope is not None else "in_proj_ctx",
    )(*args)
    return (out[0], (out[1], out[2])) if keep_kv else (out[0], None)


def _dft_kernel(c_ref, s_ref, *, seq, tr):
    k = lax.broadcasted_iota(jnp.int32, (tr, seq), 0) + pl.program_id(0) * tr
    n = lax.broadcasted_iota(jnp.int32, (tr, seq), 1)
    ang = ((k * n) & (2 * seq - 1)).astype(F32) * (math.pi / seq)
    c_ref[...] = jnp.cos(ang).astype(BF16)
    s_ref[...] = jnp.sin(ang).astype(BF16)


def _dft_tables(seq):
    tr = min(seq, 256)
    spec = pl.BlockSpec((tr, seq), lambda i: (i, 0))
    shape = jax.ShapeDtypeStruct((seq, seq), BF16)
    return pl.pallas_call(
        functools.partial(_dft_kernel, seq=seq, tr=tr),
        grid=(seq // tr,),
        out_specs=[spec, spec],
        out_shape=[shape, shape],
        compiler_params=_params("parallel"),
        name=f"dft_tables_{seq}",
    )()


FILTER_CB = 256


def _filter_kernel(w1t_ref, w1c_ref, w1s_ref, b1_ref, w2_ref, b2_ref, fq_ref, w3f_ref, w3b_ref,
                   c_ref, s_ref, a_ref, b_ref, nyq_ref, h_ref, *, seq):
    row = lax.broadcasted_iota(jnp.int32, (seq, 1), 0)
    t = row.astype(F32) * (1.0 / (seq - 1))

    @pl.when(pl.program_id(0) == 0)
    def _():
        pos = lax.broadcasted_iota(jnp.int32, (1, seq), 1).astype(F32)
        band = lax.broadcasted_iota(jnp.int32, (HY_BANDS, 1), 0).astype(F32)
        f = 1e-4 + band * ((HY_BANDS - 1 - 1e-4) / (HY_BANDS - 1))
        fw = f * (pos * (2.0 * math.pi / seq))
        pre = (w1t_ref[...] * (pos * (1.0 / (seq - 1))) + _fdot(w1c_ref[...], jnp.cos(fw))
               - _fdot(w1s_ref[...], jnp.sin(fw)) + b1_ref[...])
        h = jnp.sin(fq_ref[:, 0:1] * pre)
        h = jnp.sin(fq_ref[:, 1:2] * (_fdot(w2_ref[...], h) + b2_ref[...]))
        h_ref[...] = h.T

    hf = _fdot(h_ref[...], w3f_ref[...])
    hb = _fdot(h_ref[...], w3b_ref[...])

    chan = lax.broadcasted_iota(jnp.int32, (1, FILTER_CB), 1) + (pl.program_id(0) % (D_HY // FILTER_CB)) * FILTER_CB
    max_decay = math.log(HY_TARGET) / HY_FAST_DECAY
    min_decay = math.log(HY_TARGET) / HY_SLOW_DECAY
    deltas = jnp.abs(min_decay + chan.astype(F32) * ((max_decay - min_decay) / (D_HY - 1)))
    decay = jnp.exp(-t * deltas)
    hf = hf * decay
    hb = jnp.where(row == 0, 0.0, hb * decay)
    inv = 1.0 / (jnp.sum(jnp.abs(hf), axis=0, keepdims=True) + jnp.sum(jnp.abs(hb), axis=0, keepdims=True) + NORM_EPS)
    hs = (hf + hb) * inv
    hd = (hf - hb) * inv
    h_re = _bdot(c_ref[...], hs)
    h_im = -_bdot(s_ref[...], hd)
    alt = (1 - 2 * (row & 1)).astype(F32)
    wk = jnp.where(row == 0, 1.0, 2.0) * (0.5 / seq)
    a_ref[...] = wk * h_re
    b_ref[...] = wk * h_im
    nyq_ref[...] = jnp.sum(alt * hs, axis=0, keepdims=True) * (0.5 / seq)


def _hyena_filters(seq, w1, b1, w2, b2, w3, freq, cmat, smat):
    nc = HY_ORDER * D_HY
    nblk = nc // FILTER_CB
    small = lambda a: pl.BlockSpec(a.shape, lambda j: (0,) * a.ndim)
    w1t, w1c, w1s = w1[0:1].T, w1[1:1 + HY_BANDS].T, w1[1 + HY_BANDS:].T
    b1 = b1.reshape(HY_FFN, 1)
    b2 = b2.reshape(HY_FFN, 1)
    w2 = w2.T
    freq = freq.T
    out_blk = pl.BlockSpec((seq, FILTER_CB), lambda j: (0, j))
    return pl.pallas_call(
        functools.partial(_filter_kernel, seq=seq),
        grid=(nblk,),
        in_specs=[small(w1t), small(w1c), small(w1s), small(b1), small(w2), small(b2), small(freq),
                  pl.BlockSpec((HY_FFN, FILTER_CB), lambda j: (0, j)),
                  pl.BlockSpec((HY_FFN, FILTER_CB), lambda j: (0, nblk + j)),
                  _const_spec((seq, seq)), _const_spec((seq, seq))],
        out_specs=[out_blk, out_blk, pl.BlockSpec((1, FILTER_CB), lambda j: (0, j))],
        out_shape=[jax.ShapeDtypeStruct((seq, nc), F32), jax.ShapeDtypeStruct((seq, nc), F32),
                   jax.ShapeDtypeStruct((1, nc), F32)],
        scratch_shapes=[pltpu.VMEM((seq, HY_FFN), F32)],
        compiler_params=_params("arbitrary"),
        name=f"hyena_filters_{seq}",
    )(w1t, w1c, w1s, b1, w2, b2, freq, w3, w3, cmat, smat)


def _hyena_kernel(x1_ref, x2_ref, v_ref, z_ref, w1_ref, w2_ref, wv_ref, c1_ref, c2_ref, cv_ref,
                  a0_ref, b0_ref, a1_ref, b1_ref, n0_ref, n1_ref, s0_ref, s1_ref, cm_ref, sm_ref,
                  o_ref, u_ref, g_ref, ub_ref, p_ref, q_ref, *, seq, rb):
    row = lax.broadcasted_iota(jnp.int32, (seq, 1), 0)
    alt = (1 - 2 * (row & 1)).astype(F32)

    def short_conv(x_ref, w_ref, c_ref):
        x = x_ref[...].astype(F32)
        prev = jnp.where(row == 0, 0.0, pltpu.roll(x, 1, 0))
        nxt = jnp.where(row == seq - 1, 0.0, pltpu.roll(x, seq - 1, 0))
        return prev * w_ref[0:1, :] + x * w_ref[1:2, :] + nxt * w_ref[2:3, :] + c_ref[...]

    def long_conv(a_ref, b_ref, nyq_ref, skip_ref, finish):
        nyq = jnp.sum(u_ref[...] * alt, axis=0, keepdims=True) * nyq_ref[...]
        for r in range(0, seq, rb):
            rows = pl.ds(r, rb)
            uc = jnp.dot(cm_ref[rows, :], ub_ref[...], preferred_element_type=F32)
            us = jnp.dot(sm_ref[rows, :], ub_ref[...], preferred_element_type=F32)
            a = a_ref[rows, :]
            b = b_ref[rows, :]
            p_ref[rows, :] = (uc * a + us * b).astype(BF16)
            q_ref[rows, :] = (us * a - uc * b).astype(BF16)
        for r in range(0, seq, rb):
            rows = pl.ds(r, rb)
            y = (jnp.dot(cm_ref[rows, :], p_ref[...], preferred_element_type=F32)
                 + jnp.dot(sm_ref[rows, :], q_ref[...], preferred_element_type=F32)
                 + alt[r:r + rb] * nyq + u_ref[rows, :] * skip_ref[...])
            finish(rows, g_ref[rows, :] * y)

    def keep(rows, y):
        u_ref[rows, :] = y
        ub_ref[rows, :] = y.astype(BF16)

    def emit(rows, y):
        o_ref[rows, :] = (y * z_ref[rows, :].astype(F32)).astype(o_ref.dtype)

    v = short_conv(v_ref, wv_ref, cv_ref)
    u_ref[...] = v
    ub_ref[...] = v.astype(BF16)
    g_ref[...] = short_conv(x1_ref, w1_ref, c1_ref)
    long_conv(a0_ref, b0_ref, n0_ref, s0_ref, keep)
    g_ref[...] = short_conv(x2_ref, w2_ref, c2_ref)
    long_conv(a1_ref, b1_ref, n1_ref, s1_ref, emit)


def _hyena_branch(proj, batch, seq, conv_w, conv_b, hy_bias, filt_a, filt_b, filt_nyq, cmat, smat):
    cb = 256 if seq > 512 else D_HY
    rb = min(seq, 512)
    nj = D_HY // cb
    t = batch * seq
    conv_b = conv_b.reshape(1, 3 * D_HY)
    skip = hy_bias.reshape(1, HY_ORDER * D_HY)

    def proj_spec(col512):
        off = col512 * (CB // cb)
        return pl.BlockSpec((seq, cb), lambda j, b: (b, off + j))

    def chan_spec(rows, part):
        off = part * nj
        return pl.BlockSpec((rows, cb), lambda j, b: (0, off + j), pipeline_mode=pl.Buffered(1))

    return pl.pallas_call(
        functools.partial(_hyena_kernel, seq=seq, rb=rb),
        grid=(nj, batch),
        in_specs=[proj_spec(COL_HY_X1), proj_spec(COL_HY_X2), proj_spec(COL_HY_V), proj_spec(COL_HY_Z),
                  chan_spec(3, 0), chan_spec(3, 1), chan_spec(3, 2),
                  chan_spec(1, 0), chan_spec(1, 1), chan_spec(1, 2),
                  chan_spec(seq, 0), chan_spec(seq, 0), chan_spec(seq, 1), chan_spec(seq, 1),
                  chan_spec(1, 0), chan_spec(1, 1), chan_spec(1, 0), chan_spec(1, 1),
                  _const_spec((seq, seq)), _const_spec((seq, seq))],
        out_specs=pl.BlockSpec((seq, cb), lambda j, b: (b, j)),
        out_shape=jax.ShapeDtypeStruct((t, D_HY), BF16),
        scratch_shapes=[pltpu.VMEM((seq, cb), F32), pltpu.VMEM((seq, cb), F32), pltpu.VMEM((seq, cb), BF16),
                        pltpu.VMEM((seq, cb), BF16), pltpu.VMEM((seq, cb), BF16)],
        compiler_params=_params("parallel", "parallel"),
        name=f"hyena_{seq}",
    )(proj, proj, proj, proj, conv_w, conv_w, conv_w, conv_b, conv_b, conv_b,
      filt_a, filt_b, filt_a, filt_b, filt_nyq, filt_nyq, skip, skip, cmat, smat)


def _rope_kernel(cos_ref, sin_ref, *, tr):
    lane = lax.broadcasted_iota(jnp.int32, (1, D_AT), 1)
    dim = lane & (AT_DH - 1)
    pair = ((dim & (AT_DH // 2 - 1)) >> 1).astype(F32)
    inv = jnp.exp(pair * (-(2.0 / (AT_DH // 2)) * math.log(ROPE_BASE)))
    pos = lax.broadcasted_iota(jnp.int32, (tr, 1), 0) + pl.program_id(0) * tr
    coord = jnp.where(dim < AT_DH // 2, pos >> GRID_W_LOG2, pos & (GRID_W - 1)).astype(F32)
    ang = coord * inv
    cos_ref[...] = jnp.cos(ang)
    sin_ref[...] = jnp.where((lane & 1) == 0, -jnp.sin(ang), jnp.sin(ang))


def _rope_tables(seq):
    tr = 256
    spec = pl.BlockSpec((tr, D_AT), lambda i: (i, 0))
    shape = jax.ShapeDtypeStruct((seq, D_AT), F32)
    return pl.pallas_call(
        functools.partial(_rope_kernel, tr=tr),
        grid=(seq // tr,),
        out_specs=[spec, spec],
        out_shape=[shape, shape],
        compiler_params=_params("parallel"),
        name="rope_tables",
    )()


def _nt_dot(a, b):
    return lax.dot_general(a, b, (((1,), (1,)), ((), ())), preferred_element_type=F32)


ONES_ROWS = 8


def _diff_lambda(lq_ref, lk_ref, lam_init):
    lqk = lq_ref[...] * lk_ref[...]
    return (jnp.exp(jnp.sum(lqk[0:1, :], axis=-1, keepdims=True))
            - jnp.exp(jnp.sum(lqk[1:2, :], axis=-1, keepdims=True)) + lam_init)


def _attn_ctx_kernel(q_ref, k_ref, v_ref, z_ref, lq_ref, lk_ref, sg_ref, o_ref, vt_ref, *, lam_init):
    seq = k_ref.shape[0]
    vt_ref[0:D_AT, :] = v_ref[...].astype(F32).T.astype(BF16)
    vt_ref[D_AT:D_AT + ONES_ROWS, :] = jnp.ones((ONES_ROWS, seq), BF16)
    lam = _diff_lambda(lq_ref, lk_ref, lam_init)
    ones = vt_ref[D_AT:D_AT + ONES_ROWS, :]

    for h in range(AT_HEADS):
        hv = slice(h * AT_DV, (h + 1) * AT_DV)

        def unnormalised(c):
            cols = slice(h * AT_DV + c * AT_DH, h * AT_DV + (c + 1) * AT_DH)
            s = _nt_dot(k_ref[:, cols], q_ref[:, cols])
            e = jnp.exp2(s - jnp.max(s, axis=0, keepdims=True)).astype(BF16)
            o = jnp.dot(vt_ref[hv, :], e, preferred_element_type=F32)
            den = jnp.dot(ones, e, preferred_element_type=F32)
            return o, den[0:1, :]

        o1, den1 = unnormalised(0)
        o2, den2 = unnormalised(1)
        o = o1 * (1.0 / den1) - o2 * (lam / den2)
        o = o * lax.rsqrt(jnp.mean(o * o, axis=0, keepdims=True) + NORM_EPS)
        o = o.T * sg_ref[...] * (1.0 - lam_init)
        o_ref[:, hv] = (o * z_ref[:, hv].astype(F32)).astype(o_ref.dtype)


def _attn_latent_kernel(q_ref, k_ref, v_ref, z_ref, lq_ref, lk_ref, sg_ref, ck_ref, cv_ref, o_ref,
                        *, lam_init):
    lam = _diff_lambda(lq_ref, lk_ref, lam_init)

    def scores(h, c):
        cols = slice(h * AT_DV + c * AT_DH, h * AT_DV + (c + 1) * AT_DH)
        q = q_ref[:, cols]
        return _nt_dot(q, k_ref[:, cols]), _nt_dot(q, ck_ref[:, cols].astype(BF16))

    def softmax_parts(s, sc):
        m = jnp.maximum(jnp.max(s, axis=-1, keepdims=True), jnp.max(sc, axis=-1, keepdims=True))
        e = jnp.exp2(s - m)
        ec = jnp.exp2(sc - m)
        return e, ec, jnp.sum(e, axis=-1, keepdims=True) + jnp.sum(ec, axis=-1, keepdims=True)

    def finish(h, part1, part2):
        e1, ec1, den1 = part1
        e2, ec2, den2 = part2
        ratio = lam * den1 / den2
        hv = slice(h * AT_DV, (h + 1) * AT_DV)
        o = (jnp.dot((e1 - e2 * ratio).astype(BF16), v_ref[:, hv], preferred_element_type=F32)
             + jnp.dot((ec1 - ec2 * ratio).astype(BF16), cv_ref[:, hv].astype(BF16),
                       preferred_element_type=F32)) * (1.0 / den1)
        o = o * lax.rsqrt(jnp.mean(o * o, axis=-1, keepdims=True) + NORM_EPS) * sg_ref[...]
        o_ref[:, hv] = (o * (1.0 - lam_init) * z_ref[:, hv].astype(F32)).astype(o_ref.dtype)

    units = [(h, c) for h in range(AT_HEADS) for c in range(2)]
    ahead = scores(*units[0])
    parts = [None, None]
    for i, (h, c) in enumerate(units):
        current = ahead
        if i + 1 < len(units):
            ahead = scores(*units[i + 1])
        parts[c] = softmax_parts(*current)
        if c == 1:
            finish(h, parts[0], parts[1])


def _attention(act, batch, seq, lam_q, lam_k, subln_g, lam_init, ctx, layer):
    t = batch * seq
    tq = 256
    nq = seq // tq
    small = lambda a: pl.BlockSpec(a.shape, lambda b, i: (0,) * a.ndim)
    in_specs = [pl.BlockSpec((tq, CB), lambda b, i: (b * nq + i, COL_Q)),
                pl.BlockSpec((seq, CB), lambda b, i: (b, COL_K)),
                pl.BlockSpec((seq, CB), lambda b, i: (b, COL_V)),
                pl.BlockSpec((tq, CB), lambda b, i: (b * nq + i, COL_AT_Z)),
                small(lam_q), small(lam_k), small(subln_g)]
    args = [act, act, act, act, lam_q, lam_k, subln_g]
    if ctx is None:
        assert nq == 1
        body = functools.partial(_attn_ctx_kernel, lam_init=lam_init)
        scratch = [pltpu.VMEM((D_AT + ONES_ROWS, seq), BF16)]
    else:
        ck, cv = ctx
        cspec = pl.BlockSpec((None, None, ck.shape[2], D_AT), lambda b, i: (b, layer, 0, 0))
        in_specs += [cspec, cspec]
        args += [ck, cv]
        body = functools.partial(_attn_latent_kernel, lam_init=lam_init)
        scratch = []
    return pl.pallas_call(
        body,
        grid=(batch, nq),
        in_specs=in_specs,
        out_specs=pl.BlockSpec((tq, D_AT), lambda b, i: (b * nq + i, 0)),
        out_shape=jax.ShapeDtypeStruct((t, D_AT), BF16),
        scratch_shapes=scratch,
        compiler_params=_params("parallel", "parallel"),
        name=f"diff_attn_{seq}",
    )(*args)


def _pool_kernel(u_ref, z_ref, w_ref, sc_ref, o_ref, *, seq):
    padded = seq + 2 * POOL_PAD
    t = lax.broadcasted_iota(jnp.int32, (seq, 1), 0)
    zeros = jnp.zeros((POOL_PAD, POOL_GC), F32)
    for g, win in enumerate(POOL_WINDOWS):
        cols = slice(g * POOL_GC, (g + 1) * POOL_GC)
        u = u_ref[:, cols].astype(F32)
        s = jnp.concatenate([zeros, u, zeros], axis=0)
        s = s + pltpu.roll(s, 1, 0)
        half = 1
        while 2 * half < win:
            s = pltpu.roll(s, padded - half, 0) + pltpu.roll(s, half, 0)
            half *= 2
        lo = jnp.maximum(t - win // 2, 0)
        hi = jnp.minimum(t + win // 2, seq)
        m = s[POOL_PAD:POOL_PAD + seq] / (hi - lo).astype(F32) - u
        y = jnp.dot(m.astype(BF16), w_ref[g].astype(BF16), preferred_element_type=F32)
        o_ref[:, cols] = (y * sc_ref[:, cols] * z_ref[:, cols].astype(F32)).astype(o_ref.dtype)


def _pool_branch(proj, batch, seq, pool_w, pool_scale):
    t = batch * seq
    return pl.pallas_call(
        functools.partial(_pool_kernel, seq=seq),
        grid=(batch,),
        in_specs=[pl.BlockSpec((seq, CB), lambda b: (b, COL_PL_U)),
                  pl.BlockSpec((seq, CB), lambda b: (b, COL_PL_Z)),
                  pl.BlockSpec(pool_w.shape, lambda b: (0, 0, 0)),
                  pl.BlockSpec((1, D_PL), lambda b: (0, 0))],
        out_specs=pl.BlockSpec((seq, D_PL), lambda b: (b, 0)),
        out_shape=jax.ShapeDtypeStruct((t, D_PL), BF16),
        compiler_params=_params("parallel"),
        name=f"pool_{seq}",
    )(proj, proj, pool_w, pool_scale)


def _merge_kernel(x_ref, mod_ref, yh_ref, ya_ref, yp_ref, g0_ref, g1_ref, g2_ref,
                  wh_ref, wa_ref, wp_ref, wo_ref, o_ref):
    merged = (g0_ref[...].astype(F32) * jnp.dot(yh_ref[...], wh_ref[...], preferred_element_type=F32)
              + g1_ref[...].astype(F32) * jnp.dot(ya_ref[...], wa_ref[...], preferred_element_type=F32)
              + g2_ref[...].astype(F32) * jnp.dot(yp_ref[...], wp_ref[...], preferred_element_type=F32))
    out = jnp.dot(merged.astype(BF16), wo_ref[...], preferred_element_type=F32)
    o_ref[...] = x_ref[...] + mod_ref[:, 2 * D_MODEL:3 * D_MODEL] * out


def _merge(x, mod, mod_row0, rows_per_seq, proj, y_hy, y_at, y_pl, w_hy_o, w_at_o, w_pl_o, w_out, tm=512):
    t = x.shape[0]
    blocks_per_seq = rows_per_seq // tm
    d = D_MODEL
    branch = pl.BlockSpec((tm, CB), lambda i: (i, 0))
    gate = lambda g: pl.BlockSpec((tm, d), lambda i: (i, COL_MG // 2 + g))
    return pl.pallas_call(
        _merge_kernel,
        grid=(t // tm,),
        in_specs=[pl.BlockSpec((tm, d), lambda i: (i, 0)),
                  pl.BlockSpec((None, 1, 3 * d), lambda i: (mod_row0 + i // blocks_per_seq, 0, 0)),
                  branch, branch, branch, gate(0), gate(1), gate(2),
                  _const_spec((CB, d)), _const_spec((CB, d)), _const_spec((CB, d)), _const_spec((d, d))],
        out_specs=pl.BlockSpec((tm, d), lambda i: (i, 0)),
        out_shape=jax.ShapeDtypeStruct((t, d), F32),
        compiler_params=_params("parallel"),
        name="merge",
    )(x, mod, y_hy, y_at, y_pl, proj, proj, proj, w_hy_o, w_at_o, w_pl_o, w_out)


def _trunk_layer(x, batch, seq, mod, mod_row0, p, filters, dft, rope, ctx, layer, lam_init):
    mod_rows = seq if mod_row0 else batch * seq
    act, kv = _in_proj(x, mod, mod_row0, mod_rows, p["norm_g"], p["w_in"], p["gq"], p["gk"], rope, seq,
                       keep_kv=rope is None)
    y_hy = _hyena_branch(act, batch, seq, p["hy_conv_w"], p["hy_conv_b"], p["hy_bias"], *filters, *dft)
    y_at = _attention(act, batch, seq, p["lam_q"], p["lam_k"], p["subln_g"], lam_init, ctx, layer)
    y_pl = _pool_branch(act, batch, seq, p["pool_w"], p["pool_scale"])
    out = _merge(x, mod, mod_row0, mod_rows, act, y_hy, y_at, y_pl,
                 p["w_hy_o"], p["w_at_o"], p["w_pl_o"], p["w_out"])
    return out, kv


def kernel(x_prompt, x_sample, cache_k, cache_v, c, c_ctx, norm_g, w_ada, b_ada, w_in, hy_conv_w, hy_conv_b,
           hy_f_w1, hy_f_b1, hy_f_w2, hy_f_b2, hy_f_w3, hy_freq, hy_bias, q_norm_g, k_norm_g, lam_q, lam_k,
           subln_g, pool_w, pool_scale, w_hy_o, w_at_o, w_pl_o, w_out):
    batch, seq, d = x_prompt.shape
    dec_batch, dec_seq, _ = x_sample.shape
    past = cache_k.shape[2]
    assert d == D_MODEL and dec_batch + 1 <= COND_ROWS and dec_seq % GRID_W == 0

    cond = jnp.concatenate([c_ctx[None, :], c, jnp.zeros((COND_ROWS - 1 - dec_batch, d), F32)], axis=0)
    mod = _modulation(cond, w_ada, b_ada).reshape(DEPTH, COND_ROWS, 1, 3 * d)
    dft = {s: _dft_tables(s) for s in (seq, dec_seq)}
    rope = _rope_tables(dec_seq)
    ctx = (cache_k.reshape(dec_batch, DEPTH, past, D_AT), cache_v.reshape(dec_batch, DEPTH, past, D_AT))

    yp = x_prompt.reshape(batch * seq, d)
    ys = x_sample.reshape(dec_batch * dec_seq, d)
    new_ks, new_vs = [], []
    for l in range(DEPTH):
        p = dict(norm_g=norm_g[l][None, :], w_in=w_in[l].astype(BF16),
                 hy_conv_w=hy_conv_w[l], hy_conv_b=hy_conv_b[l], hy_bias=hy_bias[l],
                 gq=jnp.tile(q_norm_g[l], D_AT // AT_DH)[None, :], gk=jnp.tile(k_norm_g[l], D_AT // AT_DH)[None, :],
                 lam_q=lam_q[l], lam_k=lam_k[l], subln_g=subln_g[l][None, :],
                 pool_w=pool_w[l], pool_scale=pool_scale[l][None, :],
                 w_hy_o=w_hy_o[l].astype(BF16), w_at_o=w_at_o[l].astype(BF16),
                 w_pl_o=w_pl_o[l].astype(BF16), w_out=w_out[l].astype(BF16))
        filters = {s: _hyena_filters(s, hy_f_w1[l], hy_f_b1[l], hy_f_w2[l], hy_f_b2[l], hy_f_w3[l], hy_freq[l],
                                     *dft[s]) for s in (seq, dec_seq)}
        lam_init = 0.8 - 0.6 * math.exp(-0.3 * l)
        yp, kv = _trunk_layer(yp, batch, seq, mod[l], 0, p, filters[seq], dft[seq], None, None, l, lam_init)
        new_ks.append(kv[0].reshape(batch, seq, AT_HEADS, 2, AT_DH))
        new_vs.append(kv[1].reshape(batch, seq, AT_HEADS, AT_DV))
        ys, _ = _trunk_layer(ys, dec_batch, dec_seq, mod[l], 1, p, filters[dec_seq], dft[dec_seq], rope, ctx, l,
                             lam_init)
    return (yp.reshape(batch, seq, d), ys.reshape(dec_batch, dec_seq, d),
            jnp.stack(new_ks, axis=1), jnp.stack(new_vs, axis=1))
```

```python
import functools
import math

import jax
import jax.numpy as jnp
from jax import lax
from jax.experimental import pallas as pl
from jax.experimental.pallas import tpu as pltpu

F32 = jnp.float32
BF16 = jnp.bfloat16
HIGHEST = lax.Precision.HIGHEST

D_MODEL = 1024
DEPTH = 2
GRID_W = 64
GRID_W_LOG2 = GRID_W.bit_length() - 1
assert GRID_W == 1 << GRID_W_LOG2
D_HY = 512
HY_ORDER = 2
HY_EMB = 33
HY_BANDS = (HY_EMB - 1) // 2
HY_FFN = 64
HY_FAST_DECAY = 0.3
HY_SLOW_DECAY = 1.5
HY_TARGET = 1e-2
AT_HEADS = 4
AT_DH = 64
AT_DV = 2 * AT_DH
D_AT = AT_HEADS * AT_DV
ROPE_BASE = 10000.0
Q_SCALE = AT_DH ** -0.5 * math.log2(math.e)
D_PL = 512
POOL_WINDOWS = (2, 4, 8, 16)
POOL_GC = D_PL // 4
POOL_PAD = 16
NORM_EPS = 1e-6
D_IN = 8192
COND_ROWS = 8

CB = 512
COL_HY_X1, COL_HY_X2, COL_HY_V, COL_HY_Z = 0, 1, 2, 3
COL_Q, COL_K, COL_V, COL_AT_Z, COL_PL_U, COL_PL_Z = 4, 5, 6, 7, 8, 9
COL_MG = 10

VMEM_LIMIT_BYTES = 56 * 1024 * 1024


def _params(*sem):
    return pltpu.CompilerParams(dimension_semantics=sem, vmem_limit_bytes=VMEM_LIMIT_BYTES)


def _const_spec(shape):
    return pl.BlockSpec(shape, lambda *_: (0,) * len(shape), pipeline_mode=pl.Buffered(1))


def _silu(x):
    return x * jax.nn.sigmoid(x)


def _bdot(a, b):
    return jnp.dot(a.astype(BF16), b.astype(BF16), preferred_element_type=F32)


def _fdot(a, b):
    return jnp.dot(a, b, precision=HIGHEST, preferred_element_type=F32)


def _mod_kernel(cond_ref, w_ref, b_ref, o_ref):
    o_ref[...] = _fdot(_silu(cond_ref[...]), w_ref[...]) + b_ref[...]


def _modulation(cond, w_ada, b_ada):
    d = D_MODEL
    return pl.pallas_call(
        _mod_kernel,
        grid=(DEPTH, 3),
        in_specs=[
            pl.BlockSpec((COND_ROWS, d), lambda l, n: (0, 0)),
            pl.BlockSpec((None, d, d), lambda l, n: (l, 0, n)),
            pl.BlockSpec((None, 1, d), lambda l, n: (l, 0, n)),
        ],
        out_specs=pl.BlockSpec((None, COND_ROWS, d), lambda l, n: (l, 0, n)),
        out_shape=jax.ShapeDtypeStruct((DEPTH, COND_ROWS, 3 * d), F32),
        compiler_params=_params("parallel", "parallel"),
        name="modulation",
    )(cond, w_ada, b_ada.reshape(DEPTH, 1, 3 * d))


def _group_mean_matrix():
    r = lax.broadcasted_iota(jnp.int32, (D_AT, D_AT), 0) // AT_DH
    c = lax.broadcasted_iota(jnp.int32, (D_AT, D_AT), 1) // AT_DH
    return jnp.where(r == c, 1.0 / AT_DH, 0.0).astype(BF16)


def _group_rms_norm(x, g, mean_mat):
    x2 = x * x
    hi = x2.astype(BF16)
    lo = (x2 - hi.astype(F32)).astype(BF16)
    ms = (jnp.dot(hi, mean_mat, preferred_element_type=F32)
          + jnp.dot(lo, mean_mat, preferred_element_type=F32))
    return x * lax.rsqrt(ms + NORM_EPS) * g


def _rotate(x, cos, sin_signed):
    lane = lax.broadcasted_iota(jnp.int32, (1, D_AT), 1)
    partner = jnp.where((lane & 1) == 0, pltpu.roll(x, D_AT - 1, 1), pltpu.roll(x, 1, 1))
    return x * cos + partner * sin_signed


def _in_kernel(*refs, rope, keep_kv):
    x_ref, mod_ref, g_ref, w_ref, gq_ref, gk_ref = refs[:6]
    refs = refs[6:]
    if rope:
        cos_ref, sin_ref = refs[:2]
        refs = refs[2:]
    o_ref = refs[0]
    h_ref = refs[-1]

    x = x_ref[...]
    y = x * lax.rsqrt(jnp.mean(x * x, axis=-1, keepdims=True) + NORM_EPS) * g_ref[...]
    h_ref[...] = (y * (1.0 + mod_ref[:, D_MODEL:2 * D_MODEL]) + mod_ref[:, 0:D_MODEL]).astype(BF16)

    def chunk(c):
        return jnp.dot(h_ref[...], w_ref[:, c * CB:(c + 1) * CB], preferred_element_type=F32)

    def put(c, val):
        o_ref[:, c * CB:(c + 1) * CB] = val.astype(o_ref.dtype)

    for c in (COL_HY_X1, COL_HY_X2, COL_HY_V, COL_PL_U):
        put(c, chunk(c))
    for c in (COL_HY_Z, COL_AT_Z, COL_PL_Z):
        put(c, _silu(chunk(c)))
    for c in range(COL_MG, D_IN // CB):
        put(c, jax.nn.sigmoid(chunk(c)))

    mean_mat = _group_mean_matrix()
    q = _group_rms_norm(chunk(COL_Q), gq_ref[...], mean_mat)
    k = _group_rms_norm(chunk(COL_K), gk_ref[...], mean_mat)
    v = chunk(COL_V)
    if keep_kv:
        refs[1][...] = k
        refs[2][...] = v
    if rope:
        q = _rotate(q, cos_ref[...], sin_ref[...])
        k = _rotate(k, cos_ref[...], sin_ref[...])
    put(COL_Q, q * Q_SCALE)
    put(COL_K, k)
    put(COL_V, v)


def _in_proj(x, mod, mod_row0, rows_per_seq, norm_g, w_in_bf16, gq, gk, rope, seq, keep_kv, tm=512):
    t = x.shape[0]
    blocks_per_seq = rows_per_seq // tm
    vec = pl.BlockSpec((1, D_AT), lambda i: (0, 0))
    in_specs = [pl.BlockSpec((tm, D_MODEL), lambda i: (i, 0)),
                pl.BlockSpec((None, 1, 3 * D_MODEL), lambda i: (mod_row0 + i // blocks_per_seq, 0, 0)),
                pl.BlockSpec((1, D_MODEL), lambda i: (0, 0)),
                _const_spec((D_MODEL, D_IN)), vec, vec]
    args = [x, mod, norm_g, w_in_bf16, gq, gk]
    if rope is not None:
        per_seq = seq // tm
        tab = pl.BlockSpec((tm, D_AT), lambda i: (i % per_seq, 0))
        in_specs += [tab, tab]
        args += list(rope)
    out_specs = [pl.BlockSpec((tm, D_IN), lambda i: (i, 0))]
    out_shape = [jax.ShapeDtypeStruct((t, D_IN), BF16)]
    if keep_kv:
        out_specs += [pl.BlockSpec((tm, D_AT), lambda i: (i, 0))] * 2
        out_shape += [jax.ShapeDtypeStruct((t, D_AT), F32)] * 2
    out = pl.pallas_call(
        functools.partial(_in_kernel, rope=rope is not None, keep_kv=keep_kv),
        grid=(t // tm,),
        in_specs=in_specs,
        out_specs=out_specs,
        out_shape=out_shape,
        scratch_shapes=[pltpu.VMEM((tm, D_MODEL), BF16)],
        compiler_params=_params("parallel"),
        name="in_proj_latent" if rope is not None else "in_proj_ctx",
    )(*args)
    return (out[0], (out[1], out[2])) if keep_kv else (out[0], None)


TABLE_ROWS = 32


def _dft_kernel(ce_ref, se_ref, co_ref, so_ref, cot_ref, sot_ref, bc_ref, bs_ref, oc_ref, os_ref, *, seq):
    half = seq // 2
    col = lax.broadcasted_iota(jnp.int32, (1, half), 1)
    base = lax.broadcasted_iota(jnp.int32, (half // TABLE_ROWS, 1), 0) * TABLE_ROWS
    off = lax.broadcasted_iota(jnp.int32, (TABLE_ROWS, 1), 0)

    def trig(c_ref, s_ref, slot, product):
        ang = (product & (2 * seq - 1)).astype(F32) * (math.pi / seq)
        c_ref[slot] = jnp.cos(ang)
        s_ref[slot] = jnp.sin(ang)

    trig(bc_ref, bs_ref, 0, 2 * base * col)
    trig(bc_ref, bs_ref, 1, base * (2 * col + 1))
    trig(bc_ref, bs_ref, 2, (2 * base + 1) * col)
    trig(oc_ref, os_ref, 0, 2 * off * col)
    trig(oc_ref, os_ref, 1, off * (2 * col + 1))
    outs = ((ce_ref, se_ref, 0, 0), (co_ref, so_ref, 1, 1), (cot_ref, sot_ref, 2, 0))

    def block(i, carry):
        rows = pl.ds(pl.multiple_of(i * TABLE_ROWS, TABLE_ROWS), TABLE_ROWS)
        for c_ref, s_ref, b, o in outs:
            bc = bc_ref[b, pl.ds(i, 1), :]
            bs = bs_ref[b, pl.ds(i, 1), :]
            oc = oc_ref[o]
            osn = os_ref[o]
            c_ref[rows, :] = (bc * oc - bs * osn).astype(BF16)
            s_ref[rows, :] = (bs * oc + bc * osn).astype(BF16)
        return carry

    lax.fori_loop(0, half // TABLE_ROWS, block, 0)


def _dft_tables(seq):
    half = seq // 2
    shape = jax.ShapeDtypeStruct((half, half), BF16)
    return pl.pallas_call(
        functools.partial(_dft_kernel, seq=seq),
        out_shape=[shape] * 6,
        scratch_shapes=[pltpu.VMEM((3, half // TABLE_ROWS, half), F32), pltpu.VMEM((3, half // TABLE_ROWS, half), F32),
                        pltpu.VMEM((2, TABLE_ROWS, half), F32), pltpu.VMEM((2, TABLE_ROWS, half), F32)],
        compiler_params=pltpu.CompilerParams(vmem_limit_bytes=VMEM_LIMIT_BYTES),
        name=f"dft_tables_{seq}",
    )()


FILTER_CB = 256


def _filter_kernel(w1t_ref, w1c_ref, w1s_ref, b1_ref, w2_ref, b2_ref, fq_ref, w3f_ref, w3b_ref,
                   ce_ref, se_ref, co_ref, so_ref, alo_ref, blo_ref, ahi_ref, bhi_ref, mid_ref, h_ref, *, seq):
    half = seq // 2
    row = lax.broadcasted_iota(jnp.int32, (half, 1), 0)

    @pl.when(pl.program_id(0) == 0)
    def _():
        band = lax.broadcasted_iota(jnp.int32, (HY_BANDS, 1), 0).astype(F32)
        f = 1e-4 + band * ((HY_BANDS - 1 - 1e-4) / (HY_BANDS - 1))
        for p in range(2):
            pos = (2 * lax.broadcasted_iota(jnp.int32, (1, half), 1) + p).astype(F32)
            fw = f * (pos * (2.0 * math.pi / seq))
            pre = (w1t_ref[...] * (pos * (1.0 / (seq - 1))) + _fdot(w1c_ref[...], jnp.cos(fw))
                   - _fdot(w1s_ref[...], jnp.sin(fw)) + b1_ref[...])
            h = jnp.sin(fq_ref[:, 0:1] * pre)
            h = jnp.sin(fq_ref[:, 1:2] * (_fdot(w2_ref[...], h) + b2_ref[...]))
            h_ref[p] = h.T

    chan = lax.broadcasted_iota(jnp.int32, (1, FILTER_CB), 1) + (pl.program_id(0) % (D_HY // FILTER_CB)) * FILTER_CB
    max_decay = math.log(HY_TARGET) / HY_FAST_DECAY
    min_decay = math.log(HY_TARGET) / HY_SLOW_DECAY
    deltas = jnp.abs(min_decay + chan.astype(F32) * ((max_decay - min_decay) / (D_HY - 1)))

    def taps(p):
        decay = jnp.exp((2 * row + p).astype(F32) * (-1.0 / (seq - 1)) * deltas)
        hf = _fdot(h_ref[p], w3f_ref[...]) * decay
        hb = _fdot(h_ref[p], w3b_ref[...]) * decay
        if p == 0:
            hb = jnp.where(row == 0, 0.0, hb)
        return hf, hb

    hf0, hb0 = taps(0)
    hf1, hb1 = taps(1)
    inv = 1.0 / (jnp.sum(jnp.abs(hf0) + jnp.abs(hf1) + jnp.abs(hb0) + jnp.abs(hb1), axis=0, keepdims=True) + NORM_EPS)
    hs0, hs1 = (hf0 + hb0) * inv, (hf1 + hb1) * inv
    hd0, hd1 = (hf0 - hb0) * inv, (hf1 - hb1) * inv
    c_even = _bdot(ce_ref[...], hs0)
    c_odd = _bdot(co_ref[...], hs1)
    s_even = _bdot(se_ref[...], hd0)
    s_odd = _bdot(so_ref[...], hd1)
    wk = jnp.where(row == 0, 0.5 / seq, 1.0 / seq)
    alo_ref[...] = wk * (c_even + c_odd)
    ahi_ref[...] = wk * (c_even - c_odd)
    blo_ref[...] = -wk * (s_even + s_odd)
    bhi_ref[...] = wk * (s_even - s_odd)
    alt = (1 - 2 * (row & 1)).astype(F32)
    mid_ref[0:1, :] = jnp.sum(alt * hs0, axis=0, keepdims=True) * (1.0 / seq)
    mid_ref[1:2, :] = jnp.sum(alt * hd1, axis=0, keepdims=True) * (-1.0 / seq)


def _hyena_filters(seq, w1, b1, w2, b2, w3, freq, tables):
    nc = HY_ORDER * D_HY
    nblk = nc // FILTER_CB
    half = seq // 2
    small = lambda a: pl.BlockSpec(a.shape, lambda j: (0,) * a.ndim)
    w1t, w1c, w1s = w1[0:1].T, w1[1:1 + HY_BANDS].T, w1[1 + HY_BANDS:].T
    b1 = b1.reshape(HY_FFN, 1)
    b2 = b2.reshape(HY_FFN, 1)
    w2 = w2.T
    freq = freq.T
    out_blk = pl.BlockSpec((half, FILTER_CB), lambda j: (0, j))
    spec_shape = jax.ShapeDtypeStruct((half, nc), F32)
    return pl.pallas_call(
        functools.partial(_filter_kernel, seq=seq),
        grid=(nblk,),
        in_specs=[small(w1t), small(w1c), small(w1s), small(b1), small(w2), small(b2), small(freq),
                  pl.BlockSpec((HY_FFN, FILTER_CB), lambda j: (0, j)),
                  pl.BlockSpec((HY_FFN, FILTER_CB), lambda j: (0, nblk + j))] + [_const_spec((half, half))] * 4,
        out_specs=[out_blk] * 4 + [pl.BlockSpec((2, FILTER_CB), lambda j: (0, j))],
        out_shape=[spec_shape] * 4 + [jax.ShapeDtypeStruct((2, nc), F32)],
        scratch_shapes=[pltpu.VMEM((2, half, HY_FFN), F32)],
        compiler_params=_params("arbitrary"),
        name=f"hyena_filters_{seq}",
    )(w1t, w1c, w1s, b1, w2, b2, freq, w3, w3, *tables[:4])


def _hyena_kernel(*refs, seq, rb):
    xs = refs[0:8]
    conv_w, conv_c = refs[8:11], refs[11:14]
    spectra = (refs[14:20], refs[20:26])
    ce_ref, se_ref, co_ref, so_ref, cot_ref, sot_ref = refs[26:32]
    oe_ref, oo_ref, u_ref, g_ref, ub_ref, pq_ref = refs[32:38]
    half = seq // 2
    row = lax.broadcasted_iota(jnp.int32, (half, 1), 0)
    alt = (1 - 2 * (row & 1)).astype(F32)

    def short_conv(dst_ref, xe_ref, xo_ref, w_ref, c_ref):
        xe = xe_ref[...].astype(F32)
        xo = xo_ref[...].astype(F32)
        xo_prev = jnp.where(row == 0, 0.0, pltpu.roll(xo, 1, 0))
        xe_next = jnp.where(row == half - 1, 0.0, pltpu.roll(xe, half - 1, 0))
        w0, w1, w2 = w_ref[0:1, :], w_ref[1:2, :], w_ref[2:3, :]
        dst_ref[0] = xo_prev * w0 + xe * w1 + xo * w2 + c_ref[...]
        dst_ref[1] = xe * w0 + xo * w1 + xe_next * w2 + c_ref[...]

    def long_conv(alo_ref, blo_ref, ahi_ref, bhi_ref, mid_ref, skip_ref, finish):
        c_mid = jnp.sum(u_ref[0] * alt, axis=0, keepdims=True)
        s_mid = jnp.sum(u_ref[1] * alt, axis=0, keepdims=True)
        p_mid = c_mid * mid_ref[0:1, :] + s_mid * mid_ref[1:2, :]
        q_mid = s_mid * mid_ref[0:1, :] - c_mid * mid_ref[1:2, :]
        for r in range(0, half, rb):
            rows = pl.ds(r, rb)
            c_even = jnp.dot(ce_ref[rows, :], ub_ref[0], preferred_element_type=F32)
            c_odd = jnp.dot(co_ref[rows, :], ub_ref[1], preferred_element_type=F32)
            s_even = jnp.dot(se_ref[rows, :], ub_ref[0], preferred_element_type=F32)
            s_odd = jnp.dot(so_ref[rows, :], ub_ref[1], preferred_element_type=F32)
            c_lo, c_hi = c_even + c_odd, c_even - c_odd
            s_lo, s_hi = s_even + s_odd, s_odd - s_even
            alo, blo, ahi, bhi = alo_ref[rows, :], blo_ref[rows, :], ahi_ref[rows, :], bhi_ref[rows, :]
            p_lo, q_lo = c_lo * alo + s_lo * blo, s_lo * alo - c_lo * blo
            p_hi, q_hi = c_hi * ahi + s_hi * bhi, s_hi * ahi - c_hi * bhi
            pq_ref[0, rows, :] = (p_lo + p_hi).astype(BF16)
            pq_ref[1, rows, :] = (q_lo - q_hi).astype(BF16)
            pq_ref[2, rows, :] = (p_lo - p_hi).astype(BF16)
            pq_ref[3, rows, :] = (q_lo + q_hi).astype(BF16)
        for r in range(0, half, rb):
            rows = pl.ds(r, rb)
            ye = (jnp.dot(ce_ref[rows, :], pq_ref[0], preferred_element_type=F32)
                  + jnp.dot(se_ref[rows, :], pq_ref[1], preferred_element_type=F32)
                  + alt[r:r + rb] * p_mid + u_ref[0, rows, :] * skip_ref[...])
            yo = (jnp.dot(cot_ref[rows, :], pq_ref[2], preferred_element_type=F32)
                  + jnp.dot(sot_ref[rows, :], pq_ref[3], preferred_element_type=F32)
                  + alt[r:r + rb] * q_mid + u_ref[1, rows, :] * skip_ref[...])
            finish(rows, g_ref[0, rows, :] * ye, g_ref[1, rows, :] * yo)

    def keep(rows, ye, yo):
        for p, y in enumerate((ye, yo)):
            u_ref[p, rows, :] = y
            ub_ref[p, rows, :] = y.astype(BF16)

    def emit(rows, ye, yo):
        oe_ref[rows, :] = (ye * xs[6][rows, :].astype(F32)).astype(oe_ref.dtype)
        oo_ref[rows, :] = (yo * xs[7][rows, :].astype(F32)).astype(oo_ref.dtype)

    short_conv(u_ref, xs[4], xs[5], conv_w[2], conv_c[2])
    ub_ref[...] = u_ref[...].astype(BF16)
    short_conv(g_ref, xs[0], xs[1], conv_w[0], conv_c[0])
    long_conv(*spectra[0], keep)
    short_conv(g_ref, xs[2], xs[3], conv_w[1], conv_c[1])
    long_conv(*spectra[1], emit)


def _hyena_branch(act2, batch, seq, conv_w, conv_b, hy_bias, filters, tables):
    cb = 256 if seq > 512 else D_HY
    half = seq // 2
    rb = min(half, 512)
    nj = D_HY // cb
    t2 = batch * half
    conv_b = conv_b.reshape(1, 3 * D_HY)
    skip = hy_bias.reshape(1, HY_ORDER * D_HY)
    alo, blo, ahi, bhi, mid = filters

    def act_specs(col512):
        off = col512 * (CB // cb)
        return [pl.BlockSpec((half, cb), lambda j, b, o=off + p * (D_IN // cb): (b, o + j)) for p in range(2)]

    def chan_spec(rows, part):
        off = part * nj
        return pl.BlockSpec((rows, cb), lambda j, b: (0, off + j), pipeline_mode=pl.Buffered(1))

    in_specs = act_specs(COL_HY_X1) + act_specs(COL_HY_X2) + act_specs(COL_HY_V) + act_specs(COL_HY_Z)
    in_specs += [chan_spec(3, part) for part in range(3)] + [chan_spec(1, part) for part in range(3)]
    args = [act2] * 8 + [conv_w] * 3 + [conv_b] * 3
    for order in range(HY_ORDER):
        in_specs += [chan_spec(half, order)] * 4 + [chan_spec(2, order), chan_spec(1, order)]
        args += [alo, blo, ahi, bhi, mid, skip]
    in_specs += [_const_spec((half, half))] * 6
    args += list(tables)
    out_spec = pl.BlockSpec((half, cb), lambda j, b: (b, j))
    out_shape = jax.ShapeDtypeStruct((t2, D_HY), BF16)
    return pl.pallas_call(
        functools.partial(_hyena_kernel, seq=seq, rb=rb),
        grid=(nj, batch),
        in_specs=in_specs,
        out_specs=[out_spec, out_spec],
        out_shape=[out_shape, out_shape],
        scratch_shapes=[pltpu.VMEM((2, half, cb), F32), pltpu.VMEM((2, half, cb), F32),
                        pltpu.VMEM((2, half, cb), BF16), pltpu.VMEM((4, half, cb), BF16)],
        compiler_params=_params("parallel", "parallel"),
        name=f"hyena_{seq}",
    )(*args)


def _rope_kernel(cos_ref, sin_ref, *, tr):
    lane = lax.broadcasted_iota(jnp.int32, (1, D_AT), 1)
    dim = lane & (AT_DH - 1)
    pair = ((dim & (AT_DH // 2 - 1)) >> 1).astype(F32)
    inv = jnp.exp(pair * (-(2.0 / (AT_DH // 2)) * math.log(ROPE_BASE)))
    pos = lax.broadcasted_iota(jnp.int32, (tr, 1), 0) + pl.program_id(0) * tr
    coord = jnp.where(dim < AT_DH // 2, pos >> GRID_W_LOG2, pos & (GRID_W - 1)).astype(F32)
    ang = coord * inv
    cos_ref[...] = jnp.cos(ang)
    sin_ref[...] = jnp.where((lane & 1) == 0, -jnp.sin(ang), jnp.sin(ang))


def _rope_tables(seq):
    tr = 256
    spec = pl.BlockSpec((tr, D_AT), lambda i: (i, 0))
    shape = jax.ShapeDtypeStruct((seq, D_AT), F32)
    return pl.pallas_call(
        functools.partial(_rope_kernel, tr=tr),
        grid=(seq // tr,),
        out_specs=[spec, spec],
        out_shape=[shape, shape],
        compiler_params=_params("parallel"),
        name="rope_tables",
    )()


def _nt_dot(a, b):
    return lax.dot_general(a, b, (((1,), (1,)), ((), ())), preferred_element_type=F32)


ONES_ROWS = 8


def _diff_lambda(lq_ref, lk_ref, lam_init):
    lqk = lq_ref[...] * lk_ref[...]
    return (jnp.exp(jnp.sum(lqk[0:1, :], axis=-1, keepdims=True))
            - jnp.exp(jnp.sum(lqk[1:2, :], axis=-1, keepdims=True)) + lam_init)


def _attn_ctx_kernel(q_ref, k_ref, v_ref, z_ref, lq_ref, lk_ref, sg_ref, o_ref, vt_ref, *, lam_init):
    seq = k_ref.shape[0]
    vt_ref[0:D_AT, :] = v_ref[...].astype(F32).T.astype(BF16)
    vt_ref[D_AT:D_AT + ONES_ROWS, :] = jnp.ones((ONES_ROWS, seq), BF16)
    lam = _diff_lambda(lq_ref, lk_ref, lam_init)
    ones = vt_ref[D_AT:D_AT + ONES_ROWS, :]

    for h in range(AT_HEADS):
        hv = slice(h * AT_DV, (h + 1) * AT_DV)

        def unnormalised(c):
            cols = slice(h * AT_DV + c * AT_DH, h * AT_DV + (c + 1) * AT_DH)
            s = _nt_dot(k_ref[:, cols], q_ref[:, cols])
            e = jnp.exp2(s - jnp.max(s, axis=0, keepdims=True)).astype(BF16)
            o = jnp.dot(vt_ref[hv, :], e, preferred_element_type=F32)
            den = jnp.dot(ones, e, preferred_element_type=F32)
            return o, den[0:1, :]

        o1, den1 = unnormalised(0)
        o2, den2 = unnormalised(1)
        o = o1 * (1.0 / den1) - o2 * (lam / den2)
        o = o * lax.rsqrt(jnp.mean(o * o, axis=0, keepdims=True) + NORM_EPS)
        o = o.T * sg_ref[...] * (1.0 - lam_init)
        o_ref[:, hv] = (o * z_ref[:, hv].astype(F32)).astype(o_ref.dtype)


def _attn_latent_kernel(q_ref, k_ref, v_ref, z_ref, lq_ref, lk_ref, sg_ref, ck_ref, cv_ref, o_ref,
                        *, lam_init):
    lam = _diff_lambda(lq_ref, lk_ref, lam_init)

    def scores(h, c):
        cols = slice(h * AT_DV + c * AT_DH, h * AT_DV + (c + 1) * AT_DH)
        q = q_ref[:, cols]
        return _nt_dot(q, k_ref[:, cols]), _nt_dot(q, ck_ref[:, cols].astype(BF16))

    def softmax_parts(s, sc):
        m = jnp.maximum(jnp.max(s, axis=-1, keepdims=True), jnp.max(sc, axis=-1, keepdims=True))
        e = jnp.exp2(s - m)
        ec = jnp.exp2(sc - m)
        return e, ec, jnp.sum(e, axis=-1, keepdims=True) + jnp.sum(ec, axis=-1, keepdims=True)

    def finish(h, part1, part2):
        e1, ec1, den1 = part1
        e2, ec2, den2 = part2
        ratio = lam * den1 / den2
        hv = slice(h * AT_DV, (h + 1) * AT_DV)
        o = (jnp.dot((e1 - e2 * ratio).astype(BF16), v_ref[:, hv], preferred_element_type=F32)
             + jnp.dot((ec1 - ec2 * ratio).astype(BF16), cv_ref[:, hv].astype(BF16),
                       preferred_element_type=F32)) * (1.0 / den1)
        o = o * lax.rsqrt(jnp.mean(o * o, axis=-1, keepdims=True) + NORM_EPS) * sg_ref[...]
        o_ref[:, hv] = (o * (1.0 - lam_init) * z_ref[:, hv].astype(F32)).astype(o_ref.dtype)

    units = [(h, c) for h in range(AT_HEADS) for c in range(2)]
    ahead = scores(*units[0])
    parts = [None, None]
    for i, (h, c) in enumerate(units):
        current = ahead
        if i + 1 < len(units):
            ahead = scores(*units[i + 1])
        parts[c] = softmax_parts(*current)
        if c == 1:
            finish(h, parts[0], parts[1])


def _attention(act, batch, seq, lam_q, lam_k, subln_g, lam_init, ctx, layer):
    t = batch * seq
    tq = 256
    nq = seq // tq
    small = lambda a: pl.BlockSpec(a.shape, lambda b, i: (0,) * a.ndim)
    in_specs = [pl.BlockSpec((tq, CB), lambda b, i: (b * nq + i, COL_Q)),
                pl.BlockSpec((seq, CB), lambda b, i: (b, COL_K)),
                pl.BlockSpec((seq, CB), lambda b, i: (b, COL_V)),
                pl.BlockSpec((tq, CB), lambda b, i: (b * nq + i, COL_AT_Z)),
                small(lam_q), small(lam_k), small(subln_g)]
    args = [act, act, act, act, lam_q, lam_k, subln_g]
    if ctx is None:
        assert nq == 1
        body = functools.partial(_attn_ctx_kernel, lam_init=lam_init)
        scratch = [pltpu.VMEM((D_AT + ONES_ROWS, seq), BF16)]
    else:
        ck, cv = ctx
        cspec = pl.BlockSpec((None, None, ck.shape[2], D_AT), lambda b, i: (b, layer, 0, 0))
        in_specs += [cspec, cspec]
        args += [ck, cv]
        body = functools.partial(_attn_latent_kernel, lam_init=lam_init)
        scratch = []
    return pl.pallas_call(
        body,
        grid=(batch, nq),
        in_specs=in_specs,
        out_specs=pl.BlockSpec((tq, D_AT), lambda b, i: (b * nq + i, 0)),
        out_shape=jax.ShapeDtypeStruct((t, D_AT), BF16),
        scratch_shapes=scratch,
        compiler_params=_params("parallel", "parallel"),
        name=f"diff_attn_{seq}",
    )(*args)


def _pool_kernel(u_ref, z_ref, w_ref, sc_ref, o_ref, *, seq):
    padded = seq + 2 * POOL_PAD
    t = lax.broadcasted_iota(jnp.int32, (seq, 1), 0)
    zeros = jnp.zeros((POOL_PAD, POOL_GC), F32)
    for g, win in enumerate(POOL_WINDOWS):
        cols = slice(g * POOL_GC, (g + 1) * POOL_GC)
        u = u_ref[:, cols].astype(F32)
        s = jnp.concatenate([zeros, u, zeros], axis=0)
        s = s + pltpu.roll(s, 1, 0)
        half = 1
        while 2 * half < win:
            s = pltpu.roll(s, padded - half, 0) + pltpu.roll(s, half, 0)
            half *= 2
        lo = jnp.maximum(t - win // 2, 0)
        hi = jnp.minimum(t + win // 2, seq)
        m = s[POOL_PAD:POOL_PAD + seq] / (hi - lo).astype(F32) - u
        y = jnp.dot(m.astype(BF16), w_ref[g].astype(BF16), preferred_element_type=F32)
        o_ref[:, cols] = (y * sc_ref[:, cols] * z_ref[:, cols].astype(F32)).astype(o_ref.dtype)


def _pool_branch(proj, batch, seq, pool_w, pool_scale):
    t = batch * seq
    return pl.pallas_call(
        functools.partial(_pool_kernel, seq=seq),
        grid=(batch,),
        in_specs=[pl.BlockSpec((seq, CB), lambda b: (b, COL_PL_U)),
                  pl.BlockSpec((seq, CB), lambda b: (b, COL_PL_Z)),
                  pl.BlockSpec(pool_w.shape, lambda b: (0, 0, 0)),
                  pl.BlockSpec((1, D_PL), lambda b: (0, 0))],
        out_specs=pl.BlockSpec((seq, D_PL), lambda b: (b, 0)),
        out_shape=jax.ShapeDtypeStruct((t, D_PL), BF16),
        compiler_params=_params("parallel"),
        name=f"pool_{seq}",
    )(proj, proj, pool_w, pool_scale)


def _merge_kernel(x_ref, mod_ref, yhe_ref, yho_ref, ya_ref, yp_ref, *refs):
    gates = (refs[0:3], refs[3:6])
    wh_ref, wa_ref, wp_ref, wo_ref, o_ref = refs[6:]
    d = D_MODEL
    for p, yh_ref in enumerate((yhe_ref, yho_ref)):
        g0_ref, g1_ref, g2_ref = gates[p]
        branch = slice(p * CB, (p + 1) * CB)
        merged = (g0_ref[...].astype(F32) * jnp.dot(yh_ref[...], wh_ref[...], preferred_element_type=F32)
                  + g1_ref[...].astype(F32) * jnp.dot(ya_ref[:, branch], wa_ref[...], preferred_element_type=F32)
                  + g2_ref[...].astype(F32) * jnp.dot(yp_ref[:, branch], wp_ref[...], preferred_element_type=F32))
        out = jnp.dot(merged.astype(BF16), wo_ref[...], preferred_element_type=F32)
        cols = slice(p * d, (p + 1) * d)
        o_ref[:, cols] = x_ref[:, cols] + mod_ref[:, 2 * d:3 * d] * out


def _merge(x2, mod, mod_row0, pairs_per_seq, act2, y_hy, y_at2, y_pl2, w_hy_o, w_at_o, w_pl_o, w_out, tm=256):
    t2 = x2.shape[0]
    blocks_per_seq = pairs_per_seq // tm
    d = D_MODEL
    hy = pl.BlockSpec((tm, CB), lambda i: (i, 0))
    pair = pl.BlockSpec((tm, 2 * CB), lambda i: (i, 0))
    gates = [pl.BlockSpec((tm, d), lambda i, c=p * (D_IN // d) + COL_MG // 2 + g: (i, c))
             for p in range(2) for g in range(3)]
    return pl.pallas_call(
        _merge_kernel,
        grid=(t2 // tm,),
        in_specs=[pl.BlockSpec((tm, 2 * d), lambda i: (i, 0)),
                  pl.BlockSpec((None, 1, 3 * d), lambda i: (mod_row0 + i // blocks_per_seq, 0, 0)),
                  hy, hy, pair, pair] + gates
                 + [_const_spec((CB, d)), _const_spec((CB, d)), _const_spec((CB, d)), _const_spec((d, d))],
        out_specs=pl.BlockSpec((tm, 2 * d), lambda i: (i, 0)),
        out_shape=jax.ShapeDtypeStruct((t2, 2 * d), F32),
        compiler_params=_params("parallel"),
        name="merge",
    )(x2, mod, y_hy[0], y_hy[1], y_at2, y_pl2, *([act2] * 6), w_hy_o, w_at_o, w_pl_o, w_out)


def _trunk_layer(x, batch, seq, mod, mod_row0, p, filters, dft, rope, ctx, layer, lam_init):
    mod_rows = seq if mod_row0 else batch * seq
    act, kv = _in_proj(x, mod, mod_row0, mod_rows, p["norm_g"], p["w_in"], p["gq"], p["gk"], rope, seq,
                       keep_kv=rope is None)
    t = batch * seq
    act2 = act.reshape(t // 2, 2 * D_IN)
    y_hy = _hyena_branch(act2, batch, seq, p["hy_conv_w"], p["hy_conv_b"], p["hy_bias"], filters, dft)
    y_at = _attention(act, batch, seq, p["lam_q"], p["lam_k"], p["subln_g"], lam_init, ctx, layer)
    y_pl = _pool_branch(act, batch, seq, p["pool_w"], p["pool_scale"])
    out = _merge(x.reshape(t // 2, 2 * D_MODEL), mod, mod_row0, mod_rows // 2, act2, y_hy,
                 y_at.reshape(t // 2, 2 * D_AT), y_pl.reshape(t // 2, 2 * D_PL),
                 p["w_hy_o"], p["w_at_o"], p["w_pl_o"], p["w_out"])
    return out.reshape(t, D_MODEL), kv


def kernel(x_prompt, x_sample, cache_k, cache_v, c, c_ctx, norm_g, w_ada, b_ada, w_in, hy_conv_w, hy_conv_b,
           hy_f_w1, hy_f_b1, hy_f_w2, hy_f_b2, hy_f_w3, hy_freq, hy_bias, q_norm_g, k_norm_g, lam_q, lam_k,
           subln_g, pool_w, pool_scale, w_hy_o, w_at_o, w_pl_o, w_out):
    batch, seq, d = x_prompt.shape
    dec_batch, dec_seq, _ = x_sample.shape
    past = cache_k.shape[2]
    assert d == D_MODEL and dec_batch + 1 <= COND_ROWS and dec_seq % GRID_W == 0

    cond = jnp.concatenate([c_ctx[None, :], c, jnp.zeros((COND_ROWS - 1 - dec_batch, d), F32)], axis=0)
    mod = _modulation(cond, w_ada, b_ada).reshape(DEPTH, COND_ROWS, 1, 3 * d)
    dft = {s: _dft_tables(s) for s in (seq, dec_seq)}
    rope = _rope_tables(dec_seq)
    ctx = (cache_k.reshape(dec_batch, DEPTH, past, D_AT), cache_v.reshape(dec_batch, DEPTH, past, D_AT))

    yp = x_prompt.reshape(batch * seq, d)
    ys = x_sample.reshape(dec_batch * dec_seq, d)
    new_ks, new_vs = [], []
    for l in range(DEPTH):
        p = dict(norm_g=norm_g[l][None, :], w_in=w_in[l].astype(BF16),
                 hy_conv_w=hy_conv_w[l], hy_conv_b=hy_conv_b[l], hy_bias=hy_bias[l],
                 gq=jnp.tile(q_norm_g[l], D_AT // AT_DH)[None, :], gk=jnp.tile(k_norm_g[l], D_AT // AT_DH)[None, :],
                 lam_q=lam_q[l], lam_k=lam_k[l], subln_g=subln_g[l][None, :],
                 pool_w=pool_w[l], pool_scale=pool_scale[l][None, :],
                 w_hy_o=w_hy_o[l].astype(BF16), w_at_o=w_at_o[l].astype(BF16),
                 w_pl_o=w_pl_o[l].astype(BF16), w_out=w_out[l].astype(BF16))
        filters = {s: _hyena_filters(s, hy_f_w1[l], hy_f_b1[l], hy_f_w2[l], hy_f_b2[l], hy_f_w3[l], hy_freq[l],
                                     dft[s]) for s in (seq, dec_seq)}
        lam_init = 0.8 - 0.6 * math.exp(-0.3 * l)
        yp, kv = _trunk_layer(yp, batch, seq, mod[l], 0, p, filters[seq], dft[seq], None, None, l, lam_init)
        new_ks.append(kv[0].reshape(batch, seq, AT_HEADS, 2, AT_DH))
        new_vs.append(kv[1].reshape(batch, seq, AT_HEADS, AT_DV))
        ys, _ = _trunk_layer(ys, dec_batch, dec_seq, mod[l], 1, p, filters[dec_seq], dft[dec_seq], rope, ctx, l,
                             lam_init)
    return (yp.reshape(batch, seq, d), ys.reshape(dec_batch, dec_seq, d),
            jnp.stack(new_ks, axis=1), jnp.stack(new_vs, axis=1))
```

```python
import functools
import math

import jax
import jax.numpy as jnp
from jax import lax
from jax.experimental import pallas as pl
from jax.experimental.pallas import tpu as pltpu

F32 = jnp.float32
BF16 = jnp.bfloat16
HIGHEST = lax.Precision.HIGHEST

D_MODEL = 1024
DEPTH = 2
GRID_W = 64
GRID_W_LOG2 = GRID_W.bit_length() - 1
assert GRID_W == 1 << GRID_W_LOG2
D_HY = 512
HY_ORDER = 2
HY_EMB = 33
HY_BANDS = (HY_EMB - 1) // 2
HY_FFN = 64
HY_FAST_DECAY = 0.3
HY_SLOW_DECAY = 1.5
HY_TARGET = 1e-2
AT_HEADS = 4
AT_DH = 64
AT_DV = 2 * AT_DH
D_AT = AT_HEADS * AT_DV
ROPE_BASE = 10000.0
Q_SCALE = AT_DH ** -0.5 * math.log2(math.e)
D_PL = 512
POOL_WINDOWS = (2, 4, 8, 16)
POOL_GC = D_PL // 4
POOL_PAD = 16
NORM_EPS = 1e-6
D_IN = 8192
COND_ROWS = 8

CB = 512
COL_HY_X1, COL_HY_X2, COL_HY_V, COL_HY_Z = 0, 1, 2, 3
COL_Q, COL_K, COL_V, COL_AT_Z, COL_PL_U, COL_PL_Z = 4, 5, 6, 7, 8, 9
COL_MG = 10

VMEM_LIMIT_BYTES = 56 * 1024 * 1024
LANES = 128


def _params(*sem):
    return pltpu.CompilerParams(dimension_semantics=sem, vmem_limit_bytes=VMEM_LIMIT_BYTES)


def _const_spec(shape):
    return pl.BlockSpec(shape, lambda *_: (0,) * len(shape), pipeline_mode=pl.Buffered(1))


def _silu(x):
    return x * jax.nn.sigmoid(x)


def _bdot(a, b):
    return jnp.dot(a.astype(BF16), b.astype(BF16), preferred_element_type=F32)


def _fdot(a, b):
    return jnp.dot(a, b, precision=HIGHEST, preferred_element_type=F32)


def _mod_kernel(cond_ref, w_ref, b_ref, o_ref):
    o_ref[...] = _fdot(_silu(cond_ref[...]), w_ref[...]) + b_ref[...]


def _modulation(cond, w_ada, b_ada):
    d = D_MODEL
    return pl.pallas_call(
        _mod_kernel,
        grid=(DEPTH, 3),
        in_specs=[
            pl.BlockSpec((COND_ROWS, d), lambda l, n: (0, 0)),
            pl.BlockSpec((None, d, d), lambda l, n: (l, 0, n)),
            pl.BlockSpec((None, 1, d), lambda l, n: (l, 0, n)),
        ],
        out_specs=pl.BlockSpec((None, COND_ROWS, d), lambda l, n: (l, 0, n)),
        out_shape=jax.ShapeDtypeStruct((DEPTH, COND_ROWS, 3 * d), F32),
        compiler_params=_params("parallel", "parallel"),
        name="modulation",
    )(cond, w_ada, b_ada.reshape(DEPTH, 1, 3 * d))


def _group_mean_matrix():
    r = lax.broadcasted_iota(jnp.int32, (D_AT, D_AT), 0) // AT_DH
    c = lax.broadcasted_iota(jnp.int32, (D_AT, D_AT), 1) // AT_DH
    return jnp.where(r == c, 1.0 / AT_DH, 0.0).astype(BF16)


def _group_rms_norm(x, g, mean_mat):
    x2 = x * x
    hi = x2.astype(BF16)
    lo = (x2 - hi.astype(F32)).astype(BF16)
    ms = (jnp.dot(hi, mean_mat, preferred_element_type=F32)
          + jnp.dot(lo, mean_mat, preferred_element_type=F32))
    return x * lax.rsqrt(ms + NORM_EPS) * g


def _rotate(x, cos, sin_signed):
    lane = lax.broadcasted_iota(jnp.int32, (1, D_AT), 1)
    partner = jnp.where((lane & 1) == 0, pltpu.roll(x, D_AT - 1, 1), pltpu.roll(x, 1, 1))
    return x * cos + partner * sin_signed


def _in_kernel(*refs, rope, keep_kv):
    x_ref, mod_ref, g_ref, w_ref, gq_ref, gk_ref = refs[:6]
    refs = refs[6:]
    if rope:
        cos_ref, sin_ref = refs[:2]
        refs = refs[2:]
    o_ref = refs[0]
    h_ref = refs[-1]

    x = x_ref[...]
    y = x * lax.rsqrt(jnp.mean(x * x, axis=-1, keepdims=True) + NORM_EPS) * g_ref[...]
    h_ref[...] = (y * (1.0 + mod_ref[:, D_MODEL:2 * D_MODEL]) + mod_ref[:, 0:D_MODEL]).astype(BF16)

    def chunk(c):
        return jnp.dot(h_ref[...], w_ref[:, c * CB:(c + 1) * CB], preferred_element_type=F32)

    def put(c, val):
        o_ref[:, c * CB:(c + 1) * CB] = val.astype(o_ref.dtype)

    for c in (COL_HY_X1, COL_HY_X2, COL_HY_V, COL_PL_U):
        put(c, chunk(c))
    for c in (COL_HY_Z, COL_AT_Z, COL_PL_Z):
        put(c, _silu(chunk(c)))
    for c in range(COL_MG, D_IN // CB):
        put(c, jax.nn.sigmoid(chunk(c)))

    mean_mat = _group_mean_matrix()
    q = _group_rms_norm(chunk(COL_Q), gq_ref[...], mean_mat)
    k = _group_rms_norm(chunk(COL_K), gk_ref[...], mean_mat)
    v = chunk(COL_V)
    if keep_kv:
        refs[1][...] = k
        refs[2][...] = v
    if rope:
        q = _rotate(q, cos_ref[...], sin_ref[...])
        k = _rotate(k, cos_ref[...], sin_ref[...])
    put(COL_Q, q * Q_SCALE)
    put(COL_K, k)
    put(COL_V, v)


def _in_proj(x, mod, mod_row0, rows_per_seq, norm_g, w_in_bf16, gq, gk, rope, seq, keep_kv, tm=512):
    t = x.shape[0]
    blocks_per_seq = rows_per_seq // tm
    vec = pl.BlockSpec((1, D_AT), lambda i: (0, 0))
    in_specs = [pl.BlockSpec((tm, D_MODEL), lambda i: (i, 0)),
                pl.BlockSpec((None, 1, 3 * D_MODEL), lambda i: (mod_row0 + i // blocks_per_seq, 0, 0)),
                pl.BlockSpec((1, D_MODEL), lambda i: (0, 0)),
                _const_spec((D_MODEL, D_IN)), vec, vec]
    args = [x, mod, norm_g, w_in_bf16, gq, gk]
    if rope is not None:
        per_seq = seq // tm
        tab = pl.BlockSpec((tm, D_AT), lambda i: (i % per_seq, 0))
        in_specs += [tab, tab]
        args += list(rope)
    out_specs = [pl.BlockSpec((tm, D_IN), lambda i: (i, 0))]
    out_shape = [jax.ShapeDtypeStruct((t, D_IN), BF16)]
    if keep_kv:
        out_specs += [pl.BlockSpec((tm, D_AT), lambda i: (i, 0))] * 2
        out_shape += [jax.ShapeDtypeStruct((t, D_AT), F32)] * 2
    out = pl.pallas_call(
        functools.partial(_in_kernel, rope=rope is not None, keep_kv=keep_kv),
        grid=(t // tm,),
        in_specs=in_specs,
        out_specs=out_specs,
        out_shape=out_shape,
        scratch_shapes=[pltpu.VMEM((tm, D_MODEL), BF16)],
        compiler_params=_params("parallel"),
        name="in_proj_latent" if rope is not None else "in_proj_ctx",
    )(*args)
    return (out[0], (out[1], out[2])) if keep_kv else (out[0], None)


TABLE_ROWS = 32


def _dft_kernel(ce_ref, se_ref, co_ref, so_ref, cot_ref, sot_ref, bc_ref, bs_ref, oc_ref, os_ref, *, seq):
    half = seq // 2
    col = lax.broadcasted_iota(jnp.int32, (1, half), 1)
    base = lax.broadcasted_iota(jnp.int32, (half // TABLE_ROWS, 1), 0) * TABLE_ROWS
    off = lax.broadcasted_iota(jnp.int32, (TABLE_ROWS, 1), 0)

    def trig(c_ref, s_ref, slot, product):
        ang = (product & (2 * seq - 1)).astype(F32) * (math.pi / seq)
        c_ref[slot] = jnp.cos(ang)
        s_ref[slot] = jnp.sin(ang)

    trig(bc_ref, bs_ref, 0, 2 * base * col)
    trig(bc_ref, bs_ref, 1, base * (2 * col + 1))
    trig(bc_ref, bs_ref, 2, (2 * base + 1) * col)
    trig(oc_ref, os_ref, 0, 2 * off * col)
    trig(oc_ref, os_ref, 1, off * (2 * col + 1))
    outs = ((ce_ref, se_ref, 0, 0), (co_ref, so_ref, 1, 1), (cot_ref, sot_ref, 2, 0))

    def block(i, carry):
        rows = pl.ds(pl.multiple_of(i * TABLE_ROWS, TABLE_ROWS), TABLE_ROWS)
        for c_ref, s_ref, b, o in outs:
            bc = bc_ref[b, pl.ds(i, 1), :]
            bs = bs_ref[b, pl.ds(i, 1), :]
            oc = oc_ref[o]
            osn = os_ref[o]
            c_ref[rows, :] = (bc * oc - bs * osn).astype(BF16)
            s_ref[rows, :] = (bs * oc + bc * osn).astype(BF16)
        return carry

    lax.fori_loop(0, half // TABLE_ROWS, block, 0)


def _dft_tables(seq):
    half = seq // 2
    shape = jax.ShapeDtypeStruct((half, half), BF16)
    return pl.pallas_call(
        functools.partial(_dft_kernel, seq=seq),
        out_shape=[shape] * 6,
        scratch_shapes=[pltpu.VMEM((3, half // TABLE_ROWS, half), F32), pltpu.VMEM((3, half // TABLE_ROWS, half), F32),
                        pltpu.VMEM((2, TABLE_ROWS, half), F32), pltpu.VMEM((2, TABLE_ROWS, half), F32)],
        compiler_params=pltpu.CompilerParams(vmem_limit_bytes=VMEM_LIMIT_BYTES),
        name=f"dft_tables_{seq}",
    )()


FILTER_CB = 256


def _filter_kernel(w1t_ref, w1c_ref, w1s_ref, b1_ref, w2_ref, b2_ref, fq_ref, w3f_ref, w3b_ref,
                   ce_ref, se_ref, co_ref, so_ref, alo_ref, blo_ref, ahi_ref, bhi_ref, mid_ref, h_ref, *, seq):
    half = seq // 2
    row = lax.broadcasted_iota(jnp.int32, (half, 1), 0)

    @pl.when(pl.program_id(0) == 0)
    def _():
        band = lax.broadcasted_iota(jnp.int32, (HY_BANDS, 1), 0).astype(F32)
        f = 1e-4 + band * ((HY_BANDS - 1 - 1e-4) / (HY_BANDS - 1))
        for p in range(2):
            pos = (2 * lax.broadcasted_iota(jnp.int32, (1, half), 1) + p).astype(F32)
            fw = f * (pos * (2.0 * math.pi / seq))
            pre = (w1t_ref[...] * (pos * (1.0 / (seq - 1))) + _fdot(w1c_ref[...], jnp.cos(fw))
                   - _fdot(w1s_ref[...], jnp.sin(fw)) + b1_ref[...])
            h = jnp.sin(fq_ref[:, 0:1] * pre)
            h = jnp.sin(fq_ref[:, 1:2] * (_fdot(w2_ref[...], h) + b2_ref[...]))
            h_ref[p] = h.T

    chan = lax.broadcasted_iota(jnp.int32, (1, FILTER_CB), 1) + (pl.program_id(0) % (D_HY // FILTER_CB)) * FILTER_CB
    max_decay = math.log(HY_TARGET) / HY_FAST_DECAY
    min_decay = math.log(HY_TARGET) / HY_SLOW_DECAY
    deltas = jnp.abs(min_decay + chan.astype(F32) * ((max_decay - min_decay) / (D_HY - 1)))

    def taps(p):
        decay = jnp.exp((2 * row + p).astype(F32) * (-1.0 / (seq - 1)) * deltas)
        hf = _fdot(h_ref[p], w3f_ref[...]) * decay
        hb = _fdot(h_ref[p], w3b_ref[...]) * decay
        if p == 0:
            hb = jnp.where(row == 0, 0.0, hb)
        return hf, hb

    hf0, hb0 = taps(0)
    hf1, hb1 = taps(1)
    inv = 1.0 / (jnp.sum(jnp.abs(hf0) + jnp.abs(hf1) + jnp.abs(hb0) + jnp.abs(hb1), axis=0, keepdims=True) + NORM_EPS)
    hs0, hs1 = (hf0 + hb0) * inv, (hf1 + hb1) * inv
    hd0, hd1 = (hf0 - hb0) * inv, (hf1 - hb1) * inv
    c_even = _bdot(ce_ref[...], hs0)
    c_odd = _bdot(co_ref[...], hs1)
    s_even = _bdot(se_ref[...], hd0)
    s_odd = _bdot(so_ref[...], hd1)
    wk = jnp.where(row == 0, 0.5 / seq, 1.0 / seq)
    alo_ref[...] = wk * (c_even + c_odd)
    ahi_ref[...] = wk * (c_even - c_odd)
    blo_ref[...] = -wk * (s_even + s_odd)
    bhi_ref[...] = wk * (s_even - s_odd)
    alt = (1 - 2 * (row & 1)).astype(F32)
    mid_ref[0:1, :] = jnp.sum(alt * hs0, axis=0, keepdims=True) * (1.0 / seq)
    mid_ref[1:2, :] = jnp.sum(alt * hd1, axis=0, keepdims=True) * (-1.0 / seq)


def _hyena_filters(seq, w1, b1, w2, b2, w3, freq, tables):
    nc = HY_ORDER * D_HY
    nblk = nc // FILTER_CB
    half = seq // 2
    small = lambda a: pl.BlockSpec(a.shape, lambda j: (0,) * a.ndim)
    w1t, w1c, w1s = w1[0:1].T, w1[1:1 + HY_BANDS].T, w1[1 + HY_BANDS:].T
    b1 = b1.reshape(HY_FFN, 1)
    b2 = b2.reshape(HY_FFN, 1)
    w2 = w2.T
    freq = freq.T
    out_blk = pl.BlockSpec((half, FILTER_CB), lambda j: (0, j))
    spec_shape = jax.ShapeDtypeStruct((half, nc), F32)
    return pl.pallas_call(
        functools.partial(_filter_kernel, seq=seq),
        grid=(nblk,),
        in_specs=[small(w1t), small(w1c), small(w1s), small(b1), small(w2), small(b2), small(freq),
                  pl.BlockSpec((HY_FFN, FILTER_CB), lambda j: (0, j)),
                  pl.BlockSpec((HY_FFN, FILTER_CB), lambda j: (0, nblk + j))] + [_const_spec((half, half))] * 4,
        out_specs=[out_blk] * 4 + [pl.BlockSpec((2, FILTER_CB), lambda j: (0, j))],
        out_shape=[spec_shape] * 4 + [jax.ShapeDtypeStruct((2, nc), F32)],
        scratch_shapes=[pltpu.VMEM((2, half, HY_FFN), F32)],
        compiler_params=_params("arbitrary"),
        name=f"hyena_filters_{seq}",
    )(w1t, w1c, w1s, b1, w2, b2, freq, w3, w3, *tables[:4])


def _hyena_kernel(*refs, seq, rb):
    x1_ref, x2_ref, v_ref, z_ref = refs[0:4]
    conv_w, conv_c = refs[4:7], refs[7:10]
    spectra = (refs[10:16], refs[16:22])
    ce_ref, se_ref, co_ref, so_ref, cot_ref, sot_ref = refs[22:28]
    o_ref, u_ref, g_ref, ub_ref, pq_ref = refs[28:33]
    half = seq // 2
    slabs = u_ref.shape[0]
    pos = lax.broadcasted_iota(jnp.int32, (seq, 1), 0)
    row = lax.broadcasted_iota(jnp.int32, (half, 1), 0)
    alt = (1 - 2 * (row & 1)).astype(F32)

    def short_conv(x_ref, w_ref, c_ref):
        x = x_ref[...].astype(F32)
        prev = jnp.where(pos == 0, 0.0, pltpu.roll(x, 1, 0))
        nxt = jnp.where(pos == seq - 1, 0.0, pltpu.roll(x, seq - 1, 0))
        return prev * w_ref[0:1, :] + x * w_ref[1:2, :] + nxt * w_ref[2:3, :] + c_ref[...]

    def put_u(rows, val):
        for s in range(slabs):
            u_ref[s, rows, :] = val[:, s * LANES:(s + 1) * LANES]

    def get_u(rows):
        return jnp.concatenate([u_ref[s, rows, :] for s in range(slabs)], axis=1)

    def long_conv(alo_ref, blo_ref, ahi_ref, bhi_ref, mid_ref, skip_ref, finish):
        for p in range(2):
            ub_ref[p] = get_u(pl.ds(p, half, stride=2)).astype(BF16)
        c_mid = jnp.sum(get_u(pl.ds(0, half, stride=2)) * alt, axis=0, keepdims=True)
        s_mid = jnp.sum(get_u(pl.ds(1, half, stride=2)) * alt, axis=0, keepdims=True)
        p_mid = c_mid * mid_ref[0:1, :] + s_mid * mid_ref[1:2, :]
        q_mid = s_mid * mid_ref[0:1, :] - c_mid * mid_ref[1:2, :]
        for r in range(0, half, rb):
            rows = pl.ds(r, rb)
            c_even = jnp.dot(ce_ref[rows, :], ub_ref[0], preferred_element_type=F32)
            c_odd = jnp.dot(co_ref[rows, :], ub_ref[1], preferred_element_type=F32)
            s_even = jnp.dot(se_ref[rows, :], ub_ref[0], preferred_element_type=F32)
            s_odd = jnp.dot(so_ref[rows, :], ub_ref[1], preferred_element_type=F32)
            c_lo, c_hi = c_even + c_odd, c_even - c_odd
            s_lo, s_hi = s_even + s_odd, s_odd - s_even
            alo, blo, ahi, bhi = alo_ref[rows, :], blo_ref[rows, :], ahi_ref[rows, :], bhi_ref[rows, :]
            p_lo, q_lo = c_lo * alo + s_lo * blo, s_lo * alo - c_lo * blo
            p_hi, q_hi = c_hi * ahi + s_hi * bhi, s_hi * ahi - c_hi * bhi
            pq_ref[0, rows, :] = (p_lo + p_hi).astype(BF16)
            pq_ref[1, rows, :] = (q_lo - q_hi).astype(BF16)
            pq_ref[2, rows, :] = (p_lo - p_hi).astype(BF16)
            pq_ref[3, rows, :] = (q_lo + q_hi).astype(BF16)
        for r in range(0, half, rb):
            rows = pl.ds(r, rb)
            even, odd = pl.ds(2 * r, rb, stride=2), pl.ds(2 * r + 1, rb, stride=2)
            ye = (jnp.dot(ce_ref[rows, :], pq_ref[0], preferred_element_type=F32)
                  + jnp.dot(se_ref[rows, :], pq_ref[1], preferred_element_type=F32)
                  + alt[r:r + rb] * p_mid + get_u(even) * skip_ref[...])
            yo = (jnp.dot(cot_ref[rows, :], pq_ref[2], preferred_element_type=F32)
                  + jnp.dot(sot_ref[rows, :], pq_ref[3], preferred_element_type=F32)
                  + alt[r:r + rb] * q_mid + get_u(odd) * skip_ref[...])
            put_u(even, ye)
            put_u(odd, yo)
            both = pl.ds(2 * r, 2 * rb)
            finish(both, g_ref[both, :] * get_u(both))

    def keep(rows, y):
        put_u(rows, y)

    def emit(rows, y):
        o_ref[rows, :] = (y * z_ref[rows, :].astype(F32)).astype(o_ref.dtype)

    put_u(pl.ds(0, seq), short_conv(v_ref, conv_w[2], conv_c[2]))
    g_ref[...] = short_conv(x1_ref, conv_w[0], conv_c[0])
    long_conv(*spectra[0], keep)
    g_ref[...] = short_conv(x2_ref, conv_w[1], conv_c[1])
    long_conv(*spectra[1], emit)


def _hyena_branch(act, batch, seq, conv_w, conv_b, hy_bias, filters, tables):
    cb = 256 if seq > 512 else D_HY
    half = seq // 2
    rb = min(half, 512)
    nj = D_HY // cb
    t = batch * seq
    conv_b = conv_b.reshape(1, 3 * D_HY)
    skip = hy_bias.reshape(1, HY_ORDER * D_HY)
    alo, blo, ahi, bhi, mid = filters

    def act_spec(col512):
        off = col512 * (CB // cb)
        return pl.BlockSpec((seq, cb), lambda j, b: (b, off + j))

    def chan_spec(rows, part):
        off = part * nj
        return pl.BlockSpec((rows, cb), lambda j, b: (0, off + j), pipeline_mode=pl.Buffered(1))

    in_specs = [act_spec(COL_HY_X1), act_spec(COL_HY_X2), act_spec(COL_HY_V), act_spec(COL_HY_Z)]
    in_specs += [chan_spec(3, part) for part in range(3)] + [chan_spec(1, part) for part in range(3)]
    args = [act] * 4 + [conv_w] * 3 + [conv_b] * 3
    for order in range(HY_ORDER):
        in_specs += [chan_spec(half, order)] * 4 + [chan_spec(2, order), chan_spec(1, order)]
        args += [alo, blo, ahi, bhi, mid, skip]
    in_specs += [_const_spec((half, half))] * 6
    args += list(tables)
    return pl.pallas_call(
        functools.partial(_hyena_kernel, seq=seq, rb=rb),
        grid=(nj, batch),
        in_specs=in_specs,
        out_specs=pl.BlockSpec((seq, cb), lambda j, b: (b, j)),
        out_shape=jax.ShapeDtypeStruct((t, D_HY), BF16),
        scratch_shapes=[pltpu.VMEM((cb // LANES, seq, LANES), F32), pltpu.VMEM((seq, cb), F32),
                        pltpu.VMEM((2, half, cb), BF16), pltpu.VMEM((4, half, cb), BF16)],
        compiler_params=_params("parallel", "parallel"),
        name=f"hyena_{seq}",
    )(*args)


def _rope_kernel(cos_ref, sin_ref, *, tr):
    lane = lax.broadcasted_iota(jnp.int32, (1, D_AT), 1)
    dim = lane & (AT_DH - 1)
    pair = ((dim & (AT_DH // 2 - 1)) >> 1).astype(F32)
    inv = jnp.exp(pair * (-(2.0 / (AT_DH // 2)) * math.log(ROPE_BASE)))
    pos = lax.broadcasted_iota(jnp.int32, (tr, 1), 0) + pl.program_id(0) * tr
    coord = jnp.where(dim < AT_DH // 2, pos >> GRID_W_LOG2, pos & (GRID_W - 1)).astype(F32)
    ang = coord * inv
    cos_ref[...] = jnp.cos(ang)
    sin_ref[...] = jnp.where((lane & 1) == 0, -jnp.sin(ang), jnp.sin(ang))


def _rope_tables(seq):
    tr = 256
    spec = pl.BlockSpec((tr, D_AT), lambda i: (i, 0))
    shape = jax.ShapeDtypeStruct((seq, D_AT), F32)
    return pl.pallas_call(
        functools.partial(_rope_kernel, tr=tr),
        grid=(seq // tr,),
        out_specs=[spec, spec],
        out_shape=[shape, shape],
        compiler_params=_params("parallel"),
        name="rope_tables",
    )()


def _nt_dot(a, b):
    return lax.dot_general(a, b, (((1,), (1,)), ((), ())), preferred_element_type=F32)


ONES_ROWS = 8


def _diff_lambda(lq_ref, lk_ref, lam_init):
    lqk = lq_ref[...] * lk_ref[...]
    return (jnp.exp(jnp.sum(lqk[0:1, :], axis=-1, keepdims=True))
            - jnp.exp(jnp.sum(lqk[1:2, :], axis=-1, keepdims=True)) + lam_init)


def _attn_ctx_kernel(q_ref, k_ref, v_ref, z_ref, lq_ref, lk_ref, sg_ref, o_ref, vt_ref, *, lam_init):
    seq = k_ref.shape[0]
    vt_ref[0:D_AT, :] = v_ref[...].astype(F32).T.astype(BF16)
    vt_ref[D_AT:D_AT + ONES_ROWS, :] = jnp.ones((ONES_ROWS, seq), BF16)
    lam = _diff_lambda(lq_ref, lk_ref, lam_init)
    ones = vt_ref[D_AT:D_AT + ONES_ROWS, :]

    for h in range(AT_HEADS):
        hv = slice(h * AT_DV, (h + 1) * AT_DV)

        def unnormalised(c):
            cols = slice(h * AT_DV + c * AT_DH, h * AT_DV + (c + 1) * AT_DH)
            s = _nt_dot(k_ref[:, cols], q_ref[:, cols])
            e = jnp.exp2(s - jnp.max(s, axis=0, keepdims=True)).astype(BF16)
            o = jnp.dot(vt_ref[hv, :], e, preferred_element_type=F32)
            den = jnp.dot(ones, e, preferred_element_type=F32)
            return o, den[0:1, :]

        o1, den1 = unnormalised(0)
        o2, den2 = unnormalised(1)
        o = o1 * (1.0 / den1) - o2 * (lam / den2)
        o = o * lax.rsqrt(jnp.mean(o * o, axis=0, keepdims=True) + NORM_EPS)
        o = o.T * sg_ref[...] * (1.0 - lam_init)
        o_ref[:, hv] = (o * z_ref[:, hv].astype(F32)).astype(o_ref.dtype)


def _attn_latent_kernel(q_ref, k_ref, v_ref, z_ref, lq_ref, lk_ref, sg_ref, ck_ref, cv_ref, o_ref,
                        *, lam_init):
    lam = _diff_lambda(lq_ref, lk_ref, lam_init)

    def scores(h, c):
        cols = slice(h * AT_DV + c * AT_DH, h * AT_DV + (c + 1) * AT_DH)
        q = q_ref[:, cols]
        return _nt_dot(q, k_ref[:, cols]), _nt_dot(q, ck_ref[:, cols].astype(BF16))

    def softmax_parts(s, sc):
        m = jnp.maximum(jnp.max(s, axis=-1, keepdims=True), jnp.max(sc, axis=-1, keepdims=True))
        e = jnp.exp2(s - m)
        ec = jnp.exp2(sc - m)
        return e, ec, jnp.sum(e, axis=-1, keepdims=True) + jnp.sum(ec, axis=-1, keepdims=True)

    def finish(h, part1, part2):
        e1, ec1, den1 = part1
        e2, ec2, den2 = part2
        ratio = lam * den1 / den2
        hv = slice(h * AT_DV, (h + 1) * AT_DV)
        o = (jnp.dot((e1 - e2 * ratio).astype(BF16), v_ref[:, hv], preferred_element_type=F32)
             + jnp.dot((ec1 - ec2 * ratio).astype(BF16), cv_ref[:, hv].astype(BF16),
                       preferred_element_type=F32)) * (1.0 / den1)
        o = o * lax.rsqrt(jnp.mean(o * o, axis=-1, keepdims=True) + NORM_EPS) * sg_ref[...]
        o_ref[:, hv] = (o * (1.0 - lam_init) * z_ref[:, hv].astype(F32)).astype(o_ref.dtype)

    units = [(h, c) for h in range(AT_HEADS) for c in range(2)]
    ahead = scores(*units[0])
    parts = [None, None]
    for i, (h, c) in enumerate(units):
        current = ahead
        if i + 1 < len(units):
            ahead = scores(*units[i + 1])
        parts[c] = softmax_parts(*current)
        if c == 1:
            finish(h, parts[0], parts[1])


def _attention(act, batch, seq, lam_q, lam_k, subln_g, lam_init, ctx, layer):
    t = batch * seq
    tq = 256
    nq = seq // tq
    small = lambda a: pl.BlockSpec(a.shape, lambda b, i: (0,) * a.ndim)
    in_specs = [pl.BlockSpec((tq, CB), lambda b, i: (b * nq + i, COL_Q)),
                pl.BlockSpec((seq, CB), lambda b, i: (b, COL_K)),
                pl.BlockSpec((seq, CB), lambda b, i: (b, COL_V)),
                pl.BlockSpec((tq, CB), lambda b, i: (b * nq + i, COL_AT_Z)),
                small(lam_q), small(lam_k), small(subln_g)]
    args = [act, act, act, act, lam_q, lam_k, subln_g]
    if ctx is None:
        assert nq == 1
        body = functools.partial(_attn_ctx_kernel, lam_init=lam_init)
        scratch = [pltpu.VMEM((D_AT + ONES_ROWS, seq), BF16)]
    else:
        ck, cv = ctx
        cspec = pl.BlockSpec((None, None, ck.shape[2], D_AT), lambda b, i: (b, layer, 0, 0))
        in_specs += [cspec, cspec]
        args += [ck, cv]
        body = functools.partial(_attn_latent_kernel, lam_init=lam_init)
        scratch = []
    return pl.pallas_call(
        body,
        grid=(batch, nq),
        in_specs=in_specs,
        out_specs=pl.BlockSpec((tq, D_AT), lambda b, i: (b * nq + i, 0)),
        out_shape=jax.ShapeDtypeStruct((t, D_AT), BF16),
        scratch_shapes=scratch,
        compiler_params=_params("parallel", "parallel"),
        name=f"diff_attn_{seq}",
    )(*args)


def _pool_kernel(u_ref, z_ref, w_ref, sc_ref, o_ref, *, seq):
    padded = seq + 2 * POOL_PAD
    t = lax.broadcasted_iota(jnp.int32, (seq, 1), 0)
    zeros = jnp.zeros((POOL_PAD, POOL_GC), F32)
    for g, win in enumerate(POOL_WINDOWS):
        cols = slice(g * POOL_GC, (g + 1) * POOL_GC)
        u = u_ref[:, cols].astype(F32)
        s = jnp.concatenate([zeros, u, zeros], axis=0)
        s = s + pltpu.roll(s, 1, 0)
        half = 1
        while 2 * half < win:
            s = pltpu.roll(s, padded - half, 0) + pltpu.roll(s, half, 0)
            half *= 2
        lo = jnp.maximum(t - win // 2, 0)
        hi = jnp.minimum(t + win // 2, seq)
        m = s[POOL_PAD:POOL_PAD + seq] / (hi - lo).astype(F32) - u
        y = jnp.dot(m.astype(BF16), w_ref[g].astype(BF16), preferred_element_type=F32)
        o_ref[:, cols] = (y * sc_ref[:, cols] * z_ref[:, cols].astype(F32)).astype(o_ref.dtype)


def _pool_branch(proj, batch, seq, pool_w, pool_scale):
    t = batch * seq
    return pl.pallas_call(
        functools.partial(_pool_kernel, seq=seq),
        grid=(batch,),
        in_specs=[pl.BlockSpec((seq, CB), lambda b: (b, COL_PL_U)),
                  pl.BlockSpec((seq, CB), lambda b: (b, COL_PL_Z)),
                  pl.BlockSpec(pool_w.shape, lambda b: (0, 0, 0)),
                  pl.BlockSpec((1, D_PL), lambda b: (0, 0))],
        out_specs=pl.BlockSpec((seq, D_PL), lambda b: (b, 0)),
        out_shape=jax.ShapeDtypeStruct((t, D_PL), BF16),
        compiler_params=_params("parallel"),
        name=f"pool_{seq}",
    )(proj, proj, pool_w, pool_scale)


def _merge_kernel(x_ref, mod_ref, yh_ref, ya_ref, yp_ref, g0_ref, g1_ref, g2_ref,
                  wh_ref, wa_ref, wp_ref, wo_ref, o_ref):
    merged = (g0_ref[...].astype(F32) * jnp.dot(yh_ref[...], wh_ref[...], preferred_element_type=F32)
              + g1_ref[...].astype(F32) * jnp.dot(ya_ref[...], wa_ref[...], preferred_element_type=F32)
              + g2_ref[...].astype(F32) * jnp.dot(yp_ref[...], wp_ref[...], preferred_element_type=F32))
    out = jnp.dot(merged.astype(BF16), wo_ref[...], preferred_element_type=F32)
    o_ref[...] = x_ref[...] + mod_ref[:, 2 * D_MODEL:3 * D_MODEL] * out


def _merge(x, mod, mod_row0, rows_per_seq, act, y_hy, y_at, y_pl, w_hy_o, w_at_o, w_pl_o, w_out, tm=512):
    t = x.shape[0]
    blocks_per_seq = rows_per_seq // tm
    d = D_MODEL
    branch = pl.BlockSpec((tm, CB), lambda i: (i, 0))
    gate = lambda g: pl.BlockSpec((tm, d), lambda i: (i, COL_MG // 2 + g))
    return pl.pallas_call(
        _merge_kernel,
        grid=(t // tm,),
        in_specs=[pl.BlockSpec((tm, d), lambda i: (i, 0)),
                  pl.BlockSpec((None, 1, 3 * d), lambda i: (mod_row0 + i // blocks_per_seq, 0, 0)),
                  branch, branch, branch, gate(0), gate(1), gate(2),
                  _const_spec((CB, d)), _const_spec((CB, d)), _const_spec((CB, d)), _const_spec((d, d))],
        out_specs=pl.BlockSpec((tm, d), lambda i: (i, 0)),
        out_shape=jax.ShapeDtypeStruct((t, d), F32),
        compiler_params=_params("parallel"),
        name="merge",
    )(x, mod, y_hy, y_at, y_pl, act, act, act, w_hy_o, w_at_o, w_pl_o, w_out)


def _trunk_layer(x, batch, seq, mod, mod_row0, p, filters, dft, rope, ctx, layer, lam_init):
    mod_rows = seq if mod_row0 else batch * seq
    act, kv = _in_proj(x, mod, mod_row0, mod_rows, p["norm_g"], p["w_in"], p["gq"], p["gk"], rope, seq,
                       keep_kv=rope is None)
    y_hy = _hyena_branch(act, batch, seq, p["hy_conv_w"], p["hy_conv_b"], p["hy_bias"], filters, dft)
    y_at = _attention(act, batch, seq, p["lam_q"], p["lam_k"], p["subln_g"], lam_init, ctx, layer)
    y_pl = _pool_branch(act, batch, seq, p["pool_w"], p["pool_scale"])
    out = _merge(x, mod, mod_row0, mod_rows, act, y_hy, y_at, y_pl,
                 p["w_hy_o"], p["w_at_o"], p["w_pl_o"], p["w_out"])
    return out, kv


def kernel(x_prompt, x_sample, cache_k, cache_v, c, c_ctx, norm_g, w_ada, b_ada, w_in, hy_conv_w, hy_conv_b,
           hy_f_w1, hy_f_b1, hy_f_w2, hy_f_b2, hy_f_w3, hy_freq, hy_bias, q_norm_g, k_norm_g, lam_q, lam_k,
           subln_g, pool_w, pool_scale, w_hy_o, w_at_o, w_pl_o, w_out):
    batch, seq, d = x_prompt.shape
    dec_batch, dec_seq, _ = x_sample.shape
    past = cache_k.shape[2]
    assert d == D_MODEL and dec_batch + 1 <= COND_ROWS and dec_seq % GRID_W == 0

    cond = jnp.concatenate([c_ctx[None, :], c, jnp.zeros((COND_ROWS - 1 - dec_batch, d), F32)], axis=0)
    mod = _modulation(cond, w_ada, b_ada).reshape(DEPTH, COND_ROWS, 1, 3 * d)
    dft = {s: _dft_tables(s) for s in (seq, dec_seq)}
    rope = _rope_tables(dec_seq)
    ctx = (cache_k.reshape(dec_batch, DEPTH, past, D_AT), cache_v.reshape(dec_batch, DEPTH, past, D_AT))

    yp = x_prompt.reshape(batch * seq, d)
    ys = x_sample.reshape(dec_batch * dec_seq, d)
    new_ks, new_vs = [], []
    for l in range(DEPTH):
        p = dict(norm_g=norm_g[l][None, :], w_in=w_in[l].astype(BF16),
                 hy_conv_w=hy_conv_w[l], hy_conv_b=hy_conv_b[l], hy_bias=hy_bias[l],
                 gq=jnp.tile(q_norm_g[l], D_AT // AT_DH)[None, :], gk=jnp.tile(k_norm_g[l], D_AT // AT_DH)[None, :],
                 lam_q=lam_q[l], lam_k=lam_k[l], subln_g=subln_g[l][None, :],
                 pool_w=pool_w[l], pool_scale=pool_scale[l][None, :],
                 w_hy_o=w_hy_o[l].astype(BF16), w_at_o=w_at_o[l].astype(BF16),
                 w_pl_o=w_pl_o[l].astype(BF16), w_out=w_out[l].astype(BF16))
        filters = {s: _hyena_filters(s, hy_f_w1[l], hy_f_b1[l], hy_f_w2[l], hy_f_b2[l], hy_f_w3[l], hy_freq[l],
                                     dft[s]) for s in (seq, dec_seq)}
        lam_init = 0.8 - 0.6 * math.exp(-0.3 * l)
        yp, kv = _trunk_layer(yp, batch, seq, mod[l], 0, p, filters[seq], dft[seq], None, None, l, lam_init)
        new_ks.append(kv[0].reshape(batch, seq, AT_HEADS, 2, AT_DH))
        new_vs.append(kv[1].reshape(batch, seq, AT_HEADS, AT_DV))
        ys, _ = _trunk_layer(ys, dec_batch, dec_seq, mod[l], 1, p, filters[dec_seq], dft[dec_seq], rope, ctx, l,
                             lam_init)
    return (yp.reshape(batch, seq, d), ys.reshape(dec_batch, dec_seq, d),
            jnp.stack(new_ks, axis=1), jnp.stack(new_vs, axis=1))
```

```python
import functools
import math

import jax
import jax.numpy as jnp
from jax import lax
from jax.experimental import pallas as pl
from jax.experimental.pallas import tpu as pltpu

F32 = jnp.float32
BF16 = jnp.bfloat16
HIGHEST = lax.Precision.HIGHEST

D_MODEL = 1024
DEPTH = 2
GRID_W = 64
GRID_W_LOG2 = GRID_W.bit_length() - 1
assert GRID_W == 1 << GRID_W_LOG2
D_HY = 512
HY_ORDER = 2
HY_EMB = 33
HY_BANDS = (HY_EMB - 1) // 2
HY_FFN = 64
HY_FAST_DECAY = 0.3
HY_SLOW_DECAY = 1.5
HY_TARGET = 1e-2
AT_HEADS = 4
AT_DH = 64
AT_DV = 2 * AT_DH
D_AT = AT_HEADS * AT_DV
ROPE_BASE = 10000.0
Q_SCALE = AT_DH ** -0.5 * math.log2(math.e)
D_PL = 512
POOL_WINDOWS = (2, 4, 8, 16)
POOL_GC = D_PL // 4
POOL_PAD = 16
NORM_EPS = 1e-6
D_IN = 8192
COND_ROWS = 8

CB = 512
COL_HY_X1, COL_HY_X2, COL_HY_V, COL_HY_Z = 0, 1, 2, 3
COL_Q, COL_K, COL_V, COL_AT_Z, COL_PL_U, COL_PL_Z = 4, 5, 6, 7, 8, 9
COL_MG = 10

VMEM_LIMIT_BYTES = 56 * 1024 * 1024
LANES = 128


def _params(*sem):
    return pltpu.CompilerParams(dimension_semantics=sem, vmem_limit_bytes=VMEM_LIMIT_BYTES)


def _const_spec(shape):
    return pl.BlockSpec(shape, lambda *_: (0,) * len(shape), pipeline_mode=pl.Buffered(1))


def _silu(x):
    return x * jax.nn.sigmoid(x)


def _bdot(a, b):
    return jnp.dot(a.astype(BF16), b.astype(BF16), preferred_element_type=F32)


def _fdot(a, b):
    return jnp.dot(a, b, precision=HIGHEST, preferred_element_type=F32)


def _mod_kernel(cond_ref, w_ref, b_ref, o_ref):
    o_ref[...] = _fdot(_silu(cond_ref[...]), w_ref[...]) + b_ref[...]


def _modulation(cond, w_ada, b_ada):
    d = D_MODEL
    return pl.pallas_call(
        _mod_kernel,
        grid=(DEPTH, 3),
        in_specs=[
            pl.BlockSpec((COND_ROWS, d), lambda l, n: (0, 0)),
            pl.BlockSpec((None, d, d), lambda l, n: (l, 0, n)),
            pl.BlockSpec((None, 1, d), lambda l, n: (l, 0, n)),
        ],
        out_specs=pl.BlockSpec((None, COND_ROWS, d), lambda l, n: (l, 0, n)),
        out_shape=jax.ShapeDtypeStruct((DEPTH, COND_ROWS, 3 * d), F32),
        compiler_params=_params("parallel", "parallel"),
        name="modulation",
    )(cond, w_ada, b_ada.reshape(DEPTH, 1, 3 * d))


def _group_mean_matrix():
    r = lax.broadcasted_iota(jnp.int32, (D_AT, D_AT), 0) // AT_DH
    c = lax.broadcasted_iota(jnp.int32, (D_AT, D_AT), 1) // AT_DH
    return jnp.where(r == c, 1.0 / AT_DH, 0.0).astype(BF16)


def _group_rms_norm(x, g, mean_mat):
    ms = jnp.dot((x * x).astype(BF16), mean_mat, preferred_element_type=F32)
    return x * lax.rsqrt(ms + NORM_EPS) * g


def _rotate(x, cos, sin_signed):
    lane = lax.broadcasted_iota(jnp.int32, (1, D_AT), 1)
    partner = jnp.where((lane & 1) == 0, pltpu.roll(x, D_AT - 1, 1), pltpu.roll(x, 1, 1))
    return x * cos + partner * sin_signed


def _in_kernel(*refs, rope, keep_kv):
    x_ref, mod_ref, g_ref, w_ref, gq_ref, gk_ref = refs[:6]
    refs = refs[6:]
    if rope:
        cos_ref, sin_ref = refs[:2]
        refs = refs[2:]
    o_ref = refs[0]
    h_ref = refs[-1]

    x = x_ref[...]
    y = x * lax.rsqrt(jnp.mean(x * x, axis=-1, keepdims=True) + NORM_EPS) * g_ref[...]
    h_ref[...] = (y * (1.0 + mod_ref[:, D_MODEL:2 * D_MODEL]) + mod_ref[:, 0:D_MODEL]).astype(BF16)

    def chunk(c):
        return jnp.dot(h_ref[...], w_ref[:, c * CB:(c + 1) * CB], preferred_element_type=F32)

    def put(c, val):
        o_ref[:, c * CB:(c + 1) * CB] = val.astype(o_ref.dtype)

    for c in (COL_HY_X1, COL_HY_X2, COL_HY_V, COL_PL_U):
        put(c, chunk(c))
    for c in (COL_HY_Z, COL_AT_Z, COL_PL_Z):
        put(c, _silu(chunk(c)))
    for c in range(COL_MG, D_IN // CB):
        put(c, jax.nn.sigmoid(chunk(c)))

    mean_mat = _group_mean_matrix()
    q = _group_rms_norm(chunk(COL_Q), gq_ref[...], mean_mat)
    k = _group_rms_norm(chunk(COL_K), gk_ref[...], mean_mat)
    v = chunk(COL_V)
    if keep_kv:
        refs[1][...] = k
        refs[2][...] = v
    if rope:
        q = _rotate(q, cos_ref[...], sin_ref[...])
        k = _rotate(k, cos_ref[...], sin_ref[...])
    put(COL_Q, q * Q_SCALE)
    put(COL_K, k)
    put(COL_V, v)


def _in_proj(x, mod, mod_row0, rows_per_seq, norm_g, w_in_bf16, layer, gq, gk, rope, seq, keep_kv, tm=512):
    t = x.shape[0]
    blocks_per_seq = rows_per_seq // tm
    vec = pl.BlockSpec((1, D_AT), lambda i: (0, 0))
    in_specs = [pl.BlockSpec((tm, D_MODEL), lambda i: (i, 0)),
                pl.BlockSpec((None, 1, 3 * D_MODEL), lambda i: (mod_row0 + i // blocks_per_seq, 0, 0)),
                pl.BlockSpec((1, D_MODEL), lambda i: (0, 0)),
                pl.BlockSpec((None, D_MODEL, D_IN), lambda i: (layer, 0, 0), pipeline_mode=pl.Buffered(1)),
                vec, vec]
    args = [x, mod, norm_g, w_in_bf16, gq, gk]
    if rope is not None:
        per_seq = seq // tm
        tab = pl.BlockSpec((tm, D_AT), lambda i: (i % per_seq, 0))
        in_specs += [tab, tab]
        args += list(rope)
    out_specs = [pl.BlockSpec((tm, D_IN), lambda i: (i, 0))]
    out_shape = [jax.ShapeDtypeStruct((t, D_IN), BF16)]
    if keep_kv:
        out_specs += [pl.BlockSpec((tm, D_AT), lambda i: (i, 0))] * 2
        out_shape += [jax.ShapeDtypeStruct((t, D_AT), F32)] * 2
    out = pl.pallas_call(
        functools.partial(_in_kernel, rope=rope is not None, keep_kv=keep_kv),
        grid=(t // tm,),
        in_specs=in_specs,
        out_specs=out_specs,
        out_shape=out_shape,
        scratch_shapes=[pltpu.VMEM((tm, D_MODEL), BF16)],
        compiler_params=_params("parallel"),
        name="in_proj_latent" if rope is not None else "in_proj_ctx",
    )(*args)
    return (out[0], (out[1], out[2])) if keep_kv else (out[0], None)


TABLE_ROWS = 32


def _dft_kernel(ce_ref, se_ref, co_ref, so_ref, cot_ref, sot_ref, bc_ref, bs_ref, oc_ref, os_ref, *, seq):
    half = seq // 2
    col = lax.broadcasted_iota(jnp.int32, (1, half), 1)
    base = lax.broadcasted_iota(jnp.int32, (half // TABLE_ROWS, 1), 0) * TABLE_ROWS
    off = lax.broadcasted_iota(jnp.int32, (TABLE_ROWS, 1), 0)

    def trig(c_ref, s_ref, slot, product):
        ang = (product & (2 * seq - 1)).astype(F32) * (math.pi / seq)
        c_ref[slot] = jnp.cos(ang)
        s_ref[slot] = jnp.sin(ang)

    trig(bc_ref, bs_ref, 0, 2 * base * col)
    trig(bc_ref, bs_ref, 1, base * (2 * col + 1))
    trig(bc_ref, bs_ref, 2, (2 * base + 1) * col)
    trig(oc_ref, os_ref, 0, 2 * off * col)
    trig(oc_ref, os_ref, 1, off * (2 * col + 1))
    outs = ((ce_ref, se_ref, 0, 0), (co_ref, so_ref, 1, 1), (cot_ref, sot_ref, 2, 0))

    def block(i, carry):
        rows = pl.ds(pl.multiple_of(i * TABLE_ROWS, TABLE_ROWS), TABLE_ROWS)
        for c_ref, s_ref, b, o in outs:
            bc = bc_ref[b, pl.ds(i, 1), :]
            bs = bs_ref[b, pl.ds(i, 1), :]
            oc = oc_ref[o]
            osn = os_ref[o]
            c_ref[rows, :] = (bc * oc - bs * osn).astype(BF16)
            s_ref[rows, :] = (bs * oc + bc * osn).astype(BF16)
        return carry

    lax.fori_loop(0, half // TABLE_ROWS, block, 0)


def _dft_tables(seq):
    half = seq // 2
    shape = jax.ShapeDtypeStruct((half, half), BF16)
    return pl.pallas_call(
        functools.partial(_dft_kernel, seq=seq),
        out_shape=[shape] * 6,
        scratch_shapes=[pltpu.VMEM((3, half // TABLE_ROWS, half), F32), pltpu.VMEM((3, half // TABLE_ROWS, half), F32),
                        pltpu.VMEM((2, TABLE_ROWS, half), F32), pltpu.VMEM((2, TABLE_ROWS, half), F32)],
        compiler_params=pltpu.CompilerParams(vmem_limit_bytes=VMEM_LIMIT_BYTES),
        name=f"dft_tables_{seq}",
    )()


FILTER_CB = 256


def _filter_kernel(w1t_ref, w1c_ref, w1s_ref, b1_ref, w2_ref, b2_ref, fq_ref, w3f_ref, w3b_ref,
                   ce_ref, se_ref, co_ref, so_ref, alo_ref, blo_ref, ahi_ref, bhi_ref, mid_ref, h_ref, *, seq):
    half = seq // 2
    row = lax.broadcasted_iota(jnp.int32, (half, 1), 0)

    @pl.when(pl.program_id(0) == 0)
    def _():
        band = lax.broadcasted_iota(jnp.int32, (HY_BANDS, 1), 0).astype(F32)
        f = 1e-4 + band * ((HY_BANDS - 1 - 1e-4) / (HY_BANDS - 1))
        for p in range(2):
            pos = (2 * lax.broadcasted_iota(jnp.int32, (1, half), 1) + p).astype(F32)
            fw = f * (pos * (2.0 * math.pi / seq))
            pre = (w1t_ref[...] * (pos * (1.0 / (seq - 1))) + _fdot(w1c_ref[...], jnp.cos(fw))
                   - _fdot(w1s_ref[...], jnp.sin(fw)) + b1_ref[...])
            h = jnp.sin(fq_ref[:, 0:1] * pre)
            h = jnp.sin(fq_ref[:, 1:2] * (_fdot(w2_ref[...], h) + b2_ref[...]))
            h_ref[p] = h.T

    chan = lax.broadcasted_iota(jnp.int32, (1, FILTER_CB), 1) + (pl.program_id(0) % (D_HY // FILTER_CB)) * FILTER_CB
    max_decay = math.log(HY_TARGET) / HY_FAST_DECAY
    min_decay = math.log(HY_TARGET) / HY_SLOW_DECAY
    deltas = jnp.abs(min_decay + chan.astype(F32) * ((max_decay - min_decay) / (D_HY - 1)))

    def taps(p):
        decay = jnp.exp((2 * row + p).astype(F32) * (-1.0 / (seq - 1)) * deltas)
        hf = _fdot(h_ref[p], w3f_ref[...]) * decay
        hb = _fdot(h_ref[p], w3b_ref[...]) * decay
        if p == 0:
            hb = jnp.where(row == 0, 0.0, hb)
        return hf, hb

    hf0, hb0 = taps(0)
    hf1, hb1 = taps(1)
    inv = 1.0 / (jnp.sum(jnp.abs(hf0) + jnp.abs(hf1) + jnp.abs(hb0) + jnp.abs(hb1), axis=0, keepdims=True) + NORM_EPS)
    hs0, hs1 = (hf0 + hb0) * inv, (hf1 + hb1) * inv
    hd0, hd1 = (hf0 - hb0) * inv, (hf1 - hb1) * inv
    c_even = _bdot(ce_ref[...], hs0)
    c_odd = _bdot(co_ref[...], hs1)
    s_even = _bdot(se_ref[...], hd0)
    s_odd = _bdot(so_ref[...], hd1)
    wk = jnp.where(row == 0, 0.5 / seq, 1.0 / seq)
    alo_ref[...] = wk * (c_even + c_odd)
    ahi_ref[...] = wk * (c_even - c_odd)
    blo_ref[...] = -wk * (s_even + s_odd)
    bhi_ref[...] = wk * (s_even - s_odd)
    alt = (1 - 2 * (row & 1)).astype(F32)
    mid_ref[0:1, :] = jnp.sum(alt * hs0, axis=0, keepdims=True) * (1.0 / seq)
    mid_ref[1:2, :] = jnp.sum(alt * hd1, axis=0, keepdims=True) * (-1.0 / seq)


def _hyena_filters(seq, w1, b1, w2, b2, w3, freq, tables):
    nc = HY_ORDER * D_HY
    nblk = nc // FILTER_CB
    half = seq // 2
    small = lambda a: pl.BlockSpec(a.shape, lambda j: (0,) * a.ndim)
    w1t, w1c, w1s = w1[0:1].T, w1[1:1 + HY_BANDS].T, w1[1 + HY_BANDS:].T
    b1 = b1.reshape(HY_FFN, 1)
    b2 = b2.reshape(HY_FFN, 1)
    w2 = w2.T
    freq = freq.T
    out_blk = pl.BlockSpec((half, FILTER_CB), lambda j: (0, j))
    spec_shape = jax.ShapeDtypeStruct((half, nc), F32)
    return pl.pallas_call(
        functools.partial(_filter_kernel, seq=seq),
        grid=(nblk,),
        in_specs=[small(w1t), small(w1c), small(w1s), small(b1), small(w2), small(b2), small(freq),
                  pl.BlockSpec((HY_FFN, FILTER_CB), lambda j: (0, j)),
                  pl.BlockSpec((HY_FFN, FILTER_CB), lambda j: (0, nblk + j))] + [_const_spec((half, half))] * 4,
        out_specs=[out_blk] * 4 + [pl.BlockSpec((2, FILTER_CB), lambda j: (0, j))],
        out_shape=[spec_shape] * 4 + [jax.ShapeDtypeStruct((2, nc), F32)],
        scratch_shapes=[pltpu.VMEM((2, half, HY_FFN), F32)],
        compiler_params=_params("arbitrary"),
        name=f"hyena_filters_{seq}",
    )(w1t, w1c, w1s, b1, w2, b2, freq, w3, w3, *tables[:4])


def _hyena_kernel(*refs, seq, rb):
    x1_ref, x2_ref, v_ref, z_ref = refs[0:4]
    conv_w, conv_c = refs[4:7], refs[7:10]
    spectra = (refs[10:16], refs[16:22])
    ce_ref, se_ref, co_ref, so_ref, cot_ref, sot_ref = refs[22:28]
    o_ref, u_ref, g_ref, ub_ref, pq_ref = refs[28:33]
    half = seq // 2
    slabs = u_ref.shape[0]
    pos = lax.broadcasted_iota(jnp.int32, (seq, 1), 0)
    row = lax.broadcasted_iota(jnp.int32, (half, 1), 0)
    alt = (1 - 2 * (row & 1)).astype(F32)

    def short_conv(x_ref, w_ref, c_ref):
        x = x_ref[...].astype(F32)
        prev = jnp.where(pos == 0, 0.0, pltpu.roll(x, 1, 0))
        nxt = jnp.where(pos == seq - 1, 0.0, pltpu.roll(x, seq - 1, 0))
        return prev * w_ref[0:1, :] + x * w_ref[1:2, :] + nxt * w_ref[2:3, :] + c_ref[...]

    def put_u(rows, val):
        for s in range(slabs):
            u_ref[s, rows, :] = val[:, s * LANES:(s + 1) * LANES]

    def get_u(rows):
        return jnp.concatenate([u_ref[s, rows, :] for s in range(slabs)], axis=1)

    def long_conv(alo_ref, blo_ref, ahi_ref, bhi_ref, mid_ref, skip_ref, finish):
        for p in range(2):
            ub_ref[p] = get_u(pl.ds(p, half, stride=2)).astype(BF16)
        c_mid = jnp.sum(get_u(pl.ds(0, half, stride=2)) * alt, axis=0, keepdims=True)
        s_mid = jnp.sum(get_u(pl.ds(1, half, stride=2)) * alt, axis=0, keepdims=True)
        p_mid = c_mid * mid_ref[0:1, :] + s_mid * mid_ref[1:2, :]
        q_mid = s_mid * mid_ref[0:1, :] - c_mid * mid_ref[1:2, :]
        for r in range(0, half, rb):
            rows = pl.ds(r, rb)
            c_even = jnp.dot(ce_ref[rows, :], ub_ref[0], preferred_element_type=F32)
            c_odd = jnp.dot(co_ref[rows, :], ub_ref[1], preferred_element_type=F32)
            s_even = jnp.dot(se_ref[rows, :], ub_ref[0], preferred_element_type=F32)
            s_odd = jnp.dot(so_ref[rows, :], ub_ref[1], preferred_element_type=F32)
            c_lo, c_hi = c_even + c_odd, c_even - c_odd
            s_lo, s_hi = s_even + s_odd, s_odd - s_even
            alo, blo, ahi, bhi = alo_ref[rows, :], blo_ref[rows, :], ahi_ref[rows, :], bhi_ref[rows, :]
            p_lo, q_lo = c_lo * alo + s_lo * blo, s_lo * alo - c_lo * blo
            p_hi, q_hi = c_hi * ahi + s_hi * bhi, s_hi * ahi - c_hi * bhi
            pq_ref[0, rows, :] = (p_lo + p_hi).astype(BF16)
            pq_ref[1, rows, :] = (q_lo - q_hi).astype(BF16)
            pq_ref[2, rows, :] = (p_lo - p_hi).astype(BF16)
            pq_ref[3, rows, :] = (q_lo + q_hi).astype(BF16)
        for r in range(0, half, rb):
            rows = pl.ds(r, rb)
            even, odd = pl.ds(2 * r, rb, stride=2), pl.ds(2 * r + 1, rb, stride=2)
            ye = (jnp.dot(ce_ref[rows, :], pq_ref[0], preferred_element_type=F32)
                  + jnp.dot(se_ref[rows, :], pq_ref[1], preferred_element_type=F32)
                  + alt[r:r + rb] * p_mid + get_u(even) * skip_ref[...])
            yo = (jnp.dot(cot_ref[rows, :], pq_ref[2], preferred_element_type=F32)
                  + jnp.dot(sot_ref[rows, :], pq_ref[3], preferred_element_type=F32)
                  + alt[r:r + rb] * q_mid + get_u(odd) * skip_ref[...])
            put_u(even, ye)
            put_u(odd, yo)
            both = pl.ds(2 * r, 2 * rb)
            finish(both, g_ref[both, :] * get_u(both))

    def keep(rows, y):
        put_u(rows, y)

    def emit(rows, y):
        o_ref[rows, :] = (y * z_ref[rows, :].astype(F32)).astype(o_ref.dtype)

    put_u(pl.ds(0, seq), short_conv(v_ref, conv_w[2], conv_c[2]))
    g_ref[...] = short_conv(x1_ref, conv_w[0], conv_c[0])
    long_conv(*spectra[0], keep)
    g_ref[...] = short_conv(x2_ref, conv_w[1], conv_c[1])
    long_conv(*spectra[1], emit)


def _hyena_branch(act, batch, seq, conv_w, conv_b, hy_bias, filters, tables):
    cb = 256 if seq > 512 else D_HY
    half = seq // 2
    rb = min(half, 512)
    nj = D_HY // cb
    t = batch * seq
    conv_b = conv_b.reshape(1, 3 * D_HY)
    skip = hy_bias.reshape(1, HY_ORDER * D_HY)
    alo, blo, ahi, bhi, mid = filters

    def act_spec(col512):
        off = col512 * (CB // cb)
        return pl.BlockSpec((seq, cb), lambda j, b: (b, off + j))

    def chan_spec(rows, part):
        off = part * nj
        return pl.BlockSpec((rows, cb), lambda j, b: (0, off + j), pipeline_mode=pl.Buffered(1))

    in_specs = [act_spec(COL_HY_X1), act_spec(COL_HY_X2), act_spec(COL_HY_V), act_spec(COL_HY_Z)]
    in_specs += [chan_spec(3, part) for part in range(3)] + [chan_spec(1, part) for part in range(3)]
    args = [act] * 4 + [conv_w] * 3 + [conv_b] * 3
    for order in range(HY_ORDER):
        in_specs += [chan_spec(half, order)] * 4 + [chan_spec(2, order), chan_spec(1, order)]
        args += [alo, blo, ahi, bhi, mid, skip]
    in_specs += [_const_spec((half, half))] * 6
    args += list(tables)
    return pl.pallas_call(
        functools.partial(_hyena_kernel, seq=seq, rb=rb),
        grid=(nj, batch),
        in_specs=in_specs,
        out_specs=pl.BlockSpec((seq, cb), lambda j, b: (b, j)),
        out_shape=jax.ShapeDtypeStruct((t, D_HY), BF16),
        scratch_shapes=[pltpu.VMEM((cb // LANES, seq, LANES), F32), pltpu.VMEM((seq, cb), F32),
                        pltpu.VMEM((2, half, cb), BF16), pltpu.VMEM((4, half, cb), BF16)],
        compiler_params=_params("parallel", "parallel"),
        name=f"hyena_{seq}",
    )(*args)


def _rope_kernel(cos_ref, sin_ref, c_ref, s_ref, *, seq):
    lane = lax.broadcasted_iota(jnp.int32, (1, D_AT), 1)
    dim = lane & (AT_DH - 1)
    pair = ((dim & (AT_DH // 2 - 1)) >> 1).astype(F32)
    inv = jnp.exp(pair * (-(2.0 / (AT_DH // 2)) * math.log(ROPE_BASE)))
    ang = lax.broadcasted_iota(jnp.int32, (GRID_W, 1), 0).astype(F32) * inv
    c_ref[...] = jnp.cos(ang)
    s_ref[...] = jnp.where((lane & 1) == 0, -jnp.sin(ang), jnp.sin(ang))
    by_row = dim < AT_DH // 2

    def grid_row(r, carry):
        rows = pl.ds(pl.multiple_of(r * GRID_W, GRID_W), GRID_W)
        cos_ref[rows, :] = jnp.where(by_row, c_ref[pl.ds(r, 1), :], c_ref[...])
        sin_ref[rows, :] = jnp.where(by_row, s_ref[pl.ds(r, 1), :], s_ref[...])
        return carry

    lax.fori_loop(0, seq // GRID_W, grid_row, 0)


def _rope_tables(seq):
    assert seq // GRID_W <= GRID_W
    shape = jax.ShapeDtypeStruct((seq, D_AT), F32)
    return pl.pallas_call(
        functools.partial(_rope_kernel, seq=seq),
        out_shape=[shape, shape],
        scratch_shapes=[pltpu.VMEM((GRID_W, D_AT), F32), pltpu.VMEM((GRID_W, D_AT), F32)],
        compiler_params=pltpu.CompilerParams(vmem_limit_bytes=VMEM_LIMIT_BYTES),
        name="rope_tables",
    )()


def _nt_dot(a, b):
    return lax.dot_general(a, b, (((1,), (1,)), ((), ())), preferred_element_type=F32)


ONES_ROWS = 8


def _diff_lambda(lq_ref, lk_ref, lam_init):
    lqk = lq_ref[...] * lk_ref[...]
    return (jnp.exp(jnp.sum(lqk[0:1, :], axis=-1, keepdims=True))
            - jnp.exp(jnp.sum(lqk[1:2, :], axis=-1, keepdims=True)) + lam_init)


def _attn_ctx_kernel(q_ref, k_ref, v_ref, z_ref, lq_ref, lk_ref, sg_ref, o_ref, vt_ref, *, lam_init):
    seq = k_ref.shape[0]
    vt_ref[0:D_AT, :] = v_ref[...].astype(F32).T.astype(BF16)
    vt_ref[D_AT:D_AT + ONES_ROWS, :] = jnp.ones((ONES_ROWS, seq), BF16)
    lam = _diff_lambda(lq_ref, lk_ref, lam_init)
    ones = vt_ref[D_AT:D_AT + ONES_ROWS, :]

    for h in range(AT_HEADS):
        hv = slice(h * AT_DV, (h + 1) * AT_DV)

        def unnormalised(c):
            cols = slice(h * AT_DV + c * AT_DH, h * AT_DV + (c + 1) * AT_DH)
            s = _nt_dot(k_ref[:, cols], q_ref[:, cols])
            e = jnp.exp2(s - jnp.max(s, axis=0, keepdims=True)).astype(BF16)
            o = jnp.dot(vt_ref[hv, :], e, preferred_element_type=F32)
            den = jnp.dot(ones, e, preferred_element_type=F32)
            return o, den[0:1, :]

        o1, den1 = unnormalised(0)
        o2, den2 = unnormalised(1)
        o = o1 * (1.0 / den1) - o2 * (lam / den2)
        o = o * lax.rsqrt(jnp.mean(o * o, axis=0, keepdims=True) + NORM_EPS)
        o = o.T * sg_ref[...] * (1.0 - lam_init)
        o_ref[:, hv] = (o * z_ref[:, hv].astype(F32)).astype(o_ref.dtype)


def _attn_latent_kernel(q_ref, k_ref, v_ref, z_ref, lq_ref, lk_ref, sg_ref, ck_ref, cv_ref, o_ref,
                        *, lam_init):
    lam = _diff_lambda(lq_ref, lk_ref, lam_init)

    def scores(h, c):
        cols = slice(h * AT_DV + c * AT_DH, h * AT_DV + (c + 1) * AT_DH)
        q = q_ref[:, cols]
        return _nt_dot(q, k_ref[:, cols]), _nt_dot(q, ck_ref[:, cols].astype(BF16))

    def softmax_parts(s, sc):
        m = jnp.maximum(jnp.max(s, axis=-1, keepdims=True), jnp.max(sc, axis=-1, keepdims=True))
        e = jnp.exp2(s - m)
        ec = jnp.exp2(sc - m)
        return e, ec, jnp.sum(e, axis=-1, keepdims=True) + jnp.sum(ec, axis=-1, keepdims=True)

    def finish(h, part1, part2):
        e1, ec1, den1 = part1
        e2, ec2, den2 = part2
        ratio = lam * den1 / den2
        hv = slice(h * AT_DV, (h + 1) * AT_DV)
        o = (jnp.dot((e1 - e2 * ratio).astype(BF16), v_ref[:, hv], preferred_element_type=F32)
             + jnp.dot((ec1 - ec2 * ratio).astype(BF16), cv_ref[:, hv].astype(BF16),
                       preferred_element_type=F32)) * (1.0 / den1)
        o = o * lax.rsqrt(jnp.mean(o * o, axis=-1, keepdims=True) + NORM_EPS) * sg_ref[...]
        o_ref[:, hv] = (o * (1.0 - lam_init) * z_ref[:, hv].astype(F32)).astype(o_ref.dtype)

    units = [(h, c) for h in range(AT_HEADS) for c in range(2)]
    ahead = [scores(*units[0]), scores(*units[1])]
    parts = [None, None]
    for i, (h, c) in enumerate(units):
        current = ahead.pop(0)
        if i + 2 < len(units):
            ahead.append(scores(*units[i + 2]))
        parts[c] = softmax_parts(*current)
        if c == 1:
            finish(h, parts[0], parts[1])


def _attention(act, batch, seq, lam_q, lam_k, subln_g, lam_init, ctx, layer):
    t = batch * seq
    tq = 256 if ctx is None else 512
    nq = seq // tq
    small = lambda a: pl.BlockSpec(a.shape, lambda b, i: (0,) * a.ndim)
    in_specs = [pl.BlockSpec((tq, CB), lambda b, i: (b * nq + i, COL_Q)),
                pl.BlockSpec((seq, CB), lambda b, i: (b, COL_K)),
                pl.BlockSpec((seq, CB), lambda b, i: (b, COL_V)),
                pl.BlockSpec((tq, CB), lambda b, i: (b * nq + i, COL_AT_Z)),
                small(lam_q), small(lam_k), small(subln_g)]
    args = [act, act, act, act, lam_q, lam_k, subln_g]
    if ctx is None:
        assert nq == 1
        body = functools.partial(_attn_ctx_kernel, lam_init=lam_init)
        scratch = [pltpu.VMEM((D_AT + ONES_ROWS, seq), BF16)]
    else:
        ck, cv = ctx
        cspec = pl.BlockSpec((None, None, ck.shape[2], D_AT), lambda b, i: (b, layer, 0, 0))
        in_specs += [cspec, cspec]
        args += [ck, cv]
        body = functools.partial(_attn_latent_kernel, lam_init=lam_init)
        scratch = []
    return pl.pallas_call(
        body,
        grid=(batch, nq),
        in_specs=in_specs,
        out_specs=pl.BlockSpec((tq, D_AT), lambda b, i: (b * nq + i, 0)),
        out_shape=jax.ShapeDtypeStruct((t, D_AT), BF16),
        scratch_shapes=scratch,
        compiler_params=_params("parallel", "parallel"),
        name=f"diff_attn_{seq}",
    )(*args)


def _pool_kernel(u_ref, z_ref, w_ref, sc_ref, o_ref, *, seq):
    padded = seq + 2 * POOL_PAD
    t = lax.broadcasted_iota(jnp.int32, (seq, 1), 0)
    zeros = jnp.zeros((POOL_PAD, POOL_GC), F32)
    for g, win in enumerate(POOL_WINDOWS):
        cols = slice(g * POOL_GC, (g + 1) * POOL_GC)
        u = u_ref[:, cols].astype(F32)
        s = jnp.concatenate([zeros, u, zeros], axis=0)
        s = s + pltpu.roll(s, 1, 0)
        half = 1
        while 2 * half < win:
            s = pltpu.roll(s, padded - half, 0) + pltpu.roll(s, half, 0)
            half *= 2
        lo = jnp.maximum(t - win // 2, 0)
        hi = jnp.minimum(t + win // 2, seq)
        m = s[POOL_PAD:POOL_PAD + seq] / (hi - lo).astype(F32) - u
        y = jnp.dot(m.astype(BF16), w_ref[g].astype(BF16), preferred_element_type=F32)
        o_ref[:, cols] = (y * sc_ref[:, cols] * z_ref[:, cols].astype(F32)).astype(o_ref.dtype)


def _pool_branch(proj, batch, seq, pool_w, pool_scale):
    t = batch * seq
    return pl.pallas_call(
        functools.partial(_pool_kernel, seq=seq),
        grid=(batch,),
        in_specs=[pl.BlockSpec((seq, CB), lambda b: (b, COL_PL_U)),
                  pl.BlockSpec((seq, CB), lambda b: (b, COL_PL_Z)),
                  pl.BlockSpec(pool_w.shape, lambda b: (0, 0, 0)),
                  pl.BlockSpec((1, D_PL), lambda b: (0, 0))],
        out_specs=pl.BlockSpec((seq, D_PL), lambda b: (b, 0)),
        out_shape=jax.ShapeDtypeStruct((t, D_PL), BF16),
        compiler_params=_params("parallel"),
        name=f"pool_{seq}",
    )(proj, proj, pool_w, pool_scale)


def _merge_kernel(x_ref, mod_ref, yh_ref, ya_ref, yp_ref, g0_ref, g1_ref, g2_ref,
                  wh_ref, wa_ref, wp_ref, wo_ref, o_ref):
    merged = (g0_ref[...].astype(F32) * jnp.dot(yh_ref[...], wh_ref[...], preferred_element_type=F32)
              + g1_ref[...].astype(F32) * jnp.dot(ya_ref[...], wa_ref[...], preferred_element_type=F32)
              + g2_ref[...].astype(F32) * jnp.dot(yp_ref[...], wp_ref[...], preferred_element_type=F32))
    out = jnp.dot(merged.astype(BF16), wo_ref[...], preferred_element_type=F32)
    o_ref[...] = x_ref[...] + mod_ref[:, 2 * D_MODEL:3 * D_MODEL] * out


def _merge(x, mod, mod_row0, rows_per_seq, act, y_hy, y_at, y_pl, w_hy_o, w_at_o, w_pl_o, w_out, tm=512):
    t = x.shape[0]
    blocks_per_seq = rows_per_seq // tm
    d = D_MODEL
    branch = pl.BlockSpec((tm, CB), lambda i: (i, 0))
    gate = lambda g: pl.BlockSpec((tm, d), lambda i: (i, COL_MG // 2 + g))
    return pl.pallas_call(
        _merge_kernel,
        grid=(t // tm,),
        in_specs=[pl.BlockSpec((tm, d), lambda i: (i, 0)),
                  pl.BlockSpec((None, 1, 3 * d), lambda i: (mod_row0 + i // blocks_per_seq, 0, 0)),
                  branch, branch, branch, gate(0), gate(1), gate(2),
                  _const_spec((CB, d)), _const_spec((CB, d)), _const_spec((CB, d)), _const_spec((d, d))],
        out_specs=pl.BlockSpec((tm, d), lambda i: (i, 0)),
        out_shape=jax.ShapeDtypeStruct((t, d), F32),
        compiler_params=_params("parallel"),
        name="merge",
    )(x, mod, y_hy, y_at, y_pl, act, act, act, w_hy_o, w_at_o, w_pl_o, w_out)


def _trunk_layer(x, batch, seq, mod, mod_row0, p, filters, dft, rope, ctx, layer, lam_init):
    mod_rows = seq if mod_row0 else batch * seq
    act, kv = _in_proj(x, mod, mod_row0, mod_rows, p["norm_g"], p["w_in"], layer, p["gq"], p["gk"], rope, seq,
                       keep_kv=rope is None)
    y_hy = _hyena_branch(act, batch, seq, p["hy_conv_w"], p["hy_conv_b"], p["hy_bias"], filters, dft)
    y_at = _attention(act, batch, seq, p["lam_q"], p["lam_k"], p["subln_g"], lam_init, ctx, layer)
    y_pl = _pool_branch(act, batch, seq, p["pool_w"], p["pool_scale"])
    out = _merge(x, mod, mod_row0, mod_rows, act, y_hy, y_at, y_pl,
                 p["w_hy_o"], p["w_at_o"], p["w_pl_o"], p["w_out"])
    return out, kv


def kernel(x_prompt, x_sample, cache_k, cache_v, c, c_ctx, norm_g, w_ada, b_ada, w_in, hy_conv_w, hy_conv_b,
           hy_f_w1, hy_f_b1, hy_f_w2, hy_f_b2, hy_f_w3, hy_freq, hy_bias, q_norm_g, k_norm_g, lam_q, lam_k,
           subln_g, pool_w, pool_scale, w_hy_o, w_at_o, w_pl_o, w_out):
    batch, seq, d = x_prompt.shape
    dec_batch, dec_seq, _ = x_sample.shape
    past = cache_k.shape[2]
    assert d == D_MODEL and dec_batch + 1 <= COND_ROWS and dec_seq % GRID_W == 0

    cond = jnp.concatenate([c_ctx[None, :], c, jnp.zeros((COND_ROWS - 1 - dec_batch, d), F32)], axis=0)
    mod = _modulation(cond, w_ada, b_ada).reshape(DEPTH, COND_ROWS, 1, 3 * d)
    dft = {s: _dft_tables(s) for s in (seq, dec_seq)}
    rope = _rope_tables(dec_seq)
    ctx = (cache_k.reshape(dec_batch, DEPTH, past, D_AT), cache_v.reshape(dec_batch, DEPTH, past, D_AT))

    w_in_bf16 = w_in.astype(BF16)
    yp = x_prompt.reshape(batch * seq, d)
    ys = x_sample.reshape(dec_batch * dec_seq, d)
    new_ks, new_vs = [], []
    for l in range(DEPTH):
        p = dict(norm_g=norm_g[l][None, :], w_in=w_in_bf16,
                 hy_conv_w=hy_conv_w[l], hy_conv_b=hy_conv_b[l], hy_bias=hy_bias[l],
                 gq=jnp.tile(q_norm_g[l], D_AT // AT_DH)[None, :], gk=jnp.tile(k_norm_g[l], D_AT // AT_DH)[None, :],
                 lam_q=lam_q[l], lam_k=lam_k[l], subln_g=subln_g[l][None, :],
                 pool_w=pool_w[l], pool_scale=pool_scale[l][None, :],
                 w_hy_o=w_hy_o[l].astype(BF16), w_at_o=w_at_o[l].astype(BF16),
                 w_pl_o=w_pl_o[l].astype(BF16), w_out=w_out[l].astype(BF16))
        filters = {s: _hyena_filters(s, hy_f_w1[l], hy_f_b1[l], hy_f_w2[l], hy_f_b2[l], hy_f_w3[l], hy_freq[l],
                                     dft[s]) for s in (seq, dec_seq)}
        lam_init = 0.8 - 0.6 * math.exp(-0.3 * l)
        yp, kv = _trunk_layer(yp, batch, seq, mod[l], 0, p, filters[seq], dft[seq], None, None, l, lam_init)
        new_ks.append(kv[0].reshape(batch, seq, AT_HEADS, 2, AT_DH))
        new_vs.append(kv[1].reshape(batch, seq, AT_HEADS, AT_DV))
        ys, _ = _trunk_layer(ys, dec_batch, dec_seq, mod[l], 1, p, filters[dec_seq], dft[dec_seq], rope, ctx, l,
                             lam_init)
    return (yp.reshape(batch, seq, d), ys.reshape(dec_batch, dec_seq, d),
            jnp.stack(new_ks, axis=1), jnp.stack(new_vs, axis=1))
```

```python
import functools
import math

import jax
import jax.numpy as jnp
from jax import lax
from jax.experimental import pallas as pl
from jax.experimental.pallas import tpu as pltpu

F32 = jnp.float32
BF16 = jnp.bfloat16
HIGHEST = lax.Precision.HIGHEST

D_MODEL = 1024
DEPTH = 2
GRID_W = 64
GRID_W_LOG2 = GRID_W.bit_length() - 1
assert GRID_W == 1 << GRID_W_LOG2
D_HY = 512
HY_ORDER = 2
HY_EMB = 33
HY_BANDS = (HY_EMB - 1) // 2
HY_FFN = 64
HY_FAST_DECAY = 0.3
HY_SLOW_DECAY = 1.5
HY_TARGET = 1e-2
AT_HEADS = 4
AT_DH = 64
AT_DV = 2 * AT_DH
D_AT = AT_HEADS * AT_DV
ROPE_BASE = 10000.0
Q_SCALE = AT_DH ** -0.5 * math.log2(math.e)
D_PL = 512
POOL_WINDOWS = (2, 4, 8, 16)
POOL_GC = D_PL // 4
POOL_PAD = 16
NORM_EPS = 1e-6
D_IN = 8192
COND_ROWS = 8

CB = 512
COL_HY_X1, COL_HY_X2, COL_HY_V, COL_HY_Z = 0, 1, 2, 3
COL_Q, COL_K, COL_V, COL_AT_Z, COL_PL_U, COL_PL_Z = 4, 5, 6, 7, 8, 9
COL_MG = 10

VMEM_LIMIT_BYTES = 56 * 1024 * 1024
LANES = 128


def _params(*sem):
    return pltpu.CompilerParams(dimension_semantics=sem, vmem_limit_bytes=VMEM_LIMIT_BYTES)


def _const_spec(shape):
    return pl.BlockSpec(shape, lambda *_: (0,) * len(shape), pipeline_mode=pl.Buffered(1))


def _silu(x):
    return x * jax.nn.sigmoid(x)


def _bdot(a, b):
    return jnp.dot(a.astype(BF16), b.astype(BF16), preferred_element_type=F32)


def _fdot(a, b):
    return jnp.dot(a, b, precision=HIGHEST, preferred_element_type=F32)


def _mod_kernel(cond_ref, w_ref, b_ref, o_ref):
    o_ref[...] = _fdot(_silu(cond_ref[...]), w_ref[...]) + b_ref[...]


def _modulation(cond, w_ada, b_ada):
    d = D_MODEL
    return pl.pallas_call(
        _mod_kernel,
        grid=(DEPTH, 3),
        in_specs=[
            pl.BlockSpec((COND_ROWS, d), lambda l, n: (0, 0)),
            pl.BlockSpec((None, d, d), lambda l, n: (l, 0, n)),
            pl.BlockSpec((None, 1, d), lambda l, n: (l, 0, n)),
        ],
        out_specs=pl.BlockSpec((None, COND_ROWS, d), lambda l, n: (l, 0, n)),
        out_shape=jax.ShapeDtypeStruct((DEPTH, COND_ROWS, 3 * d), F32),
        compiler_params=_params("parallel", "parallel"),
        name="modulation",
    )(cond, w_ada, b_ada.reshape(DEPTH, 1, 3 * d))


def _group_mean_matrix():
    r = lax.broadcasted_iota(jnp.int32, (D_AT, D_AT), 0) // AT_DH
    c = lax.broadcasted_iota(jnp.int32, (D_AT, D_AT), 1) // AT_DH
    return jnp.where(r == c, 1.0 / AT_DH, 0.0).astype(BF16)


def _group_rms_norm(x, g, mean_mat):
    ms = jnp.dot((x * x).astype(BF16), mean_mat, preferred_element_type=F32)
    return x * lax.rsqrt(ms + NORM_EPS) * g


def _rotate(x, cos, sin_signed):
    lane = lax.broadcasted_iota(jnp.int32, (1, D_AT), 1)
    partner = jnp.where((lane & 1) == 0, pltpu.roll(x, D_AT - 1, 1), pltpu.roll(x, 1, 1))
    return x * cos + partner * sin_signed


def _in_kernel(*refs, rope, keep_kv):
    x_ref, mod_ref, g_ref, w_ref, gq_ref, gk_ref = refs[:6]
    refs = refs[6:]
    if rope:
        cos_ref, sin_ref = refs[:2]
        refs = refs[2:]
    o_ref = refs[0]
    h_ref = refs[-1]

    x = x_ref[...]
    y = x * lax.rsqrt(jnp.mean(x * x, axis=-1, keepdims=True) + NORM_EPS) * g_ref[...]
    h_ref[...] = (y * (1.0 + mod_ref[:, D_MODEL:2 * D_MODEL]) + mod_ref[:, 0:D_MODEL]).astype(BF16)

    def chunk(c):
        return jnp.dot(h_ref[...], w_ref[:, c * CB:(c + 1) * CB], preferred_element_type=F32)

    def put(c, val):
        o_ref[:, c * CB:(c + 1) * CB] = val.astype(o_ref.dtype)

    for c in (COL_HY_X1, COL_HY_X2, COL_HY_V, COL_PL_U):
        put(c, chunk(c))
    for c in (COL_HY_Z, COL_AT_Z, COL_PL_Z):
        put(c, _silu(chunk(c)))
    for c in range(COL_MG, D_IN // CB):
        put(c, jax.nn.sigmoid(chunk(c)))

    mean_mat = _group_mean_matrix()
    q = _group_rms_norm(chunk(COL_Q), gq_ref[...], mean_mat)
    k = _group_rms_norm(chunk(COL_K), gk_ref[...], mean_mat)
    v = chunk(COL_V)
    if keep_kv:
        refs[1][...] = k
        refs[2][...] = v
    if rope:
        q = _rotate(q, cos_ref[...], sin_ref[...])
        k = _rotate(k, cos_ref[...], sin_ref[...])
    put(COL_Q, q * Q_SCALE)
    put(COL_K, k)
    put(COL_V, v)


def _in_proj(x, mod, mod_row0, rows_per_seq, norm_g, w_in_bf16, layer, gq, gk, rope, seq, keep_kv, tm=512):
    t = x.shape[0]
    blocks_per_seq = rows_per_seq // tm
    vec = pl.BlockSpec((1, D_AT), lambda i: (0, 0))
    in_specs = [pl.BlockSpec((tm, D_MODEL), lambda i: (i, 0)),
                pl.BlockSpec((None, 1, 3 * D_MODEL), lambda i: (mod_row0 + i // blocks_per_seq, 0, 0)),
                pl.BlockSpec((1, D_MODEL), lambda i: (0, 0)),
                pl.BlockSpec((None, D_MODEL, D_IN), lambda i: (layer, 0, 0), pipeline_mode=pl.Buffered(1)),
                vec, vec]
    args = [x, mod, norm_g, w_in_bf16, gq, gk]
    if rope is not None:
        per_seq = seq // tm
        tab = pl.BlockSpec((tm, D_AT), lambda i: (i % per_seq, 0))
        in_specs += [tab, tab]
        args += list(rope)
    out_specs = [pl.BlockSpec((tm, D_IN), lambda i: (i, 0))]
    out_shape = [jax.ShapeDtypeStruct((t, D_IN), BF16)]
    if keep_kv:
        out_specs += [pl.BlockSpec((tm, D_AT), lambda i: (i, 0))] * 2
        out_shape += [jax.ShapeDtypeStruct((t, D_AT), F32)] * 2
    out = pl.pallas_call(
        functools.partial(_in_kernel, rope=rope is not None, keep_kv=keep_kv),
        grid=(t // tm,),
        in_specs=in_specs,
        out_specs=out_specs,
        out_shape=out_shape,
        scratch_shapes=[pltpu.VMEM((tm, D_MODEL), BF16)],
        compiler_params=_params("parallel"),
        name="in_proj_latent" if rope is not None else "in_proj_ctx",
    )(*args)
    return (out[0], (out[1], out[2])) if keep_kv else (out[0], None)


TABLE_ROWS = 32


def _dft_kernel(ce_ref, se_ref, co_ref, so_ref, cot_ref, sot_ref, bc_ref, bs_ref, oc_ref, os_ref, *, seq):
    half = seq // 2
    col = lax.broadcasted_iota(jnp.int32, (1, half), 1)
    base = lax.broadcasted_iota(jnp.int32, (half // TABLE_ROWS, 1), 0) * TABLE_ROWS
    off = lax.broadcasted_iota(jnp.int32, (TABLE_ROWS, 1), 0)

    def trig(c_ref, s_ref, slot, product):
        ang = (product & (2 * seq - 1)).astype(F32) * (math.pi / seq)
        c_ref[slot] = jnp.cos(ang)
        s_ref[slot] = jnp.sin(ang)

    trig(bc_ref, bs_ref, 0, 2 * base * col)
    trig(bc_ref, bs_ref, 1, base * (2 * col + 1))
    trig(bc_ref, bs_ref, 2, (2 * base + 1) * col)
    trig(oc_ref, os_ref, 0, 2 * off * col)
    trig(oc_ref, os_ref, 1, off * (2 * col + 1))
    outs = ((ce_ref, se_ref, 0, 0), (co_ref, so_ref, 1, 1), (cot_ref, sot_ref, 2, 0))

    def block(i, carry):
        rows = pl.ds(pl.multiple_of(i * TABLE_ROWS, TABLE_ROWS), TABLE_ROWS)
        for c_ref, s_ref, b, o in outs:
            bc = bc_ref[b, pl.ds(i, 1), :]
            bs = bs_ref[b, pl.ds(i, 1), :]
            oc = oc_ref[o]
            osn = os_ref[o]
            c_ref[rows, :] = (bc * oc - bs * osn).astype(BF16)
            s_ref[rows, :] = (bs * oc + bc * osn).astype(BF16)
        return carry

    lax.fori_loop(0, half // TABLE_ROWS, block, 0)


def _dft_tables(seq):
    half = seq // 2
    shape = jax.ShapeDtypeStruct((half, half), BF16)
    return pl.pallas_call(
        functools.partial(_dft_kernel, seq=seq),
        out_shape=[shape] * 6,
        scratch_shapes=[pltpu.VMEM((3, half // TABLE_ROWS, half), F32), pltpu.VMEM((3, half // TABLE_ROWS, half), F32),
                        pltpu.VMEM((2, TABLE_ROWS, half), F32), pltpu.VMEM((2, TABLE_ROWS, half), F32)],
        compiler_params=pltpu.CompilerParams(vmem_limit_bytes=VMEM_LIMIT_BYTES),
        name=f"dft_tables_{seq}",
    )()


FILTER_CB = 256


def _filter_kernel(w1t_ref, w1c_ref, w1s_ref, b1_ref, w2_ref, b2_ref, fq_ref, w3f_ref, w3b_ref,
                   ce_ref, se_ref, co_ref, so_ref, alo_ref, blo_ref, ahi_ref, bhi_ref, mid_ref, h_ref, *, seq):
    half = seq // 2
    row = lax.broadcasted_iota(jnp.int32, (half, 1), 0)

    @pl.when(pl.program_id(0) == 0)
    def _():
        band = lax.broadcasted_iota(jnp.int32, (HY_BANDS, 1), 0).astype(F32)
        f = 1e-4 + band * ((HY_BANDS - 1 - 1e-4) / (HY_BANDS - 1))
        for p in range(2):
            pos = (2 * lax.broadcasted_iota(jnp.int32, (1, half), 1) + p).astype(F32)
            fw = f * (pos * (2.0 * math.pi / seq))
            pre = (w1t_ref[...] * (pos * (1.0 / (seq - 1))) + _fdot(w1c_ref[...], jnp.cos(fw))
                   - _fdot(w1s_ref[...], jnp.sin(fw)) + b1_ref[...])
            h = jnp.sin(fq_ref[:, 0:1] * pre)
            h = jnp.sin(fq_ref[:, 1:2] * (_fdot(w2_ref[...], h) + b2_ref[...]))
            h_ref[p] = h.T

    chan = lax.broadcasted_iota(jnp.int32, (1, FILTER_CB), 1) + (pl.program_id(0) % (D_HY // FILTER_CB)) * FILTER_CB
    max_decay = math.log(HY_TARGET) / HY_FAST_DECAY
    min_decay = math.log(HY_TARGET) / HY_SLOW_DECAY
    deltas = jnp.abs(min_decay + chan.astype(F32) * ((max_decay - min_decay) / (D_HY - 1)))

    def taps(p):
        decay = jnp.exp((2 * row + p).astype(F32) * (-1.0 / (seq - 1)) * deltas)
        hf = _fdot(h_ref[p], w3f_ref[...]) * decay
        hb = _fdot(h_ref[p], w3b_ref[...]) * decay
        if p == 0:
            hb = jnp.where(row == 0, 0.0, hb)
        return hf, hb

    hf0, hb0 = taps(0)
    hf1, hb1 = taps(1)
    inv = 1.0 / (jnp.sum(jnp.abs(hf0) + jnp.abs(hf1) + jnp.abs(hb0) + jnp.abs(hb1), axis=0, keepdims=True) + NORM_EPS)
    hs0, hs1 = (hf0 + hb0) * inv, (hf1 + hb1) * inv
    hd0, hd1 = (hf0 - hb0) * inv, (hf1 - hb1) * inv
    c_even = _bdot(ce_ref[...], hs0)
    c_odd = _bdot(co_ref[...], hs1)
    s_even = _bdot(se_ref[...], hd0)
    s_odd = _bdot(so_ref[...], hd1)
    wk = jnp.where(row == 0, 0.5 / seq, 1.0 / seq)
    alo_ref[...] = wk * (c_even + c_odd)
    ahi_ref[...] = wk * (c_even - c_odd)
    blo_ref[...] = -wk * (s_even + s_odd)
    bhi_ref[...] = wk * (s_even - s_odd)
    alt = (1 - 2 * (row & 1)).astype(F32)
    mid_ref[0:1, :] = jnp.sum(alt * hs0, axis=0, keepdims=True) * (1.0 / seq)
    mid_ref[1:2, :] = jnp.sum(alt * hd1, axis=0, keepdims=True) * (-1.0 / seq)


def _hyena_filters(seq, w1, b1, w2, b2, w3, freq, tables):
    nc = HY_ORDER * D_HY
    nblk = nc // FILTER_CB
    half = seq // 2
    small = lambda a: pl.BlockSpec(a.shape, lambda j: (0,) * a.ndim)
    w1t, w1c, w1s = w1[0:1].T, w1[1:1 + HY_BANDS].T, w1[1 + HY_BANDS:].T
    b1 = b1.reshape(HY_FFN, 1)
    b2 = b2.reshape(HY_FFN, 1)
    w2 = w2.T
    freq = freq.T
    out_blk = pl.BlockSpec((half, FILTER_CB), lambda j: (0, j))
    spec_shape = jax.ShapeDtypeStruct((half, nc), F32)
    return pl.pallas_call(
        functools.partial(_filter_kernel, seq=seq),
        grid=(nblk,),
        in_specs=[small(w1t), small(w1c), small(w1s), small(b1), small(w2), small(b2), small(freq),
                  pl.BlockSpec((HY_FFN, FILTER_CB), lambda j: (0, j)),
                  pl.BlockSpec((HY_FFN, FILTER_CB), lambda j: (0, nblk + j))] + [_const_spec((half, half))] * 4,
        out_specs=[out_blk] * 4 + [pl.BlockSpec((2, FILTER_CB), lambda j: (0, j))],
        out_shape=[spec_shape] * 4 + [jax.ShapeDtypeStruct((2, nc), F32)],
        scratch_shapes=[pltpu.VMEM((2, half, HY_FFN), F32)],
        compiler_params=_params("arbitrary"),
        name=f"hyena_filters_{seq}",
    )(w1t, w1c, w1s, b1, w2, b2, freq, w3, w3, *tables[:4])


def _hyena_kernel(*refs, seq, rb):
    x1_ref, x2_ref, v_ref, z_ref = refs[0:4]
    conv_w, conv_c = refs[4:7], refs[7:10]
    spectra = (refs[10:16], refs[16:22])
    ce_ref, se_ref, co_ref, so_ref, cot_ref, sot_ref = refs[22:28]
    o_ref, u_ref, g_ref, ub_ref, pq_ref = refs[28:33]
    half = seq // 2
    slabs = u_ref.shape[0]
    pos = lax.broadcasted_iota(jnp.int32, (seq, 1), 0)
    row = lax.broadcasted_iota(jnp.int32, (half, 1), 0)
    alt = (1 - 2 * (row & 1)).astype(F32)

    def short_conv(x_ref, w_ref, c_ref):
        x = x_ref[...].astype(F32)
        prev = jnp.where(pos == 0, 0.0, pltpu.roll(x, 1, 0))
        nxt = jnp.where(pos == seq - 1, 0.0, pltpu.roll(x, seq - 1, 0))
        return prev * w_ref[0:1, :] + x * w_ref[1:2, :] + nxt * w_ref[2:3, :] + c_ref[...]

    def put_u(rows, val):
        for s in range(slabs):
            u_ref[s, rows, :] = val[:, s * LANES:(s + 1) * LANES]

    def get_u(rows):
        return jnp.concatenate([u_ref[s, rows, :] for s in range(slabs)], axis=1)

    def long_conv(alo_ref, blo_ref, ahi_ref, bhi_ref, mid_ref, skip_ref, finish):
        for p in range(2):
            ub_ref[p] = get_u(pl.ds(p, half, stride=2)).astype(BF16)
        c_mid = jnp.sum(get_u(pl.ds(0, half, stride=2)) * alt, axis=0, keepdims=True)
        s_mid = jnp.sum(get_u(pl.ds(1, half, stride=2)) * alt, axis=0, keepdims=True)
        p_mid = c_mid * mid_ref[0:1, :] + s_mid * mid_ref[1:2, :]
        q_mid = s_mid * mid_ref[0:1, :] - c_mid * mid_ref[1:2, :]
        for r in range(0, half, rb):
            rows = pl.ds(r, rb)
            c_even = jnp.dot(ce_ref[rows, :], ub_ref[0], preferred_element_type=F32)
            c_odd = jnp.dot(co_ref[rows, :], ub_ref[1], preferred_element_type=F32)
            s_even = jnp.dot(se_ref[rows, :], ub_ref[0], preferred_element_type=F32)
            s_odd = jnp.dot(so_ref[rows, :], ub_ref[1], preferred_element_type=F32)
            c_lo, c_hi = c_even + c_odd, c_even - c_odd
            s_lo, s_hi = s_even + s_odd, s_odd - s_even
            alo, blo, ahi, bhi = alo_ref[rows, :], blo_ref[rows, :], ahi_ref[rows, :], bhi_ref[rows, :]
            p_lo, q_lo = c_lo * alo + s_lo * blo, s_lo * alo - c_lo * blo
            p_hi, q_hi = c_hi * ahi + s_hi * bhi, s_hi * ahi - c_hi * bhi
            pq_ref[0, rows, :] = (p_lo + p_hi).astype(BF16)
            pq_ref[1, rows, :] = (q_lo - q_hi).astype(BF16)
            pq_ref[2, rows, :] = (p_lo - p_hi).astype(BF16)
            pq_ref[3, rows, :] = (q_lo + q_hi).astype(BF16)
        for r in range(0, half, rb):
            rows = pl.ds(r, rb)
            even, odd = pl.ds(2 * r, rb, stride=2), pl.ds(2 * r + 1, rb, stride=2)
            ye = (jnp.dot(ce_ref[rows, :], pq_ref[0], preferred_element_type=F32)
                  + jnp.dot(se_ref[rows, :], pq_ref[1], preferred_element_type=F32)
                  + alt[r:r + rb] * p_mid + get_u(even) * skip_ref[...])
            yo = (jnp.dot(cot_ref[rows, :], pq_ref[2], preferred_element_type=F32)
                  + jnp.dot(sot_ref[rows, :], pq_ref[3], preferred_element_type=F32)
                  + alt[r:r + rb] * q_mid + get_u(odd) * skip_ref[...])
            put_u(even, ye)
            put_u(odd, yo)
            both = pl.ds(2 * r, 2 * rb)
            finish(both, g_ref[both, :] * get_u(both))

    def keep(rows, y):
        put_u(rows, y)

    def emit(rows, y):
        o_ref[rows, :] = (y * z_ref[rows, :].astype(F32)).astype(o_ref.dtype)

    put_u(pl.ds(0, seq), short_conv(v_ref, conv_w[2], conv_c[2]))
    g_ref[...] = short_conv(x1_ref, conv_w[0], conv_c[0])
    long_conv(*spectra[0], keep)
    g_ref[...] = short_conv(x2_ref, conv_w[1], conv_c[1])
    long_conv(*spectra[1], emit)


def _hyena_branch(act, batch, seq, conv_w, conv_b, hy_bias, filters, tables):
    cb = 256 if seq > 512 else D_HY
    half = seq // 2
    rb = min(half, 512)
    nj = D_HY // cb
    t = batch * seq
    conv_b = conv_b.reshape(1, 3 * D_HY)
    skip = hy_bias.reshape(1, HY_ORDER * D_HY)
    alo, blo, ahi, bhi, mid = filters

    def act_spec(col512):
        off = col512 * (CB // cb)
        return pl.BlockSpec((seq, cb), lambda j, b: (b, off + j))

    def chan_spec(rows, part):
        off = part * nj
        return pl.BlockSpec((rows, cb), lambda j, b: (0, off + j), pipeline_mode=pl.Buffered(1))

    in_specs = [act_spec(COL_HY_X1), act_spec(COL_HY_X2), act_spec(COL_HY_V), act_spec(COL_HY_Z)]
    in_specs += [chan_spec(3, part) for part in range(3)] + [chan_spec(1, part) for part in range(3)]
    args = [act] * 4 + [conv_w] * 3 + [conv_b] * 3
    for order in range(HY_ORDER):
        in_specs += [chan_spec(half, order)] * 4 + [chan_spec(2, order), chan_spec(1, order)]
        args += [alo, blo, ahi, bhi, mid, skip]
    in_specs += [_const_spec((half, half))] * 6
    args += list(tables)
    return pl.pallas_call(
        functools.partial(_hyena_kernel, seq=seq, rb=rb),
        grid=(nj, batch),
        in_specs=in_specs,
        out_specs=pl.BlockSpec((seq, cb), lambda j, b: (b, j)),
        out_shape=jax.ShapeDtypeStruct((t, D_HY), BF16),
        scratch_shapes=[pltpu.VMEM((cb // LANES, seq, LANES), F32), pltpu.VMEM((seq, cb), F32),
                        pltpu.VMEM((2, half, cb), BF16), pltpu.VMEM((4, half, cb), BF16)],
        compiler_params=_params("parallel", "parallel"),
        name=f"hyena_{seq}",
    )(*args)


def _rope_kernel(cos_ref, sin_ref, c_ref, s_ref, *, seq):
    lane = lax.broadcasted_iota(jnp.int32, (1, D_AT), 1)
    dim = lane & (AT_DH - 1)
    pair = ((dim & (AT_DH // 2 - 1)) >> 1).astype(F32)
    inv = jnp.exp(pair * (-(2.0 / (AT_DH // 2)) * math.log(ROPE_BASE)))
    ang = lax.broadcasted_iota(jnp.int32, (GRID_W, 1), 0).astype(F32) * inv
    c_ref[...] = jnp.cos(ang)
    s_ref[...] = jnp.where((lane & 1) == 0, -jnp.sin(ang), jnp.sin(ang))
    by_row = dim < AT_DH // 2

    def grid_row(r, carry):
        rows = pl.ds(pl.multiple_of(r * GRID_W, GRID_W), GRID_W)
        cos_ref[rows, :] = jnp.where(by_row, c_ref[pl.ds(r, 1), :], c_ref[...])
        sin_ref[rows, :] = jnp.where(by_row, s_ref[pl.ds(r, 1), :], s_ref[...])
        return carry

    lax.fori_loop(0, seq // GRID_W, grid_row, 0)


def _rope_tables(seq):
    assert seq // GRID_W <= GRID_W
    shape = jax.ShapeDtypeStruct((seq, D_AT), F32)
    return pl.pallas_call(
        functools.partial(_rope_kernel, seq=seq),
        out_shape=[shape, shape],
        scratch_shapes=[pltpu.VMEM((GRID_W, D_AT), F32), pltpu.VMEM((GRID_W, D_AT), F32)],
        compiler_params=pltpu.CompilerParams(vmem_limit_bytes=VMEM_LIMIT_BYTES),
        name="rope_tables",
    )()


def _nt_dot(a, b):
    return lax.dot_general(a, b, (((1,), (1,)), ((), ())), preferred_element_type=F32)


ONES_ROWS = 8


def _diff_lambda(lq_ref, lk_ref, lam_init):
    lqk = lq_ref[...] * lk_ref[...]
    return (jnp.exp(jnp.sum(lqk[0:1, :], axis=-1, keepdims=True))
            - jnp.exp(jnp.sum(lqk[1:2, :], axis=-1, keepdims=True)) + lam_init)


def _attn_ctx_kernel(q_ref, k_ref, v_ref, z_ref, lq_ref, lk_ref, sg_ref, o_ref, vt_ref, *, lam_init):
    seq = k_ref.shape[0]
    vt_ref[0:D_AT, :] = v_ref[...].astype(F32).T.astype(BF16)
    vt_ref[D_AT:D_AT + ONES_ROWS, :] = jnp.ones((ONES_ROWS, seq), BF16)
    lam = _diff_lambda(lq_ref, lk_ref, lam_init)
    ones = vt_ref[D_AT:D_AT + ONES_ROWS, :]

    for h in range(AT_HEADS):
        hv = slice(h * AT_DV, (h + 1) * AT_DV)

        def unnormalised(c):
            cols = slice(h * AT_DV + c * AT_DH, h * AT_DV + (c + 1) * AT_DH)
            s = _nt_dot(k_ref[:, cols], q_ref[:, cols])
            e = jnp.exp2(s - jnp.max(s, axis=0, keepdims=True)).astype(BF16)
            o = jnp.dot(vt_ref[hv, :], e, preferred_element_type=F32)
            den = jnp.dot(ones, e, preferred_element_type=F32)
            return o, den[0:1, :]

        o1, den1 = unnormalised(0)
        o2, den2 = unnormalised(1)
        o = o1 * (1.0 / den1) - o2 * (lam / den2)
        o = o * lax.rsqrt(jnp.mean(o * o, axis=0, keepdims=True) + NORM_EPS)
        o = o.T * sg_ref[...] * (1.0 - lam_init)
        o_ref[:, hv] = (o * z_ref[:, hv].astype(F32)).astype(o_ref.dtype)


SCORE_BOUND = 60.0


def _score_bound_kernel(gq_ref, gk_ref, ck_ref, o_ref):
    layer, b = pl.program_id(0), pl.program_id(1)
    ck = ck_ref[...]
    cached = jnp.dot((ck * ck).astype(BF16), _group_mean_matrix(), preferred_element_type=F32) * AT_DH
    gq = gq_ref[pl.ds(layer, 1), :]
    gk = gk_ref[pl.ds(layer, 1), :]
    k_sq = jnp.maximum(jnp.max(cached), AT_DH * jnp.max(gk * gk))
    q_sq = AT_DH * jnp.max(gq * gq) * (Q_SCALE * Q_SCALE)
    o_ref[layer, b] = 1.05 * q_sq * k_sq


def _score_bounds(q_norm_g, k_norm_g, ck):
    batch, depth, past, _ = ck.shape
    return pl.pallas_call(
        _score_bound_kernel,
        grid=(depth, batch),
        in_specs=[pl.BlockSpec(q_norm_g.shape, lambda l, b: (0, 0)), pl.BlockSpec(k_norm_g.shape, lambda l, b: (0, 0)),
                  pl.BlockSpec((None, None, past, D_AT), lambda l, b: (b, l, 0, 0))],
        out_specs=pl.BlockSpec(memory_space=pltpu.SMEM),
        out_shape=jax.ShapeDtypeStruct((depth, batch), F32),
        compiler_params=_params("arbitrary", "arbitrary"),
        name="score_bounds",
    )(q_norm_g, k_norm_g, ck)


def _attn_latent_kernel(q_ref, k_ref, v_ref, z_ref, lq_ref, lk_ref, sg_ref, bound_ref, ck_ref, cv_ref,
                        o_ref, *, lam_init, layer):
    lam = _diff_lambda(lq_ref, lk_ref, lam_init)
    small_scores = bound_ref[layer, pl.program_id(0)] <= SCORE_BOUND * SCORE_BOUND

    def scores(h, c):
        cols = slice(h * AT_DV + c * AT_DH, h * AT_DV + (c + 1) * AT_DH)
        q = q_ref[:, cols]
        return _nt_dot(q, k_ref[:, cols]), _nt_dot(q, ck_ref[:, cols].astype(BF16))

    def shifted_parts(s, sc):
        m = jnp.maximum(jnp.max(s, axis=-1, keepdims=True), jnp.max(sc, axis=-1, keepdims=True))
        e = jnp.exp2(s - m)
        ec = jnp.exp2(sc - m)
        return e, ec, jnp.sum(e, axis=-1, keepdims=True) + jnp.sum(ec, axis=-1, keepdims=True)

    def raw_parts(s, sc):
        e = jnp.exp2(s)
        ec = jnp.exp2(sc)
        return e, ec, jnp.sum(e, axis=-1, keepdims=True) + jnp.sum(ec, axis=-1, keepdims=True)

    def finish(h, part1, part2):
        e1, ec1, den1 = part1
        e2, ec2, den2 = part2
        ratio = lam * den1 / den2
        hv = slice(h * AT_DV, (h + 1) * AT_DV)
        o = (jnp.dot((e1 - e2 * ratio).astype(BF16), v_ref[:, hv], preferred_element_type=F32)
             + jnp.dot((ec1 - ec2 * ratio).astype(BF16), cv_ref[:, hv].astype(BF16),
                       preferred_element_type=F32)) * (1.0 / den1)
        o = o * lax.rsqrt(jnp.mean(o * o, axis=-1, keepdims=True) + NORM_EPS) * sg_ref[...]
        o_ref[:, hv] = (o * (1.0 - lam_init) * z_ref[:, hv].astype(F32)).astype(o_ref.dtype)

    def attend(softmax_parts):
        units = [(h, c) for h in range(AT_HEADS) for c in range(2)]
        ahead = [scores(*units[0]), scores(*units[1])]
        parts = [None, None]
        for i, (h, c) in enumerate(units):
            current = ahead.pop(0)
            if i + 2 < len(units):
                ahead.append(scores(*units[i + 2]))
            parts[c] = softmax_parts(*current)
            if c == 1:
                finish(h, parts[0], parts[1])

    pl.when(small_scores)(lambda: attend(raw_parts))
    pl.when(jnp.logical_not(small_scores))(lambda: attend(shifted_parts))


def _attention(act, batch, seq, lam_q, lam_k, subln_g, lam_init, ctx, layer):
    t = batch * seq
    tq = 256 if ctx is None else 512
    nq = seq // tq
    small = lambda a: pl.BlockSpec(a.shape, lambda b, i: (0,) * a.ndim)
    in_specs = [pl.BlockSpec((tq, CB), lambda b, i: (b * nq + i, COL_Q)),
                pl.BlockSpec((seq, CB), lambda b, i: (b, COL_K)),
                pl.BlockSpec((seq, CB), lambda b, i: (b, COL_V)),
                pl.BlockSpec((tq, CB), lambda b, i: (b * nq + i, COL_AT_Z)),
                small(lam_q), small(lam_k), small(subln_g)]
    args = [act, act, act, act, lam_q, lam_k, subln_g]
    if ctx is None:
        assert nq == 1
        body = functools.partial(_attn_ctx_kernel, lam_init=lam_init)
        scratch = [pltpu.VMEM((D_AT + ONES_ROWS, seq), BF16)]
    else:
        ck, cv, bounds = ctx
        cspec = pl.BlockSpec((None, None, ck.shape[2], D_AT), lambda b, i: (b, layer, 0, 0))
        in_specs += [pl.BlockSpec(memory_space=pltpu.SMEM), cspec, cspec]
        args += [bounds, ck, cv]
        body = functools.partial(_attn_latent_kernel, lam_init=lam_init, layer=layer)
        scratch = []
    return pl.pallas_call(
        body,
        grid=(batch, nq),
        in_specs=in_specs,
        out_specs=pl.BlockSpec((tq, D_AT), lambda b, i: (b * nq + i, 0)),
        out_shape=jax.ShapeDtypeStruct((t, D_AT), BF16),
        scratch_shapes=scratch,
        compiler_params=_params("parallel", "parallel"),
        name=f"diff_attn_{seq}",
    )(*args)


def _pool_kernel(u_ref, z_ref, w_ref, sc_ref, o_ref, *, seq):
    padded = seq + 2 * POOL_PAD
    t = lax.broadcasted_iota(jnp.int32, (seq, 1), 0)
    zeros = jnp.zeros((POOL_PAD, POOL_GC), F32)
    for g, win in enumerate(POOL_WINDOWS):
        cols = slice(g * POOL_GC, (g + 1) * POOL_GC)
        u = u_ref[:, cols].astype(F32)
        s = jnp.concatenate([zeros, u, zeros], axis=0)
        s = s + pltpu.roll(s, 1, 0)
        half = 1
        while 2 * half < win:
            s = pltpu.roll(s, padded - half, 0) + pltpu.roll(s, half, 0)
            half *= 2
        lo = jnp.maximum(t - win // 2, 0)
        hi = jnp.minimum(t + win // 2, seq)
        m = s[POOL_PAD:POOL_PAD + seq] / (hi - lo).astype(F32) - u
        y = jnp.dot(m.astype(BF16), w_ref[g].astype(BF16), preferred_element_type=F32)
        o_ref[:, cols] = (y * sc_ref[:, cols] * z_ref[:, cols].astype(F32)).astype(o_ref.dtype)


def _pool_branch(proj, batch, seq, pool_w, pool_scale):
    t = batch * seq
    return pl.pallas_call(
        functools.partial(_pool_kernel, seq=seq),
        grid=(batch,),
        in_specs=[pl.BlockSpec((seq, CB), lambda b: (b, COL_PL_U)),
                  pl.BlockSpec((seq, CB), lambda b: (b, COL_PL_Z)),
                  pl.BlockSpec(pool_w.shape, lambda b: (0, 0, 0)),
                  pl.BlockSpec((1, D_PL), lambda b: (0, 0))],
        out_specs=pl.BlockSpec((seq, D_PL), lambda b: (b, 0)),
        out_shape=jax.ShapeDtypeStruct((t, D_PL), BF16),
        compiler_params=_params("parallel"),
        name=f"pool_{seq}",
    )(proj, proj, pool_w, pool_scale)


def _merge_kernel(x_ref, mod_ref, yh_ref, ya_ref, yp_ref, g0_ref, g1_ref, g2_ref,
                  wh_ref, wa_ref, wp_ref, wo_ref, o_ref):
    merged = (g0_ref[...].astype(F32) * jnp.dot(yh_ref[...], wh_ref[...], preferred_element_type=F32)
              + g1_ref[...].astype(F32) * jnp.dot(ya_ref[...], wa_ref[...], preferred_element_type=F32)
              + g2_ref[...].astype(F32) * jnp.dot(yp_ref[...], wp_ref[...], preferred_element_type=F32))
    out = jnp.dot(merged.astype(BF16), wo_ref[...], preferred_element_type=F32)
    o_ref[...] = x_ref[...] + mod_ref[:, 2 * D_MODEL:3 * D_MODEL] * out


def _merge(x, mod, mod_row0, rows_per_seq, act, y_hy, y_at, y_pl, w_hy_o, w_at_o, w_pl_o, w_out, tm=1024):
    t = x.shape[0]
    blocks_per_seq = rows_per_seq // tm
    d = D_MODEL
    branch = pl.BlockSpec((tm, CB), lambda i: (i, 0))
    gate = lambda g: pl.BlockSpec((tm, d), lambda i: (i, COL_MG // 2 + g))
    return pl.pallas_call(
        _merge_kernel,
        grid=(t // tm,),
        in_specs=[pl.BlockSpec((tm, d), lambda i: (i, 0)),
                  pl.BlockSpec((None, 1, 3 * d), lambda i: (mod_row0 + i // blocks_per_seq, 0, 0)),
                  branch, branch, branch, gate(0), gate(1), gate(2),
                  _const_spec((CB, d)), _const_spec((CB, d)), _const_spec((CB, d)), _const_spec((d, d))],
        out_specs=pl.BlockSpec((tm, d), lambda i: (i, 0)),
        out_shape=jax.ShapeDtypeStruct((t, d), F32),
        compiler_params=_params("parallel"),
        name="merge",
    )(x, mod, y_hy, y_at, y_pl, act, act, act, w_hy_o, w_at_o, w_pl_o, w_out)


def _trunk_layer(x, batch, seq, mod, mod_row0, p, filters, dft, rope, ctx, layer, lam_init):
    mod_rows = seq if mod_row0 else batch * seq
    act, kv = _in_proj(x, mod, mod_row0, mod_rows, p["norm_g"], p["w_in"], layer, p["gq"], p["gk"], rope, seq,
                       keep_kv=rope is None)
    y_hy = _hyena_branch(act, batch, seq, p["hy_conv_w"], p["hy_conv_b"], p["hy_bias"], filters, dft)
    y_at = _attention(act, batch, seq, p["lam_q"], p["lam_k"], p["subln_g"], lam_init, ctx, layer)
    y_pl = _pool_branch(act, batch, seq, p["pool_w"], p["pool_scale"])
    out = _merge(x, mod, mod_row0, mod_rows, act, y_hy, y_at, y_pl,
                 p["w_hy_o"], p["w_at_o"], p["w_pl_o"], p["w_out"])
    return out, kv


def kernel(x_prompt, x_sample, cache_k, cache_v, c, c_ctx, norm_g, w_ada, b_ada, w_in, hy_conv_w, hy_conv_b,
           hy_f_w1, hy_f_b1, hy_f_w2, hy_f_b2, hy_f_w3, hy_freq, hy_bias, q_norm_g, k_norm_g, lam_q, lam_k,
           subln_g, pool_w, pool_scale, w_hy_o, w_at_o, w_pl_o, w_out):
    batch, seq, d = x_prompt.shape
    dec_batch, dec_seq, _ = x_sample.shape
    past = cache_k.shape[2]
    assert d == D_MODEL and dec_batch + 1 <= COND_ROWS and dec_seq % GRID_W == 0

    cond = jnp.concatenate([c_ctx[None, :], c, jnp.zeros((COND_ROWS - 1 - dec_batch, d), F32)], axis=0)
    mod = _modulation(cond, w_ada, b_ada).reshape(DEPTH, COND_ROWS, 1, 3 * d)
    dft = {s: _dft_tables(s) for s in (seq, dec_seq)}
    rope = _rope_tables(dec_seq)
    ctx_k = cache_k.reshape(dec_batch, DEPTH, past, D_AT)
    ctx = (ctx_k, cache_v.reshape(dec_batch, DEPTH, past, D_AT), _score_bounds(q_norm_g, k_norm_g, ctx_k))

    w_in_bf16 = w_in.astype(BF16)
    yp = x_prompt.reshape(batch * seq, d)
    ys = x_sample.reshape(dec_batch * dec_seq, d)
    new_ks, new_vs = [], []
    for l in range(DEPTH):
        p = dict(norm_g=norm_g[l][None, :], w_in=w_in_bf16,
                 hy_conv_w=hy_conv_w[l], hy_conv_b=hy_conv_b[l], hy_bias=hy_bias[l],
                 gq=jnp.tile(q_norm_g[l], D_AT // AT_DH)[None, :], gk=jnp.tile(k_norm_g[l], D_AT // AT_DH)[None, :],
                 lam_q=lam_q[l], lam_k=lam_k[l], subln_g=subln_g[l][None, :],
                 pool_w=pool_w[l], pool_scale=pool_scale[l][None, :],
                 w_hy_o=w_hy_o[l].astype(BF16), w_at_o=w_at_o[l].astype(BF16),
                 w_pl_o=w_pl_o[l].astype(BF16), w_out=w_out[l].astype(BF16))
        filters = {s: _hyena_filters(s, hy_f_w1[l], hy_f_b1[l], hy_f_w2[l], hy_f_b2[l], hy_f_w3[l], hy_freq[l],
                                     dft[s]) for s in (seq, dec_seq)}
        lam_init = 0.8 - 0.6 * math.exp(-0.3 * l)
        yp, kv = _trunk_layer(yp, batch, seq, mod[l], 0, p, filters[seq], dft[seq], None, None, l, lam_init)
        new_ks.append(kv[0].reshape(batch, seq, AT_HEADS, 2, AT_DH))
        new_vs.append(kv[1].reshape(batch, seq, AT_HEADS, AT_DV))
        ys, _ = _trunk_layer(ys, dec_batch, dec_seq, mod[l], 1, p, filters[dec_seq], dft[dec_seq], rope, ctx, l,
                             lam_init)
    return (yp.reshape(batch, seq, d), ys.reshape(dec_batch, dec_seq, d),
            jnp.stack(new_ks, axis=1), jnp.stack(new_vs, axis=1))
```

```python
import functools
import math

import jax
import jax.numpy as jnp
from jax import lax
from jax.experimental import pallas as pl
from jax.experimental.pallas import tpu as pltpu

F32 = jnp.float32
BF16 = jnp.bfloat16
HIGHEST = lax.Precision.HIGHEST

D_MODEL = 1024
DEPTH = 2
GRID_W = 64
GRID_W_LOG2 = GRID_W.bit_length() - 1
assert GRID_W == 1 << GRID_W_LOG2
D_HY = 512
HY_ORDER = 2
HY_EMB = 33
HY_BANDS = (HY_EMB - 1) // 2
HY_FFN = 64
HY_FAST_DECAY = 0.3
HY_SLOW_DECAY = 1.5
HY_TARGET = 1e-2
AT_HEADS = 4
AT_DH = 64
AT_DV = 2 * AT_DH
D_AT = AT_HEADS * AT_DV
ROPE_BASE = 10000.0
Q_SCALE = AT_DH ** -0.5 * math.log2(math.e)
D_PL = 512
POOL_WINDOWS = (2, 4, 8, 16)
POOL_GC = D_PL // 4
POOL_PAD = 16
NORM_EPS = 1e-6
D_IN = 8192
COND_ROWS = 8

CB = 512
COL_HY_X1, COL_HY_X2, COL_HY_V, COL_HY_Z = 0, 1, 2, 3
COL_Q, COL_K, COL_V, COL_AT_Z, COL_PL_U, COL_PL_Z = 4, 5, 6, 7, 8, 9
COL_MG = 10

VMEM_LIMIT_BYTES = 56 * 1024 * 1024
LANES = 128


def _params(*sem):
    return pltpu.CompilerParams(dimension_semantics=sem, vmem_limit_bytes=VMEM_LIMIT_BYTES)


def _const_spec(shape):
    return pl.BlockSpec(shape, lambda *_: (0,) * len(shape), pipeline_mode=pl.Buffered(1))


def _silu(x):
    return x * jax.nn.sigmoid(x)


def _bdot(a, b):
    return jnp.dot(a.astype(BF16), b.astype(BF16), preferred_element_type=F32)


def _fdot(a, b):
    return jnp.dot(a, b, precision=HIGHEST, preferred_element_type=F32)


def _mod_kernel(cond_ref, w_ref, b_ref, o_ref):
    o_ref[...] = _fdot(_silu(cond_ref[...]), w_ref[...]) + b_ref[...]


def _modulation(cond, w_ada, b_ada):
    d = D_MODEL
    return pl.pallas_call(
        _mod_kernel,
        grid=(DEPTH, 3),
        in_specs=[
            pl.BlockSpec((COND_ROWS, d), lambda l, n: (0, 0)),
            pl.BlockSpec((None, d, d), lambda l, n: (l, 0, n)),
            pl.BlockSpec((None, 1, d), lambda l, n: (l, 0, n)),
        ],
        out_specs=pl.BlockSpec((None, COND_ROWS, d), lambda l, n: (l, 0, n)),
        out_shape=jax.ShapeDtypeStruct((DEPTH, COND_ROWS, 3 * d), F32),
        compiler_params=_params("parallel", "parallel"),
        name="modulation",
    )(cond, w_ada, b_ada.reshape(DEPTH, 1, 3 * d))


def _group_mean_matrix():
    r = lax.broadcasted_iota(jnp.int32, (D_AT, D_AT), 0) // AT_DH
    c = lax.broadcasted_iota(jnp.int32, (D_AT, D_AT), 1) // AT_DH
    return jnp.where(r == c, 1.0 / AT_DH, 0.0).astype(BF16)


def _group_rms_norm(x, g, mean_mat):
    ms = jnp.dot((x * x).astype(BF16), mean_mat, preferred_element_type=F32)
    return x * lax.rsqrt(ms + NORM_EPS) * g


def _rotate(x, cos, sin_signed):
    lane = lax.broadcasted_iota(jnp.int32, (1, D_AT), 1)
    partner = jnp.where((lane & 1) == 0, pltpu.roll(x, D_AT - 1, 1), pltpu.roll(x, 1, 1))
    return x * cos + partner * sin_signed


def _in_kernel(*refs, rope, kv_slots, seq):
    x_ref, mod_ref, g_ref, w_ref, gq_ref, gk_ref = refs[:6]
    refs = refs[6:]
    if rope:
        cos_ref, sin_ref = refs[:2]
        refs = refs[2:]
    if kv_slots == "own":
        refs = refs[2:]
    o_ref = refs[0]
    h_ref = refs[-1]

    x = x_ref[...]
    y = x * lax.rsqrt(jnp.mean(x * x, axis=-1, keepdims=True) + NORM_EPS) * g_ref[...]
    h_ref[...] = (y * (1.0 + mod_ref[:, D_MODEL:2 * D_MODEL]) + mod_ref[:, 0:D_MODEL]).astype(BF16)

    def chunk(c):
        return jnp.dot(h_ref[...], w_ref[:, c * CB:(c + 1) * CB], preferred_element_type=F32)

    def put(c, val):
        o_ref[:, c * CB:(c + 1) * CB] = val.astype(o_ref.dtype)

    for c in (COL_HY_X1, COL_HY_X2, COL_HY_V, COL_PL_U):
        put(c, chunk(c))
    for c in (COL_HY_Z, COL_AT_Z, COL_PL_Z):
        put(c, _silu(chunk(c)))
    for c in range(COL_MG, D_IN // CB):
        put(c, jax.nn.sigmoid(chunk(c)))

    mean_mat = _group_mean_matrix()
    q = _group_rms_norm(chunk(COL_Q), gq_ref[...], mean_mat)
    k = _group_rms_norm(chunk(COL_K), gk_ref[...], mean_mat)
    v = chunk(COL_V)
    for dst_ref, val in ((refs[1], k), (refs[2], v)) if kv_slots else ():
        for s in range(val.shape[0] // seq):
            rows = val[s * seq:(s + 1) * seq]
            if kv_slots == "all":
                dst_ref[s, 0] = rows
                dst_ref[s, 1:] = jnp.zeros((dst_ref.shape[1] - 1,) + rows.shape, F32)
            else:
                dst_ref[s] = rows
    if rope:
        q = _rotate(q, cos_ref[...], sin_ref[...])
        k = _rotate(k, cos_ref[...], sin_ref[...])
    put(COL_Q, q * Q_SCALE)
    put(COL_K, k)
    put(COL_V, v)


def _in_proj(x, mod, mod_row0, rows_per_seq, norm_g, w_in_bf16, layer, gq, gk, rope, seq, keep_kv, kv_prev,
             tm=512):
    t = x.shape[0]
    blocks_per_seq = rows_per_seq // tm
    vec = pl.BlockSpec((1, D_AT), lambda i: (0, 0))
    in_specs = [pl.BlockSpec((tm, D_MODEL), lambda i: (i, 0)),
                pl.BlockSpec((None, 1, 3 * D_MODEL), lambda i: (mod_row0 + i // blocks_per_seq, 0, 0)),
                pl.BlockSpec((1, D_MODEL), lambda i: (0, 0)),
                pl.BlockSpec((None, D_MODEL, D_IN), lambda i: (layer, 0, 0), pipeline_mode=pl.Buffered(1)),
                vec, vec]
    args = [x, mod, norm_g, w_in_bf16, gq, gk]
    if rope is not None:
        per_seq = seq // tm
        tab = pl.BlockSpec((tm, D_AT), lambda i: (i % per_seq, 0))
        in_specs += [tab, tab]
        args += list(rope)
    out_specs = [pl.BlockSpec((tm, D_IN), lambda i: (i, 0))]
    out_shape = [jax.ShapeDtypeStruct((t, D_IN), BF16)]
    kv_slots, aliases = None, {}
    if keep_kv:
        spb = tm // seq
        shape = jax.ShapeDtypeStruct((t // seq, DEPTH, seq, D_AT), F32)
        out_shape += [shape, shape]
        if kv_prev is None:
            assert layer == 0
            kv_slots = "all"
            out_specs += [pl.BlockSpec((spb, DEPTH, seq, D_AT), lambda i: (i, 0, 0, 0))] * 2
        else:
            kv_slots = "own"
            out_specs += [pl.BlockSpec((spb, None, seq, D_AT), lambda i: (i, layer, 0, 0))] * 2
            aliases = {len(args): 1, len(args) + 1: 2}
            in_specs += [pl.BlockSpec(memory_space=pl.ANY)] * 2
            args += list(kv_prev)
    out = pl.pallas_call(
        functools.partial(_in_kernel, rope=rope is not None, kv_slots=kv_slots, seq=seq),
        grid=(t // tm,),
        in_specs=in_specs,
        out_specs=out_specs,
        out_shape=out_shape,
        input_output_aliases=aliases,
        scratch_shapes=[pltpu.VMEM((tm, D_MODEL), BF16)],
        compiler_params=_params("parallel"),
        name="in_proj_latent" if rope is not None else "in_proj_ctx",
    )(*args)
    return (out[0], (out[1], out[2])) if keep_kv else (out[0], None)


TABLE_ROWS = 32


def _dft_kernel(ce_ref, se_ref, co_ref, so_ref, cot_ref, sot_ref, bc_ref, bs_ref, oc_ref, os_ref, *, seq):
    half = seq // 2
    col = lax.broadcasted_iota(jnp.int32, (1, half), 1)
    base = lax.broadcasted_iota(jnp.int32, (half // TABLE_ROWS, 1), 0) * TABLE_ROWS
    off = lax.broadcasted_iota(jnp.int32, (TABLE_ROWS, 1), 0)

    def trig(c_ref, s_ref, slot, product):
        ang = (product & (2 * seq - 1)).astype(F32) * (math.pi / seq)
        c_ref[slot] = jnp.cos(ang)
        s_ref[slot] = jnp.sin(ang)

    trig(bc_ref, bs_ref, 0, 2 * base * col)
    trig(bc_ref, bs_ref, 1, base * (2 * col + 1))
    trig(bc_ref, bs_ref, 2, (2 * base + 1) * col)
    trig(oc_ref, os_ref, 0, 2 * off * col)
    trig(oc_ref, os_ref, 1, off * (2 * col + 1))
    outs = ((ce_ref, se_ref, 0, 0), (co_ref, so_ref, 1, 1), (cot_ref, sot_ref, 2, 0))

    def block(i, carry):
        rows = pl.ds(pl.multiple_of(i * TABLE_ROWS, TABLE_ROWS), TABLE_ROWS)
        for c_ref, s_ref, b, o in outs:
            bc = bc_ref[b, pl.ds(i, 1), :]
            bs = bs_ref[b, pl.ds(i, 1), :]
            oc = oc_ref[o]
            osn = os_ref[o]
            c_ref[rows, :] = (bc * oc - bs * osn).astype(BF16)
            s_ref[rows, :] = (bs * oc + bc * osn).astype(BF16)
        return carry

    lax.fori_loop(0, half // TABLE_ROWS, block, 0)


def _dft_tables(seq):
    half = seq // 2
    shape = jax.ShapeDtypeStruct((half, half), BF16)
    return pl.pallas_call(
        functools.partial(_dft_kernel, seq=seq),
        out_shape=[shape] * 6,
        scratch_shapes=[pltpu.VMEM((3, half // TABLE_ROWS, half), F32), pltpu.VMEM((3, half // TABLE_ROWS, half), F32),
                        pltpu.VMEM((2, TABLE_ROWS, half), F32), pltpu.VMEM((2, TABLE_ROWS, half), F32)],
        compiler_params=pltpu.CompilerParams(vmem_limit_bytes=VMEM_LIMIT_BYTES),
        name=f"dft_tables_{seq}",
    )()


FILTER_CB = 256


def _filter_kernel(w1t_ref, w1c_ref, w1s_ref, b1_ref, w2_ref, b2_ref, fq_ref, w3f_ref, w3b_ref,
                   ce_ref, se_ref, co_ref, so_ref, alo_ref, blo_ref, ahi_ref, bhi_ref, mid_ref, h_ref, *, seq):
    half = seq // 2
    row = lax.broadcasted_iota(jnp.int32, (half, 1), 0)

    @pl.when(pl.program_id(0) == 0)
    def _():
        band = lax.broadcasted_iota(jnp.int32, (HY_BANDS, 1), 0).astype(F32)
        f = 1e-4 + band * ((HY_BANDS - 1 - 1e-4) / (HY_BANDS - 1))
        for p in range(2):
            pos = (2 * lax.broadcasted_iota(jnp.int32, (1, half), 1) + p).astype(F32)
            fw = f * (pos * (2.0 * math.pi / seq))
            pre = (w1t_ref[...] * (pos * (1.0 / (seq - 1))) + _fdot(w1c_ref[...], jnp.cos(fw))
                   - _fdot(w1s_ref[...], jnp.sin(fw)) + b1_ref[...])
            h = jnp.sin(fq_ref[:, 0:1] * pre)
            h = jnp.sin(fq_ref[:, 1:2] * (_fdot(w2_ref[...], h) + b2_ref[...]))
            h_ref[p] = h.T

    chan = lax.broadcasted_iota(jnp.int32, (1, FILTER_CB), 1) + (pl.program_id(0) % (D_HY // FILTER_CB)) * FILTER_CB
    max_decay = math.log(HY_TARGET) / HY_FAST_DECAY
    min_decay = math.log(HY_TARGET) / HY_SLOW_DECAY
    deltas = jnp.abs(min_decay + chan.astype(F32) * ((max_decay - min_decay) / (D_HY - 1)))

    def taps(p):
        decay = jnp.exp((2 * row + p).astype(F32) * (-1.0 / (seq - 1)) * deltas)
        hf = _fdot(h_ref[p], w3f_ref[...]) * decay
        hb = _fdot(h_ref[p], w3b_ref[...]) * decay
        if p == 0:
            hb = jnp.where(row == 0, 0.0, hb)
        return hf, hb

    hf0, hb0 = taps(0)
    hf1, hb1 = taps(1)
    inv = 1.0 / (jnp.sum(jnp.abs(hf0) + jnp.abs(hf1) + jnp.abs(hb0) + jnp.abs(hb1), axis=0, keepdims=True) + NORM_EPS)
    hs0, hs1 = (hf0 + hb0) * inv, (hf1 + hb1) * inv
    hd0, hd1 = (hf0 - hb0) * inv, (hf1 - hb1) * inv
    c_even = _bdot(ce_ref[...], hs0)
    c_odd = _bdot(co_ref[...], hs1)
    s_even = _bdot(se_ref[...], hd0)
    s_odd = _bdot(so_ref[...], hd1)
    wk = jnp.where(row == 0, 0.5 / seq, 1.0 / seq)
    alo_ref[...] = wk * (c_even + c_odd)
    ahi_ref[...] = wk * (c_even - c_odd)
    blo_ref[...] = -wk * (s_even + s_odd)
    bhi_ref[...] = wk * (s_even - s_odd)
    alt = (1 - 2 * (row & 1)).astype(F32)
    mid_ref[0:1, :] = jnp.sum(alt * hs0, axis=0, keepdims=True) * (1.0 / seq)
    mid_ref[1:2, :] = jnp.sum(alt * hd1, axis=0, keepdims=True) * (-1.0 / seq)


def _hyena_filters(seq, w1, b1, w2, b2, w3, freq, tables):
    nc = HY_ORDER * D_HY
    nblk = nc // FILTER_CB
    half = seq // 2
    small = lambda a: pl.BlockSpec(a.shape, lambda j: (0,) * a.ndim)
    w1t, w1c, w1s = w1[0:1].T, w1[1:1 + HY_BANDS].T, w1[1 + HY_BANDS:].T
    b1 = b1.reshape(HY_FFN, 1)
    b2 = b2.reshape(HY_FFN, 1)
    w2 = w2.T
    freq = freq.T
    out_blk = pl.BlockSpec((half, FILTER_CB), lambda j: (0, j))
    spec_shape = jax.ShapeDtypeStruct((half, nc), F32)
    return pl.pallas_call(
        functools.partial(_filter_kernel, seq=seq),
        grid=(nblk,),
        in_specs=[small(w1t), small(w1c), small(w1s), small(b1), small(w2), small(b2), small(freq),
                  pl.BlockSpec((HY_FFN, FILTER_CB), lambda j: (0, j)),
                  pl.BlockSpec((HY_FFN, FILTER_CB), lambda j: (0, nblk + j))] + [_const_spec((half, half))] * 4,
        out_specs=[out_blk] * 4 + [pl.BlockSpec((2, FILTER_CB), lambda j: (0, j))],
        out_shape=[spec_shape] * 4 + [jax.ShapeDtypeStruct((2, nc), F32)],
        scratch_shapes=[pltpu.VMEM((2, half, HY_FFN), F32)],
        compiler_params=_params("arbitrary"),
        name=f"hyena_filters_{seq}",
    )(w1t, w1c, w1s, b1, w2, b2, freq, w3, w3, *tables[:4])


def _hyena_kernel(*refs, seq, rb):
    x1_ref, x2_ref, v_ref, z_ref = refs[0:4]
    conv_w, conv_c = refs[4:7], refs[7:10]
    spectra = (refs[10:16], refs[16:22])
    ce_ref, se_ref, co_ref, so_ref, cot_ref, sot_ref = refs[22:28]
    o_ref, u_ref, g_ref, ub_ref, pq_ref = refs[28:33]
    half = seq // 2
    slabs = u_ref.shape[0]
    pos = lax.broadcasted_iota(jnp.int32, (seq, 1), 0)
    row = lax.broadcasted_iota(jnp.int32, (half, 1), 0)
    alt = (1 - 2 * (row & 1)).astype(F32)

    def short_conv(x_ref, w_ref, c_ref):
        x = x_ref[...].astype(F32)
        prev = jnp.where(pos == 0, 0.0, pltpu.roll(x, 1, 0))
        nxt = jnp.where(pos == seq - 1, 0.0, pltpu.roll(x, seq - 1, 0))
        return prev * w_ref[0:1, :] + x * w_ref[1:2, :] + nxt * w_ref[2:3, :] + c_ref[...]

    def put_u(rows, val):
        for s in range(slabs):
            u_ref[s, rows, :] = val[:, s * LANES:(s + 1) * LANES]

    def get_u(rows):
        return jnp.concatenate([u_ref[s, rows, :] for s in range(slabs)], axis=1)

    def long_conv(alo_ref, blo_ref, ahi_ref, bhi_ref, mid_ref, skip_ref, finish):
        for p in range(2):
            ub_ref[p] = get_u(pl.ds(p, half, stride=2)).astype(BF16)
        c_mid = jnp.sum(get_u(pl.ds(0, half, stride=2)) * alt, axis=0, keepdims=True)
        s_mid = jnp.sum(get_u(pl.ds(1, half, stride=2)) * alt, axis=0, keepdims=True)
        p_mid = c_mid * mid_ref[0:1, :] + s_mid * mid_ref[1:2, :]
        q_mid = s_mid * mid_ref[0:1, :] - c_mid * mid_ref[1:2, :]
        for r in range(0, half, rb):
            rows = pl.ds(r, rb)
            c_even = jnp.dot(ce_ref[rows, :], ub_ref[0], preferred_element_type=F32)
            c_odd = jnp.dot(co_ref[rows, :], ub_ref[1], preferred_element_type=F32)
            s_even = jnp.dot(se_ref[rows, :], ub_ref[0], preferred_element_type=F32)
            s_odd = jnp.dot(so_ref[rows, :], ub_ref[1], preferred_element_type=F32)
            c_lo, c_hi = c_even + c_odd, c_even - c_odd
            s_lo, s_hi = s_even + s_odd, s_odd - s_even
            alo, blo, ahi, bhi = alo_ref[rows, :], blo_ref[rows, :], ahi_ref[rows, :], bhi_ref[rows, :]
            p_lo, q_lo = c_lo * alo + s_lo * blo, s_lo * alo - c_lo * blo
            p_hi, q_hi = c_hi * ahi + s_hi * bhi, s_hi * ahi - c_hi * bhi
            pq_ref[0, rows, :] = (p_lo + p_hi).astype(BF16)
            pq_ref[1, rows, :] = (q_lo - q_hi).astype(BF16)
            pq_ref[2, rows, :] = (p_lo - p_hi).astype(BF16)
            pq_ref[3, rows, :] = (q_lo + q_hi).astype(BF16)
        for r in range(0, half, rb):
            rows = pl.ds(r, rb)
            even, odd = pl.ds(2 * r, rb, stride=2), pl.ds(2 * r + 1, rb, stride=2)
            ye = (jnp.dot(ce_ref[rows, :], pq_ref[0], preferred_element_type=F32)
                  + jnp.dot(se_ref[rows, :], pq_ref[1], preferred_element_type=F32)
                  + alt[r:r + rb] * p_mid + get_u(even) * skip_ref[...])
            yo = (jnp.dot(cot_ref[rows, :], pq_ref[2], preferred_element_type=F32)
                  + jnp.dot(sot_ref[rows, :], pq_ref[3], preferred_element_type=F32)
                  + alt[r:r + rb] * q_mid + get_u(odd) * skip_ref[...])
            put_u(even, ye)
            put_u(odd, yo)
            both = pl.ds(2 * r, 2 * rb)
            finish(both, g_ref[both, :] * get_u(both))

    def keep(rows, y):
        put_u(rows, y)

    def emit(rows, y):
        o_ref[rows, :] = (y * z_ref[rows, :].astype(F32)).astype(o_ref.dtype)

    put_u(pl.ds(0, seq), short_conv(v_ref, conv_w[2], conv_c[2]))
    g_ref[...] = short_conv(x1_ref, conv_w[0], conv_c[0])
    long_conv(*spectra[0], keep)
    g_ref[...] = short_conv(x2_ref, conv_w[1], conv_c[1])
    long_conv(*spectra[1], emit)


def _hyena_branch(act, batch, seq, conv_w, conv_b, hy_bias, filters, tables):
    cb = 256 if seq > 512 else D_HY
    half = seq // 2
    rb = min(half, 512)
    nj = D_HY // cb
    t = batch * seq
    conv_b = conv_b.reshape(1, 3 * D_HY)
    skip = hy_bias.reshape(1, HY_ORDER * D_HY)
    alo, blo, ahi, bhi, mid = filters

    def act_spec(col512):
        off = col512 * (CB // cb)
        return pl.BlockSpec((seq, cb), lambda j, b: (b, off + j))

    def chan_spec(rows, part):
        off = part * nj
        return pl.BlockSpec((rows, cb), lambda j, b: (0, off + j), pipeline_mode=pl.Buffered(1))

    in_specs = [act_spec(COL_HY_X1), act_spec(COL_HY_X2), act_spec(COL_HY_V), act_spec(COL_HY_Z)]
    in_specs += [chan_spec(3, part) for part in range(3)] + [chan_spec(1, part) for part in range(3)]
    args = [act] * 4 + [conv_w] * 3 + [conv_b] * 3
    for order in range(HY_ORDER):
        in_specs += [chan_spec(half, order)] * 4 + [chan_spec(2, order), chan_spec(1, order)]
        args += [alo, blo, ahi, bhi, mid, skip]
    in_specs += [_const_spec((half, half))] * 6
    args += list(tables)
    return pl.pallas_call(
        functools.partial(_hyena_kernel, seq=seq, rb=rb),
        grid=(nj, batch),
        in_specs=in_specs,
        out_specs=pl.BlockSpec((seq, cb), lambda j, b: (b, j)),
        out_shape=jax.ShapeDtypeStruct((t, D_HY), BF16),
        scratch_shapes=[pltpu.VMEM((cb // LANES, seq, LANES), F32), pltpu.VMEM((seq, cb), F32),
                        pltpu.VMEM((2, half, cb), BF16), pltpu.VMEM((4, half, cb), BF16)],
        compiler_params=_params("parallel", "parallel"),
        name=f"hyena_{seq}",
    )(*args)


def _rope_kernel(cos_ref, sin_ref, c_ref, s_ref, *, seq):
    lane = lax.broadcasted_iota(jnp.int32, (1, D_AT), 1)
    dim = lane & (AT_DH - 1)
    pair = ((dim & (AT_DH // 2 - 1)) >> 1).astype(F32)
    inv = jnp.exp(pair * (-(2.0 / (AT_DH // 2)) * math.log(ROPE_BASE)))
    ang = lax.broadcasted_iota(jnp.int32, (GRID_W, 1), 0).astype(F32) * inv
    c_ref[...] = jnp.cos(ang)
    s_ref[...] = jnp.where((lane & 1) == 0, -jnp.sin(ang), jnp.sin(ang))
    by_row = dim < AT_DH // 2

    def grid_row(r, carry):
        rows = pl.ds(pl.multiple_of(r * GRID_W, GRID_W), GRID_W)
        cos_ref[rows, :] = jnp.where(by_row, c_ref[pl.ds(r, 1), :], c_ref[...])
        sin_ref[rows, :] = jnp.where(by_row, s_ref[pl.ds(r, 1), :], s_ref[...])
        return carry

    lax.fori_loop(0, seq // GRID_W, grid_row, 0)


def _rope_tables(seq):
    assert seq // GRID_W <= GRID_W
    shape = jax.ShapeDtypeStruct((seq, D_AT), F32)
    return pl.pallas_call(
        functools.partial(_rope_kernel, seq=seq),
        out_shape=[shape, shape],
        scratch_shapes=[pltpu.VMEM((GRID_W, D_AT), F32), pltpu.VMEM((GRID_W, D_AT), F32)],
        compiler_params=pltpu.CompilerParams(vmem_limit_bytes=VMEM_LIMIT_BYTES),
        name="rope_tables",
    )()


def _nt_dot(a, b):
    return lax.dot_general(a, b, (((1,), (1,)), ((), ())), preferred_element_type=F32)


ONES_ROWS = 8


def _diff_lambda(lq_ref, lk_ref, lam_init):
    lqk = lq_ref[...] * lk_ref[...]
    return (jnp.exp(jnp.sum(lqk[0:1, :], axis=-1, keepdims=True))
            - jnp.exp(jnp.sum(lqk[1:2, :], axis=-1, keepdims=True)) + lam_init)


def _attn_ctx_kernel(q_ref, k_ref, v_ref, z_ref, lq_ref, lk_ref, sg_ref, o_ref, vt_ref, *, lam_init):
    seq = k_ref.shape[0]
    vt_ref[0:D_AT, :] = v_ref[...].astype(F32).T.astype(BF16)
    vt_ref[D_AT:D_AT + ONES_ROWS, :] = jnp.ones((ONES_ROWS, seq), BF16)
    lam = _diff_lambda(lq_ref, lk_ref, lam_init)
    ones = vt_ref[D_AT:D_AT + ONES_ROWS, :]

    for h in range(AT_HEADS):
        hv = slice(h * AT_DV, (h + 1) * AT_DV)

        def unnormalised(c):
            cols = slice(h * AT_DV + c * AT_DH, h * AT_DV + (c + 1) * AT_DH)
            s = _nt_dot(k_ref[:, cols], q_ref[:, cols])
            e = jnp.exp2(s - jnp.max(s, axis=0, keepdims=True)).astype(BF16)
            o = jnp.dot(vt_ref[hv, :], e, preferred_element_type=F32)
            den = jnp.dot(ones, e, preferred_element_type=F32)
            return o, den[0:1, :]

        o1, den1 = unnormalised(0)
        o2, den2 = unnormalised(1)
        o = o1 * (1.0 / den1) - o2 * (lam / den2)
        o = o * lax.rsqrt(jnp.mean(o * o, axis=0, keepdims=True) + NORM_EPS)
        o = o.T * sg_ref[...] * (1.0 - lam_init)
        o_ref[:, hv] = (o * z_ref[:, hv].astype(F32)).astype(o_ref.dtype)


SCORE_BOUND = 60.0


def _score_bound_kernel(gq_ref, gk_ref, ck_ref, o_ref):
    layer, b = pl.program_id(0), pl.program_id(1)
    ck = ck_ref[...]
    cached = jnp.dot((ck * ck).astype(BF16), _group_mean_matrix(), preferred_element_type=F32) * AT_DH
    gq = gq_ref[pl.ds(layer, 1), :]
    gk = gk_ref[pl.ds(layer, 1), :]
    k_sq = jnp.maximum(jnp.max(cached), AT_DH * jnp.max(gk * gk))
    q_sq = AT_DH * jnp.max(gq * gq) * (Q_SCALE * Q_SCALE)
    o_ref[layer, b] = 1.05 * q_sq * k_sq


def _score_bounds(q_norm_g, k_norm_g, ck):
    batch, depth, past, _ = ck.shape
    return pl.pallas_call(
        _score_bound_kernel,
        grid=(depth, batch),
        in_specs=[pl.BlockSpec(q_norm_g.shape, lambda l, b: (0, 0)), pl.BlockSpec(k_norm_g.shape, lambda l, b: (0, 0)),
                  pl.BlockSpec((None, None, past, D_AT), lambda l, b: (b, l, 0, 0))],
        out_specs=pl.BlockSpec(memory_space=pltpu.SMEM),
        out_shape=jax.ShapeDtypeStruct((depth, batch), F32),
        compiler_params=_params("arbitrary", "arbitrary"),
        name="score_bounds",
    )(q_norm_g, k_norm_g, ck)


def _attn_latent_kernel(q_ref, k_ref, v_ref, z_ref, lq_ref, lk_ref, sg_ref, bound_ref, ck_ref, cv_ref,
                        o_ref, *, lam_init, layer):
    lam = _diff_lambda(lq_ref, lk_ref, lam_init)
    small_scores = bound_ref[layer, pl.program_id(0)] <= SCORE_BOUND * SCORE_BOUND

    def scores(h, c):
        cols = slice(h * AT_DV + c * AT_DH, h * AT_DV + (c + 1) * AT_DH)
        q = q_ref[:, cols]
        return _nt_dot(q, k_ref[:, cols]), _nt_dot(q, ck_ref[:, cols].astype(BF16))

    def shifted_parts(s, sc):
        m = jnp.maximum(jnp.max(s, axis=-1, keepdims=True), jnp.max(sc, axis=-1, keepdims=True))
        e = jnp.exp2(s - m)
        ec = jnp.exp2(sc - m)
        return e, ec, jnp.sum(e, axis=-1, keepdims=True) + jnp.sum(ec, axis=-1, keepdims=True)

    def raw_parts(s, sc):
        e = jnp.exp2(s)
        ec = jnp.exp2(sc)
        return e, ec, jnp.sum(e, axis=-1, keepdims=True) + jnp.sum(ec, axis=-1, keepdims=True)

    def finish(h, part1, part2):
        e1, ec1, den1 = part1
        e2, ec2, den2 = part2
        ratio = lam * den1 / den2
        hv = slice(h * AT_DV, (h + 1) * AT_DV)
        o = (jnp.dot((e1 - e2 * ratio).astype(BF16), v_ref[:, hv], preferred_element_type=F32)
             + jnp.dot((ec1 - ec2 * ratio).astype(BF16), cv_ref[:, hv].astype(BF16),
                       preferred_element_type=F32)) * (1.0 / den1)
        o = o * lax.rsqrt(jnp.mean(o * o, axis=-1, keepdims=True) + NORM_EPS) * sg_ref[...]
        o_ref[:, hv] = (o * (1.0 - lam_init) * z_ref[:, hv].astype(F32)).astype(o_ref.dtype)

    def attend(softmax_parts):
        units = [(h, c) for h in range(AT_HEADS) for c in range(2)]
        ahead = [scores(*units[0]), scores(*units[1])]
        parts = [None, None]
        for i, (h, c) in enumerate(units):
            current = ahead.pop(0)
            if i + 2 < len(units):
                ahead.append(scores(*units[i + 2]))
            parts[c] = softmax_parts(*current)
            if c == 1:
                finish(h, parts[0], parts[1])

    pl.when(small_scores)(lambda: attend(raw_parts))
    pl.when(jnp.logical_not(small_scores))(lambda: attend(shifted_parts))


def _attention(act, batch, seq, lam_q, lam_k, subln_g, lam_init, ctx, layer):
    t = batch * seq
    tq = 256 if ctx is None else 512
    nq = seq // tq
    small = lambda a: pl.BlockSpec(a.shape, lambda b, i: (0,) * a.ndim)
    in_specs = [pl.BlockSpec((tq, CB), lambda b, i: (b * nq + i, COL_Q)),
                pl.BlockSpec((seq, CB), lambda b, i: (b, COL_K)),
                pl.BlockSpec((seq, CB), lambda b, i: (b, COL_V)),
                pl.BlockSpec((tq, CB), lambda b, i: (b * nq + i, COL_AT_Z)),
                small(lam_q), small(lam_k), small(subln_g)]
    args = [act, act, act, act, lam_q, lam_k, subln_g]
    if ctx is None:
        assert nq == 1
        body = functools.partial(_attn_ctx_kernel, lam_init=lam_init)
        scratch = [pltpu.VMEM((D_AT + ONES_ROWS, seq), BF16)]
    else:
        ck, cv, bounds = ctx
        cspec = pl.BlockSpec((None, None, ck.shape[2], D_AT), lambda b, i: (b, layer, 0, 0))
        in_specs += [pl.BlockSpec(memory_space=pltpu.SMEM), cspec, cspec]
        args += [bounds, ck, cv]
        body = functools.partial(_attn_latent_kernel, lam_init=lam_init, layer=layer)
        scratch = []
    return pl.pallas_call(
        body,
        grid=(batch, nq),
        in_specs=in_specs,
        out_specs=pl.BlockSpec((tq, D_AT), lambda b, i: (b * nq + i, 0)),
        out_shape=jax.ShapeDtypeStruct((t, D_AT), BF16),
        scratch_shapes=scratch,
        compiler_params=_params("parallel", "parallel"),
        name=f"diff_attn_{seq}",
    )(*args)


def _pool_kernel(u_ref, z_ref, w_ref, sc_ref, o_ref, *, seq):
    padded = seq + 2 * POOL_PAD
    t = lax.broadcasted_iota(jnp.int32, (seq, 1), 0)
    zeros = jnp.zeros((POOL_PAD, POOL_GC), F32)
    for g, win in enumerate(POOL_WINDOWS):
        cols = slice(g * POOL_GC, (g + 1) * POOL_GC)
        u = u_ref[:, cols].astype(F32)
        s = jnp.concatenate([zeros, u, zeros], axis=0)
        s = s + pltpu.roll(s, 1, 0)
        half = 1
        while 2 * half < win:
            s = pltpu.roll(s, padded - half, 0) + pltpu.roll(s, half, 0)
            half *= 2
        lo = jnp.maximum(t - win // 2, 0)
        hi = jnp.minimum(t + win // 2, seq)
        m = s[POOL_PAD:POOL_PAD + seq] / (hi - lo).astype(F32) - u
        y = jnp.dot(m.astype(BF16), w_ref[g].astype(BF16), preferred_element_type=F32)
        o_ref[:, cols] = (y * sc_ref[:, cols] * z_ref[:, cols].astype(F32)).astype(o_ref.dtype)


def _pool_branch(proj, batch, seq, pool_w, pool_scale):
    t = batch * seq
    return pl.pallas_call(
        functools.partial(_pool_kernel, seq=seq),
        grid=(batch,),
        in_specs=[pl.BlockSpec((seq, CB), lambda b: (b, COL_PL_U)),
                  pl.BlockSpec((seq, CB), lambda b: (b, COL_PL_Z)),
                  pl.BlockSpec(pool_w.shape, lambda b: (0, 0, 0)),
                  pl.BlockSpec((1, D_PL), lambda b: (0, 0))],
        out_specs=pl.BlockSpec((seq, D_PL), lambda b: (b, 0)),
        out_shape=jax.ShapeDtypeStruct((t, D_PL), BF16),
        compiler_params=_params("parallel"),
        name=f"pool_{seq}",
    )(proj, proj, pool_w, pool_scale)


def _merge_kernel(x_ref, mod_ref, yh_ref, ya_ref, yp_ref, g0_ref, g1_ref, g2_ref,
                  wh_ref, wa_ref, wp_ref, wo_ref, o_ref):
    merged = (g0_ref[...].astype(F32) * jnp.dot(yh_ref[...], wh_ref[...], preferred_element_type=F32)
              + g1_ref[...].astype(F32) * jnp.dot(ya_ref[...], wa_ref[...], preferred_element_type=F32)
              + g2_ref[...].astype(F32) * jnp.dot(yp_ref[...], wp_ref[...], preferred_element_type=F32))
    out = jnp.dot(merged.astype(BF16), wo_ref[...], preferred_element_type=F32)
    o_ref[...] = x_ref[...] + mod_ref[:, 2 * D_MODEL:3 * D_MODEL] * out


def _merge(x, mod, mod_row0, rows_per_seq, act, y_hy, y_at, y_pl, w_hy_o, w_at_o, w_pl_o, w_out, tm=1024):
    t = x.shape[0]
    blocks_per_seq = rows_per_seq // tm
    d = D_MODEL
    branch = pl.BlockSpec((tm, CB), lambda i: (i, 0))
    gate = lambda g: pl.BlockSpec((tm, d), lambda i: (i, COL_MG // 2 + g))
    return pl.pallas_call(
        _merge_kernel,
        grid=(t // tm,),
        in_specs=[pl.BlockSpec((tm, d), lambda i: (i, 0)),
                  pl.BlockSpec((None, 1, 3 * d), lambda i: (mod_row0 + i // blocks_per_seq, 0, 0)),
                  branch, branch, branch, gate(0), gate(1), gate(2),
                  _const_spec((CB, d)), _const_spec((CB, d)), _const_spec((CB, d)), _const_spec((d, d))],
        out_specs=pl.BlockSpec((tm, d), lambda i: (i, 0)),
        out_shape=jax.ShapeDtypeStruct((t, d), F32),
        compiler_params=_params("parallel"),
        name="merge",
    )(x, mod, y_hy, y_at, y_pl, act, act, act, w_hy_o, w_at_o, w_pl_o, w_out)


def _trunk_layer(x, batch, seq, mod, mod_row0, p, filters, dft, rope, ctx, layer, lam_init, kv_prev=None):
    mod_rows = seq if mod_row0 else batch * seq
    act, kv = _in_proj(x, mod, mod_row0, mod_rows, p["norm_g"], p["w_in"], layer, p["gq"], p["gk"], rope, seq,
                       keep_kv=rope is None, kv_prev=kv_prev)
    y_hy = _hyena_branch(act, batch, seq, p["hy_conv_w"], p["hy_conv_b"], p["hy_bias"], filters, dft)
    y_at = _attention(act, batch, seq, p["lam_q"], p["lam_k"], p["subln_g"], lam_init, ctx, layer)
    y_pl = _pool_branch(act, batch, seq, p["pool_w"], p["pool_scale"])
    out = _merge(x, mod, mod_row0, mod_rows, act, y_hy, y_at, y_pl,
                 p["w_hy_o"], p["w_at_o"], p["w_pl_o"], p["w_out"])
    return out, kv


def kernel(x_prompt, x_sample, cache_k, cache_v, c, c_ctx, norm_g, w_ada, b_ada, w_in, hy_conv_w, hy_conv_b,
           hy_f_w1, hy_f_b1, hy_f_w2, hy_f_b2, hy_f_w3, hy_freq, hy_bias, q_norm_g, k_norm_g, lam_q, lam_k,
           subln_g, pool_w, pool_scale, w_hy_o, w_at_o, w_pl_o, w_out):
    batch, seq, d = x_prompt.shape
    dec_batch, dec_seq, _ = x_sample.shape
    past = cache_k.shape[2]
    assert d == D_MODEL and dec_batch + 1 <= COND_ROWS and dec_seq % GRID_W == 0

    cond = jnp.concatenate([c_ctx[None, :], c, jnp.zeros((COND_ROWS - 1 - dec_batch, d), F32)], axis=0)
    mod = _modulation(cond, w_ada, b_ada).reshape(DEPTH, COND_ROWS, 1, 3 * d)
    dft = {s: _dft_tables(s) for s in (seq, dec_seq)}
    rope = _rope_tables(dec_seq)
    ctx_k = cache_k.reshape(dec_batch, DEPTH, past, D_AT)
    ctx = (ctx_k, cache_v.reshape(dec_batch, DEPTH, past, D_AT), _score_bounds(q_norm_g, k_norm_g, ctx_k))

    w_in_bf16 = w_in.astype(BF16)
    yp = x_prompt.reshape(batch * seq, d)
    ys = x_sample.reshape(dec_batch * dec_seq, d)
    kv = None
    for l in range(DEPTH):
        p = dict(norm_g=norm_g[l][None, :], w_in=w_in_bf16,
                 hy_conv_w=hy_conv_w[l], hy_conv_b=hy_conv_b[l], hy_bias=hy_bias[l],
                 gq=jnp.tile(q_norm_g[l], D_AT // AT_DH)[None, :], gk=jnp.tile(k_norm_g[l], D_AT // AT_DH)[None, :],
                 lam_q=lam_q[l], lam_k=lam_k[l], subln_g=subln_g[l][None, :],
                 pool_w=pool_w[l], pool_scale=pool_scale[l][None, :],
                 w_hy_o=w_hy_o[l].astype(BF16), w_at_o=w_at_o[l].astype(BF16),
                 w_pl_o=w_pl_o[l].astype(BF16), w_out=w_out[l].astype(BF16))
        filters = {s: _hyena_filters(s, hy_f_w1[l], hy_f_b1[l], hy_f_w2[l], hy_f_b2[l], hy_f_w3[l], hy_freq[l],
                                     dft[s]) for s in (seq, dec_seq)}
        lam_init = 0.8 - 0.6 * math.exp(-0.3 * l)
        yp, kv = _trunk_layer(yp, batch, seq, mod[l], 0, p, filters[seq], dft[seq], None, None, l, lam_init, kv)
        ys, _ = _trunk_layer(ys, dec_batch, dec_seq, mod[l], 1, p, filters[dec_seq], dft[dec_seq], rope, ctx, l,
                             lam_init)
    return (yp.reshape(batch, seq, d), ys.reshape(dec_batch, dec_seq, d),
            kv[0].reshape(batch, DEPTH, seq, AT_HEADS, 2, AT_DH), kv[1].reshape(batch, DEPTH, seq, AT_HEADS, AT_DV))
```

```python
import functools
import math

import jax
import jax.numpy as jnp
from jax import lax
from jax.experimental import pallas as pl
from jax.experimental.pallas import tpu as pltpu

F32 = jnp.float32
BF16 = jnp.bfloat16
HIGHEST = lax.Precision.HIGHEST

D_MODEL = 1024
DEPTH = 2
GRID_W = 64
GRID_W_LOG2 = GRID_W.bit_length() - 1
assert GRID_W == 1 << GRID_W_LOG2
D_HY = 512
HY_ORDER = 2
HY_EMB = 33
HY_BANDS = (HY_EMB - 1) // 2
HY_FFN = 64
HY_FAST_DECAY = 0.3
HY_SLOW_DECAY = 1.5
HY_TARGET = 1e-2
AT_HEADS = 4
AT_DH = 64
AT_DV = 2 * AT_DH
D_AT = AT_HEADS * AT_DV
ROPE_BASE = 10000.0
Q_SCALE = AT_DH ** -0.5 * math.log2(math.e)
D_PL = 512
POOL_WINDOWS = (2, 4, 8, 16)
POOL_GC = D_PL // 4
POOL_PAD = 16
POOL_BLOCK_ROWS = 2048
CTX_ATTN_BLOCK_ROWS = 1024
NORM_EPS = 1e-6
D_IN = 8192
COND_ROWS = 8

CB = 512
COL_HY_X1, COL_HY_X2, COL_HY_V, COL_HY_Z = 0, 1, 2, 3
COL_Q, COL_K, COL_V, COL_AT_Z, COL_PL_U, COL_PL_Z = 4, 5, 6, 7, 8, 9
COL_MG = 10

VMEM_LIMIT_BYTES = 56 * 1024 * 1024
LANES = 128


def _params(*sem):
    return pltpu.CompilerParams(dimension_semantics=sem, vmem_limit_bytes=VMEM_LIMIT_BYTES)


def _const_spec(shape):
    return pl.BlockSpec(shape, lambda *_: (0,) * len(shape), pipeline_mode=pl.Buffered(1))


def _silu(x):
    return x * jax.nn.sigmoid(x)


def _bdot(a, b):
    return jnp.dot(a.astype(BF16), b.astype(BF16), preferred_element_type=F32)


def _fdot(a, b):
    return jnp.dot(a, b, precision=HIGHEST, preferred_element_type=F32)


def _mod_kernel(cond_ref, w_ref, b_ref, o_ref):
    o_ref[...] = _fdot(_silu(cond_ref[...]), w_ref[...]) + b_ref[...]


def _modulation(cond, w_ada, b_ada):
    d = D_MODEL
    return pl.pallas_call(
        _mod_kernel,
        grid=(DEPTH, 3),
        in_specs=[
            pl.BlockSpec((COND_ROWS, d), lambda l, n: (0, 0)),
            pl.BlockSpec((None, d, d), lambda l, n: (l, 0, n)),
            pl.BlockSpec((None, 1, d), lambda l, n: (l, 0, n)),
        ],
        out_specs=pl.BlockSpec((None, COND_ROWS, d), lambda l, n: (l, 0, n)),
        out_shape=jax.ShapeDtypeStruct((DEPTH, COND_ROWS, 3 * d), F32),
        compiler_params=_params("parallel", "parallel"),
        name="modulation",
    )(cond, w_ada, b_ada.reshape(DEPTH, 1, 3 * d))


def _group_mean_matrix():
    r = lax.broadcasted_iota(jnp.int32, (D_AT, D_AT), 0) // AT_DH
    c = lax.broadcasted_iota(jnp.int32, (D_AT, D_AT), 1) // AT_DH
    return jnp.where(r == c, 1.0 / AT_DH, 0.0).astype(BF16)


def _group_rms_norm(x, g, mean_mat):
    ms = jnp.dot((x * x).astype(BF16), mean_mat, preferred_element_type=F32)
    return x * lax.rsqrt(ms + NORM_EPS) * g


def _rotate(x, cos, sin_signed):
    lane = lax.broadcasted_iota(jnp.int32, (1, D_AT), 1)
    partner = jnp.where((lane & 1) == 0, pltpu.roll(x, D_AT - 1, 1), pltpu.roll(x, 1, 1))
    return x * cos + partner * sin_signed


def _in_kernel(*refs, rope, kv_slots, seq):
    x_ref, mod_ref, g_ref, w_ref, gq_ref, gk_ref = refs[:6]
    refs = refs[6:]
    if rope:
        cos_ref, sin_ref = refs[:2]
        refs = refs[2:]
    if kv_slots == "own":
        refs = refs[2:]
    o_ref = refs[0]
    h_ref = refs[-1]

    x = x_ref[...]
    y = x * lax.rsqrt(jnp.mean(x * x, axis=-1, keepdims=True) + NORM_EPS) * g_ref[...]
    h_ref[...] = (y * (1.0 + mod_ref[:, D_MODEL:2 * D_MODEL]) + mod_ref[:, 0:D_MODEL]).astype(BF16)

    def chunk(c):
        return jnp.dot(h_ref[...], w_ref[:, c * CB:(c + 1) * CB], preferred_element_type=F32)

    def put(c, val):
        o_ref[:, c * CB:(c + 1) * CB] = val.astype(o_ref.dtype)

    for c in (COL_HY_X1, COL_HY_X2, COL_HY_V, COL_PL_U):
        put(c, chunk(c))
    for c in (COL_HY_Z, COL_AT_Z, COL_PL_Z):
        put(c, _silu(chunk(c)))
    for c in range(COL_MG, D_IN // CB):
        put(c, jax.nn.sigmoid(chunk(c)))

    mean_mat = _group_mean_matrix()
    q = _group_rms_norm(chunk(COL_Q), gq_ref[...], mean_mat)
    k = _group_rms_norm(chunk(COL_K), gk_ref[...], mean_mat)
    v = chunk(COL_V)
    for dst_ref, val in ((refs[1], k), (refs[2], v)) if kv_slots else ():
        for s in range(val.shape[0] // seq):
            rows = val[s * seq:(s + 1) * seq]
            if kv_slots == "all":
                dst_ref[s, 0] = rows
                dst_ref[s, 1:] = jnp.zeros((dst_ref.shape[1] - 1,) + rows.shape, F32)
            else:
                dst_ref[s] = rows
    if rope:
        q = _rotate(q, cos_ref[...], sin_ref[...])
        k = _rotate(k, cos_ref[...], sin_ref[...])
    put(COL_Q, q * Q_SCALE)
    put(COL_K, k)
    put(COL_V, v)


def _in_proj(x, mod, mod_row0, rows_per_seq, norm_g, w_in_bf16, layer, gq, gk, rope, seq, keep_kv, kv_prev,
             tm=512):
    t = x.shape[0]
    blocks_per_seq = rows_per_seq // tm
    vec = pl.BlockSpec((1, D_AT), lambda i: (0, 0))
    in_specs = [pl.BlockSpec((tm, D_MODEL), lambda i: (i, 0)),
                pl.BlockSpec((None, 1, 3 * D_MODEL), lambda i: (mod_row0 + i // blocks_per_seq, 0, 0)),
                pl.BlockSpec((1, D_MODEL), lambda i: (0, 0)),
                pl.BlockSpec((None, D_MODEL, D_IN), lambda i: (layer, 0, 0), pipeline_mode=pl.Buffered(1)),
                vec, vec]
    args = [x, mod, norm_g, w_in_bf16, gq, gk]
    if rope is not None:
        per_seq = seq // tm
        tab = pl.BlockSpec((tm, D_AT), lambda i: (i % per_seq, 0))
        in_specs += [tab, tab]
        args += list(rope)
    out_specs = [pl.BlockSpec((tm, D_IN), lambda i: (i, 0))]
    out_shape = [jax.ShapeDtypeStruct((t, D_IN), BF16)]
    kv_slots, aliases = None, {}
    if keep_kv:
        spb = tm // seq
        shape = jax.ShapeDtypeStruct((t // seq, DEPTH, seq, D_AT), F32)
        out_shape += [shape, shape]
        if kv_prev is None:
            assert layer == 0
            kv_slots = "all"
            out_specs += [pl.BlockSpec((spb, DEPTH, seq, D_AT), lambda i: (i, 0, 0, 0))] * 2
        else:
            kv_slots = "own"
            out_specs += [pl.BlockSpec((spb, None, seq, D_AT), lambda i: (i, layer, 0, 0))] * 2
            aliases = {len(args): 1, len(args) + 1: 2}
            in_specs += [pl.BlockSpec(memory_space=pl.ANY)] * 2
            args += list(kv_prev)
    out = pl.pallas_call(
        functools.partial(_in_kernel, rope=rope is not None, kv_slots=kv_slots, seq=seq),
        grid=(t // tm,),
        in_specs=in_specs,
        out_specs=out_specs,
        out_shape=out_shape,
        input_output_aliases=aliases,
        scratch_shapes=[pltpu.VMEM((tm, D_MODEL), BF16)],
        compiler_params=_params("parallel"),
        name="in_proj_latent" if rope is not None else "in_proj_ctx",
    )(*args)
    return (out[0], (out[1], out[2])) if keep_kv else (out[0], None)


TABLE_ROWS = 32


def _dft_kernel(ce_ref, se_ref, co_ref, so_ref, cot_ref, sot_ref, bc_ref, bs_ref, oc_ref, os_ref, *, seq):
    half = seq // 2
    col = lax.broadcasted_iota(jnp.int32, (1, half), 1)
    base = lax.broadcasted_iota(jnp.int32, (half // TABLE_ROWS, 1), 0) * TABLE_ROWS
    off = lax.broadcasted_iota(jnp.int32, (TABLE_ROWS, 1), 0)

    def trig(c_ref, s_ref, slot, product):
        ang = (product & (2 * seq - 1)).astype(F32) * (math.pi / seq)
        c_ref[slot] = jnp.cos(ang)
        s_ref[slot] = jnp.sin(ang)

    trig(bc_ref, bs_ref, 0, 2 * base * col)
    trig(bc_ref, bs_ref, 1, base * (2 * col + 1))
    trig(bc_ref, bs_ref, 2, (2 * base + 1) * col)
    trig(oc_ref, os_ref, 0, 2 * off * col)
    trig(oc_ref, os_ref, 1, off * (2 * col + 1))
    outs = ((ce_ref, se_ref, 0, 0), (co_ref, so_ref, 1, 1), (cot_ref, sot_ref, 2, 0))

    def block(i, carry):
        rows = pl.ds(pl.multiple_of(i * TABLE_ROWS, TABLE_ROWS), TABLE_ROWS)
        for c_ref, s_ref, b, o in outs:
            bc = bc_ref[b, pl.ds(i, 1), :]
            bs = bs_ref[b, pl.ds(i, 1), :]
            oc = oc_ref[o]
            osn = os_ref[o]
            c_ref[rows, :] = (bc * oc - bs * osn).astype(BF16)
            s_ref[rows, :] = (bs * oc + bc * osn).astype(BF16)
        return carry

    lax.fori_loop(0, half // TABLE_ROWS, block, 0)


def _dft_tables(seq):
    half = seq // 2
    shape = jax.ShapeDtypeStruct((half, half), BF16)
    return pl.pallas_call(
        functools.partial(_dft_kernel, seq=seq),
        out_shape=[shape] * 6,
        scratch_shapes=[pltpu.VMEM((3, half // TABLE_ROWS, half), F32), pltpu.VMEM((3, half // TABLE_ROWS, half), F32),
                        pltpu.VMEM((2, TABLE_ROWS, half), F32), pltpu.VMEM((2, TABLE_ROWS, half), F32)],
        compiler_params=pltpu.CompilerParams(vmem_limit_bytes=VMEM_LIMIT_BYTES),
        name=f"dft_tables_{seq}",
    )()


FILTER_CB = 256


def _filter_kernel(w1t_ref, w1c_ref, w1s_ref, b1_ref, w2_ref, b2_ref, fq_ref, w3f_ref, w3b_ref,
                   ce_ref, se_ref, co_ref, so_ref, alo_ref, blo_ref, ahi_ref, bhi_ref, mid_ref, h_ref, *, seq):
    half = seq // 2
    row = lax.broadcasted_iota(jnp.int32, (half, 1), 0)

    @pl.when(pl.program_id(0) == 0)
    def _():
        band = lax.broadcasted_iota(jnp.int32, (HY_BANDS, 1), 0).astype(F32)
        f = 1e-4 + band * ((HY_BANDS - 1 - 1e-4) / (HY_BANDS - 1))
        for p in range(2):
            pos = (2 * lax.broadcasted_iota(jnp.int32, (1, half), 1) + p).astype(F32)
            fw = f * (pos * (2.0 * math.pi / seq))
            pre = (w1t_ref[...] * (pos * (1.0 / (seq - 1))) + _fdot(w1c_ref[...], jnp.cos(fw))
                   - _fdot(w1s_ref[...], jnp.sin(fw)) + b1_ref[...])
            h = jnp.sin(fq_ref[:, 0:1] * pre)
            h = jnp.sin(fq_ref[:, 1:2] * (_fdot(w2_ref[...], h) + b2_ref[...]))
            h_ref[p] = h.T

    chan = lax.broadcasted_iota(jnp.int32, (1, FILTER_CB), 1) + (pl.program_id(0) % (D_HY // FILTER_CB)) * FILTER_CB
    max_decay = math.log(HY_TARGET) / HY_FAST_DECAY
    min_decay = math.log(HY_TARGET) / HY_SLOW_DECAY
    deltas = jnp.abs(min_decay + chan.astype(F32) * ((max_decay - min_decay) / (D_HY - 1)))

    def taps(p):
        decay = jnp.exp((2 * row + p).astype(F32) * (-1.0 / (seq - 1)) * deltas)
        hf = _fdot(h_ref[p], w3f_ref[...]) * decay
        hb = _fdot(h_ref[p], w3b_ref[...]) * decay
        if p == 0:
            hb = jnp.where(row == 0, 0.0, hb)
        return hf, hb

    hf0, hb0 = taps(0)
    hf1, hb1 = taps(1)
    inv = 1.0 / (jnp.sum(jnp.abs(hf0) + jnp.abs(hf1) + jnp.abs(hb0) + jnp.abs(hb1), axis=0, keepdims=True) + NORM_EPS)
    hs0, hs1 = (hf0 + hb0) * inv, (hf1 + hb1) * inv
    hd0, hd1 = (hf0 - hb0) * inv, (hf1 - hb1) * inv
    c_even = _bdot(ce_ref[...], hs0)
    c_odd = _bdot(co_ref[...], hs1)
    s_even = _bdot(se_ref[...], hd0)
    s_odd = _bdot(so_ref[...], hd1)
    wk = jnp.where(row == 0, 0.5 / seq, 1.0 / seq)
    alo_ref[...] = wk * (c_even + c_odd)
    ahi_ref[...] = wk * (c_even - c_odd)
    blo_ref[...] = -wk * (s_even + s_odd)
    bhi_ref[...] = wk * (s_even - s_odd)
    alt = (1 - 2 * (row & 1)).astype(F32)
    mid_ref[0:1, :] = jnp.sum(alt * hs0, axis=0, keepdims=True) * (1.0 / seq)
    mid_ref[1:2, :] = jnp.sum(alt * hd1, axis=0, keepdims=True) * (-1.0 / seq)


def _hyena_filters(seq, w1, b1, w2, b2, w3, freq, tables):
    nc = HY_ORDER * D_HY
    nblk = nc // FILTER_CB
    half = seq // 2
    small = lambda a: pl.BlockSpec(a.shape, lambda j: (0,) * a.ndim)
    w1t, w1c, w1s = w1[0:1].T, w1[1:1 + HY_BANDS].T, w1[1 + HY_BANDS:].T
    b1 = b1.reshape(HY_FFN, 1)
    b2 = b2.reshape(HY_FFN, 1)
    w2 = w2.T
    freq = freq.T
    out_blk = pl.BlockSpec((half, FILTER_CB), lambda j: (0, j))
    spec_shape = jax.ShapeDtypeStruct((half, nc), F32)
    return pl.pallas_call(
        functools.partial(_filter_kernel, seq=seq),
        grid=(nblk,),
        in_specs=[small(w1t), small(w1c), small(w1s), small(b1), small(w2), small(b2), small(freq),
                  pl.BlockSpec((HY_FFN, FILTER_CB), lambda j: (0, j)),
                  pl.BlockSpec((HY_FFN, FILTER_CB), lambda j: (0, nblk + j))] + [_const_spec((half, half))] * 4,
        out_specs=[out_blk] * 4 + [pl.BlockSpec((2, FILTER_CB), lambda j: (0, j))],
        out_shape=[spec_shape] * 4 + [jax.ShapeDtypeStruct((2, nc), F32)],
        scratch_shapes=[pltpu.VMEM((2, half, HY_FFN), F32)],
        compiler_params=_params("arbitrary"),
        name=f"hyena_filters_{seq}",
    )(w1t, w1c, w1s, b1, w2, b2, freq, w3, w3, *tables[:4])


def _hyena_kernel(*refs, seq, rb):
    x1_ref, x2_ref, v_ref, z_ref = refs[0:4]
    conv_w, conv_c = refs[4:7], refs[7:10]
    spectra = (refs[10:16], refs[16:22])
    ce_ref, se_ref, co_ref, so_ref, cot_ref, sot_ref = refs[22:28]
    o_ref, u_ref, g_ref, ub_ref, pq_ref = refs[28:33]
    half = seq // 2
    slabs = u_ref.shape[0]
    pos = lax.broadcasted_iota(jnp.int32, (seq, 1), 0)
    row = lax.broadcasted_iota(jnp.int32, (half, 1), 0)
    alt = (1 - 2 * (row & 1)).astype(F32)

    def short_conv(x_ref, w_ref, c_ref):
        x = x_ref[...].astype(F32)
        prev = jnp.where(pos == 0, 0.0, pltpu.roll(x, 1, 0))
        nxt = jnp.where(pos == seq - 1, 0.0, pltpu.roll(x, seq - 1, 0))
        return prev * w_ref[0:1, :] + x * w_ref[1:2, :] + nxt * w_ref[2:3, :] + c_ref[...]

    def put_u(rows, val):
        for s in range(slabs):
            u_ref[s, rows, :] = val[:, s * LANES:(s + 1) * LANES]

    def get_u(rows):
        return jnp.concatenate([u_ref[s, rows, :] for s in range(slabs)], axis=1)

    def long_conv(alo_ref, blo_ref, ahi_ref, bhi_ref, mid_ref, skip_ref, finish):
        for p in range(2):
            ub_ref[p] = get_u(pl.ds(p, half, stride=2)).astype(BF16)
        c_mid = jnp.sum(get_u(pl.ds(0, half, stride=2)) * alt, axis=0, keepdims=True)
        s_mid = jnp.sum(get_u(pl.ds(1, half, stride=2)) * alt, axis=0, keepdims=True)
        p_mid = c_mid * mid_ref[0:1, :] + s_mid * mid_ref[1:2, :]
        q_mid = s_mid * mid_ref[0:1, :] - c_mid * mid_ref[1:2, :]
        for r in range(0, half, rb):
            rows = pl.ds(r, rb)
            c_even = jnp.dot(ce_ref[rows, :], ub_ref[0], preferred_element_type=F32)
            c_odd = jnp.dot(co_ref[rows, :], ub_ref[1], preferred_element_type=F32)
            s_even = jnp.dot(se_ref[rows, :], ub_ref[0], preferred_element_type=F32)
            s_odd = jnp.dot(so_ref[rows, :], ub_ref[1], preferred_element_type=F32)
            c_lo, c_hi = c_even + c_odd, c_even - c_odd
            s_lo, s_hi = s_even + s_odd, s_odd - s_even
            alo, blo, ahi, bhi = alo_ref[rows, :], blo_ref[rows, :], ahi_ref[rows, :], bhi_ref[rows, :]
            p_lo, q_lo = c_lo * alo + s_lo * blo, s_lo * alo - c_lo * blo
            p_hi, q_hi = c_hi * ahi + s_hi * bhi, s_hi * ahi - c_hi * bhi
            pq_ref[0, rows, :] = (p_lo + p_hi).astype(BF16)
            pq_ref[1, rows, :] = (q_lo - q_hi).astype(BF16)
            pq_ref[2, rows, :] = (p_lo - p_hi).astype(BF16)
            pq_ref[3, rows, :] = (q_lo + q_hi).astype(BF16)
        for r in range(0, half, rb):
            rows = pl.ds(r, rb)
            even, odd = pl.ds(2 * r, rb, stride=2), pl.ds(2 * r + 1, rb, stride=2)
            ye = (jnp.dot(ce_ref[rows, :], pq_ref[0], preferred_element_type=F32)
                  + jnp.dot(se_ref[rows, :], pq_ref[1], preferred_element_type=F32)
                  + alt[r:r + rb] * p_mid + get_u(even) * skip_ref[...])
            yo = (jnp.dot(cot_ref[rows, :], pq_ref[2], preferred_element_type=F32)
                  + jnp.dot(sot_ref[rows, :], pq_ref[3], preferred_element_type=F32)
                  + alt[r:r + rb] * q_mid + get_u(odd) * skip_ref[...])
            put_u(even, ye)
            put_u(odd, yo)
            both = pl.ds(2 * r, 2 * rb)
            finish(both, g_ref[both, :] * get_u(both))

    def keep(rows, y):
        put_u(rows, y)

    def emit(rows, y):
        o_ref[rows, :] = (y * z_ref[rows, :].astype(F32)).astype(o_ref.dtype)

    put_u(pl.ds(0, seq), short_conv(v_ref, conv_w[2], conv_c[2]))
    g_ref[...] = short_conv(x1_ref, conv_w[0], conv_c[0])
    long_conv(*spectra[0], keep)
    g_ref[...] = short_conv(x2_ref, conv_w[1], conv_c[1])
    long_conv(*spectra[1], emit)


def _hyena_branch(act, batch, seq, conv_w, conv_b, hy_bias, filters, tables):
    cb = 256 if seq > 512 else D_HY
    half = seq // 2
    rb = min(half, 512)
    nj = D_HY // cb
    t = batch * seq
    conv_b = conv_b.reshape(1, 3 * D_HY)
    skip = hy_bias.reshape(1, HY_ORDER * D_HY)
    alo, blo, ahi, bhi, mid = filters

    def act_spec(col512):
        off = col512 * (CB // cb)
        return pl.BlockSpec((seq, cb), lambda j, b: (b, off + j))

    def chan_spec(rows, part):
        off = part * nj
        return pl.BlockSpec((rows, cb), lambda j, b: (0, off + j), pipeline_mode=pl.Buffered(1))

    in_specs = [act_spec(COL_HY_X1), act_spec(COL_HY_X2), act_spec(COL_HY_V), act_spec(COL_HY_Z)]
    in_specs += [chan_spec(3, part) for part in range(3)] + [chan_spec(1, part) for part in range(3)]
    args = [act] * 4 + [conv_w] * 3 + [conv_b] * 3
    for order in range(HY_ORDER):
        in_specs += [chan_spec(half, order)] * 4 + [chan_spec(2, order), chan_spec(1, order)]
        args += [alo, blo, ahi, bhi, mid, skip]
    in_specs += [_const_spec((half, half))] * 6
    args += list(tables)
    return pl.pallas_call(
        functools.partial(_hyena_kernel, seq=seq, rb=rb),
        grid=(nj, batch),
        in_specs=in_specs,
        out_specs=pl.BlockSpec((seq, cb), lambda j, b: (b, j)),
        out_shape=jax.ShapeDtypeStruct((t, D_HY), BF16),
        scratch_shapes=[pltpu.VMEM((cb // LANES, seq, LANES), F32), pltpu.VMEM((seq, cb), F32),
                        pltpu.VMEM((2, half, cb), BF16), pltpu.VMEM((4, half, cb), BF16)],
        compiler_params=_params("parallel", "parallel"),
        name=f"hyena_{seq}",
    )(*args)


def _rope_kernel(cos_ref, sin_ref, c_ref, s_ref, *, seq):
    lane = lax.broadcasted_iota(jnp.int32, (1, D_AT), 1)
    dim = lane & (AT_DH - 1)
    pair = ((dim & (AT_DH // 2 - 1)) >> 1).astype(F32)
    inv = jnp.exp(pair * (-(2.0 / (AT_DH // 2)) * math.log(ROPE_BASE)))
    ang = lax.broadcasted_iota(jnp.int32, (GRID_W, 1), 0).astype(F32) * inv
    c_ref[...] = jnp.cos(ang)
    s_ref[...] = jnp.where((lane & 1) == 0, -jnp.sin(ang), jnp.sin(ang))
    by_row = dim < AT_DH // 2

    def grid_row(r, carry):
        rows = pl.ds(pl.multiple_of(r * GRID_W, GRID_W), GRID_W)
        cos_ref[rows, :] = jnp.where(by_row, c_ref[pl.ds(r, 1), :], c_ref[...])
        sin_ref[rows, :] = jnp.where(by_row, s_ref[pl.ds(r, 1), :], s_ref[...])
        return carry

    lax.fori_loop(0, seq // GRID_W, grid_row, 0)


def _rope_tables(seq):
    assert seq // GRID_W <= GRID_W
    shape = jax.ShapeDtypeStruct((seq, D_AT), F32)
    return pl.pallas_call(
        functools.partial(_rope_kernel, seq=seq),
        out_shape=[shape, shape],
        scratch_shapes=[pltpu.VMEM((GRID_W, D_AT), F32), pltpu.VMEM((GRID_W, D_AT), F32)],
        compiler_params=pltpu.CompilerParams(vmem_limit_bytes=VMEM_LIMIT_BYTES),
        name="rope_tables",
    )()


def _nt_dot(a, b):
    return lax.dot_general(a, b, (((1,), (1,)), ((), ())), preferred_element_type=F32)


ONES_ROWS = 8


def _diff_lambda(lq_ref, lk_ref, lam_init):
    lqk = lq_ref[...] * lk_ref[...]
    return (jnp.exp(jnp.sum(lqk[0:1, :], axis=-1, keepdims=True))
            - jnp.exp(jnp.sum(lqk[1:2, :], axis=-1, keepdims=True)) + lam_init)


def _attn_ctx_kernel(q_ref, k_ref, v_ref, z_ref, lq_ref, lk_ref, sg_ref, o_ref, vt_ref, *, lam_init, seq):
    lam = _diff_lambda(lq_ref, lk_ref, lam_init)
    for n in range(q_ref.shape[0] // seq):
        rows = slice(n * seq, (n + 1) * seq)
        vt_ref[n, 0:D_AT, :] = v_ref[rows, :].astype(F32).T.astype(BF16)
        vt_ref[n, D_AT:D_AT + ONES_ROWS, :] = jnp.ones((ONES_ROWS, seq), BF16)
        ones = vt_ref[n, D_AT:D_AT + ONES_ROWS, :]

        for h in range(AT_HEADS):
            hv = slice(h * AT_DV, (h + 1) * AT_DV)

            def unnormalised(c):
                cols = slice(h * AT_DV + c * AT_DH, h * AT_DV + (c + 1) * AT_DH)
                s = _nt_dot(k_ref[rows, cols], q_ref[rows, cols])
                e = jnp.exp2(s - jnp.max(s, axis=0, keepdims=True)).astype(BF16)
                o = jnp.dot(vt_ref[n, hv, :], e, preferred_element_type=F32)
                den = jnp.dot(ones, e, preferred_element_type=F32)
                return o, den[0:1, :]

            o1, den1 = unnormalised(0)
            o2, den2 = unnormalised(1)
            o = o1 * (1.0 / den1) - o2 * (lam / den2)
            o = o * lax.rsqrt(jnp.mean(o * o, axis=0, keepdims=True) + NORM_EPS)
            o = o.T * sg_ref[...] * (1.0 - lam_init)
            o_ref[rows, hv] = (o * z_ref[rows, hv].astype(F32)).astype(o_ref.dtype)


SCORE_BOUND = 60.0


def _score_bound_kernel(gq_ref, gk_ref, ck_ref, o_ref):
    layer, b = pl.program_id(0), pl.program_id(1)
    ck = ck_ref[...]
    cached = jnp.dot((ck * ck).astype(BF16), _group_mean_matrix(), preferred_element_type=F32) * AT_DH
    gq = gq_ref[pl.ds(layer, 1), :]
    gk = gk_ref[pl.ds(layer, 1), :]
    k_sq = jnp.maximum(jnp.max(cached), AT_DH * jnp.max(gk * gk))
    q_sq = AT_DH * jnp.max(gq * gq) * (Q_SCALE * Q_SCALE)
    o_ref[layer, b] = 1.05 * q_sq * k_sq


def _score_bounds(q_norm_g, k_norm_g, ck):
    batch, depth, past, _ = ck.shape
    return pl.pallas_call(
        _score_bound_kernel,
        grid=(depth, batch),
        in_specs=[pl.BlockSpec(q_norm_g.shape, lambda l, b: (0, 0)), pl.BlockSpec(k_norm_g.shape, lambda l, b: (0, 0)),
                  pl.BlockSpec((None, None, past, D_AT), lambda l, b: (b, l, 0, 0))],
        out_specs=pl.BlockSpec(memory_space=pltpu.SMEM),
        out_shape=jax.ShapeDtypeStruct((depth, batch), F32),
        compiler_params=_params("arbitrary", "arbitrary"),
        name="score_bounds",
    )(q_norm_g, k_norm_g, ck)


def _attn_latent_kernel(q_ref, k_ref, v_ref, z_ref, lq_ref, lk_ref, sg_ref, bound_ref, ck_ref, cv_ref,
                        o_ref, *, lam_init, layer):
    lam = _diff_lambda(lq_ref, lk_ref, lam_init)
    small_scores = bound_ref[layer, pl.program_id(0)] <= SCORE_BOUND * SCORE_BOUND

    def scores(h, c):
        cols = slice(h * AT_DV + c * AT_DH, h * AT_DV + (c + 1) * AT_DH)
        q = q_ref[:, cols]
        return _nt_dot(q, k_ref[:, cols]), _nt_dot(q, ck_ref[:, cols].astype(BF16))

    def shifted_parts(s, sc):
        m = jnp.maximum(jnp.max(s, axis=-1, keepdims=True), jnp.max(sc, axis=-1, keepdims=True))
        e = jnp.exp2(s - m)
        ec = jnp.exp2(sc - m)
        return e, ec, jnp.sum(e, axis=-1, keepdims=True) + jnp.sum(ec, axis=-1, keepdims=True)

    def raw_parts(s, sc):
        e = jnp.exp2(s)
        ec = jnp.exp2(sc)
        return e, ec, jnp.sum(e, axis=-1, keepdims=True) + jnp.sum(ec, axis=-1, keepdims=True)

    def finish(h, part1, part2):
        e1, ec1, den1 = part1
        e2, ec2, den2 = part2
        ratio = lam * den1 / den2
        hv = slice(h * AT_DV, (h + 1) * AT_DV)
        o = (jnp.dot((e1 - e2 * ratio).astype(BF16), v_ref[:, hv], preferred_element_type=F32)
             + jnp.dot((ec1 - ec2 * ratio).astype(BF16), cv_ref[:, hv].astype(BF16),
                       preferred_element_type=F32)) * (1.0 / den1)
        o = o * lax.rsqrt(jnp.mean(o * o, axis=-1, keepdims=True) + NORM_EPS) * sg_ref[...]
        o_ref[:, hv] = (o * (1.0 - lam_init) * z_ref[:, hv].astype(F32)).astype(o_ref.dtype)

    def attend(softmax_parts):
        units = [(h, c) for h in range(AT_HEADS) for c in range(2)]
        ahead = [scores(*units[0]), scores(*units[1])]
        parts = [None, None]
        for i, (h, c) in enumerate(units):
            current = ahead.pop(0)
            if i + 2 < len(units):
                ahead.append(scores(*units[i + 2]))
            parts[c] = softmax_parts(*current)
            if c == 1:
                finish(h, parts[0], parts[1])

    pl.when(small_scores)(lambda: attend(raw_parts))
    pl.when(jnp.logical_not(small_scores))(lambda: attend(shifted_parts))


def _attention(act, batch, seq, lam_q, lam_k, subln_g, lam_init, ctx, layer):
    t = batch * seq
    small = lambda a: pl.BlockSpec(a.shape, lambda *_: (0,) * a.ndim)
    params = [lam_q, lam_k, subln_g]
    if ctx is None:
        rows = max(seq, CTX_ATTN_BLOCK_ROWS)
        col = lambda c: pl.BlockSpec((rows, CB), lambda i: (i, c))
        return pl.pallas_call(
            functools.partial(_attn_ctx_kernel, lam_init=lam_init, seq=seq),
            grid=(t // rows,),
            in_specs=[col(COL_Q), col(COL_K), col(COL_V), col(COL_AT_Z)] + [small(a) for a in params],
            out_specs=pl.BlockSpec((rows, D_AT), lambda i: (i, 0)),
            out_shape=jax.ShapeDtypeStruct((t, D_AT), BF16),
            scratch_shapes=[pltpu.VMEM((rows // seq, D_AT + ONES_ROWS, seq), BF16)],
            compiler_params=_params("parallel"),
            name=f"diff_attn_{seq}",
        )(act, act, act, act, *params)
    tq = 512
    nq = seq // tq
    ck, cv, bounds = ctx
    cspec = pl.BlockSpec((None, None, ck.shape[2], D_AT), lambda b, i: (b, layer, 0, 0))
    return pl.pallas_call(
        functools.partial(_attn_latent_kernel, lam_init=lam_init, layer=layer),
        grid=(batch, nq),
        in_specs=[pl.BlockSpec((tq, CB), lambda b, i: (b * nq + i, COL_Q)),
                  pl.BlockSpec((seq, CB), lambda b, i: (b, COL_K)),
                  pl.BlockSpec((seq, CB), lambda b, i: (b, COL_V)),
                  pl.BlockSpec((tq, CB), lambda b, i: (b * nq + i, COL_AT_Z))]
                 + [small(a) for a in params] + [pl.BlockSpec(memory_space=pltpu.SMEM), cspec, cspec],
        out_specs=pl.BlockSpec((tq, D_AT), lambda b, i: (b * nq + i, 0)),
        out_shape=jax.ShapeDtypeStruct((t, D_AT), BF16),
        compiler_params=_params("parallel", "parallel"),
        name=f"diff_attn_{seq}",
    )(act, act, act, act, *params, bounds, ck, cv)


def _pool_kernel(u_ref, z_ref, w_ref, sc_ref, o_ref, *, seq):
    padded = seq + 2 * POOL_PAD
    t = lax.broadcasted_iota(jnp.int32, (seq, 1), 0)
    zeros = jnp.zeros((POOL_PAD, POOL_GC), F32)
    for g, win in enumerate(POOL_WINDOWS):
        cols = slice(g * POOL_GC, (g + 1) * POOL_GC)
        count = (jnp.minimum(t + win // 2, seq) - jnp.maximum(t - win // 2, 0)).astype(F32)
        centred = []
        for r in range(0, u_ref.shape[0], seq):
            u = u_ref[r:r + seq, cols].astype(F32)
            s = jnp.concatenate([zeros, u, zeros], axis=0)
            s = s + pltpu.roll(s, 1, 0)
            half = 1
            while 2 * half < win:
                s = pltpu.roll(s, padded - half, 0) + pltpu.roll(s, half, 0)
                half *= 2
            centred.append(s[POOL_PAD:POOL_PAD + seq] / count - u)
        m = centred[0] if len(centred) == 1 else jnp.concatenate(centred, axis=0)
        y = jnp.dot(m.astype(BF16), w_ref[g].astype(BF16), preferred_element_type=F32)
        o_ref[:, cols] = (y * sc_ref[:, cols] * z_ref[:, cols].astype(F32)).astype(o_ref.dtype)


def _pool_branch(proj, batch, seq, pool_w, pool_scale):
    t = batch * seq
    rows = max(seq, POOL_BLOCK_ROWS)
    return pl.pallas_call(
        functools.partial(_pool_kernel, seq=seq),
        grid=(t // rows,),
        in_specs=[pl.BlockSpec((rows, CB), lambda b: (b, COL_PL_U)),
                  pl.BlockSpec((rows, CB), lambda b: (b, COL_PL_Z)),
                  pl.BlockSpec(pool_w.shape, lambda b: (0, 0, 0)),
                  pl.BlockSpec((1, D_PL), lambda b: (0, 0))],
        out_specs=pl.BlockSpec((rows, D_PL), lambda b: (b, 0)),
        out_shape=jax.ShapeDtypeStruct((t, D_PL), BF16),
        compiler_params=_params("parallel"),
        name=f"pool_{seq}",
    )(proj, proj, pool_w, pool_scale)


def _merge_kernel(x_ref, mod_ref, yh_ref, ya_ref, yp_ref, g0_ref, g1_ref, g2_ref,
                  wh_ref, wa_ref, wp_ref, wo_ref, o_ref):
    merged = (g0_ref[...].astype(F32) * jnp.dot(yh_ref[...], wh_ref[...], preferred_element_type=F32)
              + g1_ref[...].astype(F32) * jnp.dot(ya_ref[...], wa_ref[...], preferred_element_type=F32)
              + g2_ref[...].astype(F32) * jnp.dot(yp_ref[...], wp_ref[...], preferred_element_type=F32))
    out = jnp.dot(merged.astype(BF16), wo_ref[...], preferred_element_type=F32)
    o_ref[...] = x_ref[...] + mod_ref[:, 2 * D_MODEL:3 * D_MODEL] * out


def _merge(x, mod, mod_row0, rows_per_seq, act, y_hy, y_at, y_pl, w_hy_o, w_at_o, w_pl_o, w_out, tm=1024):
    t = x.shape[0]
    blocks_per_seq = rows_per_seq // tm
    d = D_MODEL
    branch = pl.BlockSpec((tm, CB), lambda i: (i, 0))
    gate = lambda g: pl.BlockSpec((tm, d), lambda i: (i, COL_MG // 2 + g))
    return pl.pallas_call(
        _merge_kernel,
        grid=(t // tm,),
        in_specs=[pl.BlockSpec((tm, d), lambda i: (i, 0)),
                  pl.BlockSpec((None, 1, 3 * d), lambda i: (mod_row0 + i // blocks_per_seq, 0, 0)),
                  branch, branch, branch, gate(0), gate(1), gate(2),
                  _const_spec((CB, d)), _const_spec((CB, d)), _const_spec((CB, d)), _const_spec((d, d))],
        out_specs=pl.BlockSpec((tm, d), lambda i: (i, 0)),
        out_shape=jax.ShapeDtypeStruct((t, d), F32),
        compiler_params=_params("parallel"),
        name="merge",
    )(x, mod, y_hy, y_at, y_pl, act, act, act, w_hy_o, w_at_o, w_pl_o, w_out)


def _trunk_layer(x, batch, seq, mod, mod_row0, p, filters, dft, rope, ctx, layer, lam_init, kv_prev=None):
    mod_rows = seq if mod_row0 else batch * seq
    act, kv = _in_proj(x, mod, mod_row0, mod_rows, p["norm_g"], p["w_in"], layer, p["gq"], p["gk"], rope, seq,
                       keep_kv=rope is None, kv_prev=kv_prev)
    y_hy = _hyena_branch(act, batch, seq, p["hy_conv_w"], p["hy_conv_b"], p["hy_bias"], filters, dft)
    y_at = _attention(act, batch, seq, p["lam_q"], p["lam_k"], p["subln_g"], lam_init, ctx, layer)
    y_pl = _pool_branch(act, batch, seq, p["pool_w"], p["pool_scale"])
    out = _merge(x, mod, mod_row0, mod_rows, act, y_hy, y_at, y_pl,
                 p["w_hy_o"], p["w_at_o"], p["w_pl_o"], p["w_out"])
    return out, kv


def kernel(x_prompt, x_sample, cache_k, cache_v, c, c_ctx, norm_g, w_ada, b_ada, w_in, hy_conv_w, hy_conv_b,
           hy_f_w1, hy_f_b1, hy_f_w2, hy_f_b2, hy_f_w3, hy_freq, hy_bias, q_norm_g, k_norm_g, lam_q, lam_k,
           subln_g, pool_w, pool_scale, w_hy_o, w_at_o, w_pl_o, w_out):
    batch, seq, d = x_prompt.shape
    dec_batch, dec_seq, _ = x_sample.shape
    past = cache_k.shape[2]
    assert d == D_MODEL and dec_batch + 1 <= COND_ROWS and dec_seq % GRID_W == 0

    cond = jnp.concatenate([c_ctx[None, :], c, jnp.zeros((COND_ROWS - 1 - dec_batch, d), F32)], axis=0)
    mod = _modulation(cond, w_ada, b_ada).reshape(DEPTH, COND_ROWS, 1, 3 * d)
    dft = {s: _dft_tables(s) for s in (seq, dec_seq)}
    rope = _rope_tables(dec_seq)
    ctx_k = cache_k.reshape(dec_batch, DEPTH, past, D_AT)
    ctx = (ctx_k, cache_v.reshape(dec_batch, DEPTH, past, D_AT), _score_bounds(q_norm_g, k_norm_g, ctx_k))

    w_in_bf16 = w_in.astype(BF16)
    yp = x_prompt.reshape(batch * seq, d)
    ys = x_sample.reshape(dec_batch * dec_seq, d)
    kv = None
    for l in range(DEPTH):
        p = dict(norm_g=norm_g[l][None, :], w_in=w_in_bf16,
                 hy_conv_w=hy_conv_w[l], hy_conv_b=hy_conv_b[l], hy_bias=hy_bias[l],
                 gq=jnp.tile(q_norm_g[l], D_AT // AT_DH)[None, :], gk=jnp.tile(k_norm_g[l], D_AT // AT_DH)[None, :],
                 lam_q=lam_q[l], lam_k=lam_k[l], subln_g=subln_g[l][None, :],
                 pool_w=pool_w[l], pool_scale=pool_scale[l][None, :],
                 w_hy_o=w_hy_o[l].astype(BF16), w_at_o=w_at_o[l].astype(BF16),
                 w_pl_o=w_pl_o[l].astype(BF16), w_out=w_out[l].astype(BF16))
        filters = {s: _hyena_filters(s, hy_f_w1[l], hy_f_b1[l], hy_f_w2[l], hy_f_b2[l], hy_f_w3[l], hy_freq[l],
                                     dft[s]) for s in (seq, dec_seq)}
        lam_init = 0.8 - 0.6 * math.exp(-0.3 * l)
        yp, kv = _trunk_layer(yp, batch, seq, mod[l], 0, p, filters[seq], dft[seq], None, None, l, lam_init, kv)
        ys, _ = _trunk_layer(ys, dec_batch, dec_seq, mod[l], 1, p, filters[dec_seq], dft[dec_seq], rope, ctx, l,
                             lam_init)
    return (yp.reshape(batch, seq, d), ys.reshape(dec_batch, dec_seq, d),
            kv[0].reshape(batch, DEPTH, seq, AT_HEADS, 2, AT_DH), kv[1].reshape(batch, DEPTH, seq, AT_HEADS, AT_DV))
```

```python
import functools
import math

import jax
import jax.numpy as jnp
from jax import lax
from jax.experimental import pallas as pl
from jax.experimental.pallas import tpu as pltpu

F32 = jnp.float32
BF16 = jnp.bfloat16

D_MODEL = 1024
DEPTH = 2
GRID_W = 64
GRID_W_LOG2 = GRID_W.bit_length() - 1
assert GRID_W == 1 << GRID_W_LOG2
D_HY = 512
HY_ORDER = 2
HY_EMB = 33
HY_BANDS = (HY_EMB - 1) // 2
HY_FFN = 64
HY_FAST_DECAY = 0.3
HY_SLOW_DECAY = 1.5
HY_TARGET = 1e-2
AT_HEADS = 4
AT_DH = 64
AT_DV = 2 * AT_DH
D_AT = AT_HEADS * AT_DV
ROPE_BASE = 10000.0
Q_SCALE = AT_DH ** -0.5 * math.log2(math.e)
D_PL = 512
POOL_WINDOWS = (2, 4, 8, 16)
POOL_GC = D_PL // 4
POOL_PAD = 16
POOL_BLOCK_ROWS = 2048
CTX_ATTN_BLOCK_ROWS = 1024
NORM_EPS = 1e-6
D_IN = 8192
COND_ROWS = 8

CB = 512
COL_HY_X1, COL_HY_X2, COL_HY_V, COL_HY_Z = 0, 1, 2, 3
COL_Q, COL_K, COL_V, COL_AT_Z, COL_PL_U, COL_PL_Z = 4, 5, 6, 7, 8, 9
COL_MG = 10

VMEM_LIMIT_BYTES = 56 * 1024 * 1024
LANES = 128


def _params(*sem):
    return pltpu.CompilerParams(dimension_semantics=sem, vmem_limit_bytes=VMEM_LIMIT_BYTES)


def _const_spec(shape):
    return pl.BlockSpec(shape, lambda *_: (0,) * len(shape), pipeline_mode=pl.Buffered(1))


def _silu(x):
    return x * jax.nn.sigmoid(x)


def _bdot(a, b):
    return jnp.dot(a.astype(BF16), b.astype(BF16), preferred_element_type=F32)


def _split(x):
    hi = x.astype(BF16)
    return hi, (x - hi.astype(F32)).astype(BF16)


def _fdot(a, b):
    a_hi, a_lo = _split(a)
    b_hi, b_lo = _split(b)
    dot = functools.partial(jnp.dot, preferred_element_type=F32)
    return dot(a_hi, b_hi) + (dot(a_hi, b_lo) + dot(a_lo, b_hi))


def _mod_kernel(cond_ref, w_ref, b_ref, o_ref):
    o_ref[...] = _fdot(_silu(cond_ref[...]), w_ref[...]) + b_ref[...]


def _modulation(cond, w_ada, b_ada):
    d = D_MODEL
    return pl.pallas_call(
        _mod_kernel,
        grid=(DEPTH, 3),
        in_specs=[
            pl.BlockSpec((COND_ROWS, d), lambda l, n: (0, 0)),
            pl.BlockSpec((None, d, d), lambda l, n: (l, 0, n)),
            pl.BlockSpec((None, 1, d), lambda l, n: (l, 0, n)),
        ],
        out_specs=pl.BlockSpec((None, COND_ROWS, d), lambda l, n: (l, 0, n)),
        out_shape=jax.ShapeDtypeStruct((DEPTH, COND_ROWS, 3 * d), F32),
        compiler_params=_params("parallel", "parallel"),
        name="modulation",
    )(cond, w_ada, b_ada.reshape(DEPTH, 1, 3 * d))


def _group_mean_matrix():
    r = lax.broadcasted_iota(jnp.int32, (D_AT, D_AT), 0) // AT_DH
    c = lax.broadcasted_iota(jnp.int32, (D_AT, D_AT), 1) // AT_DH
    return jnp.where(r == c, 1.0 / AT_DH, 0.0).astype(BF16)


def _group_rms_norm(x, g, mean_mat):
    ms = jnp.dot((x * x).astype(BF16), mean_mat, preferred_element_type=F32)
    return x * lax.rsqrt(ms + NORM_EPS) * g


def _rotate(x, cos, sin_signed):
    lane = lax.broadcasted_iota(jnp.int32, (1, D_AT), 1)
    partner = jnp.where((lane & 1) == 0, pltpu.roll(x, D_AT - 1, 1), pltpu.roll(x, 1, 1))
    return x * cos + partner * sin_signed


def _in_kernel(*refs, rope, kv_slots, seq):
    x_ref, mod_ref, g_ref, w_ref, gq_ref, gk_ref = refs[:6]
    refs = refs[6:]
    if rope:
        cos_ref, sin_ref = refs[:2]
        refs = refs[2:]
    if kv_slots == "own":
        refs = refs[2:]
    o_ref = refs[0]
    h_ref = refs[-1]

    x = x_ref[...]
    y = x * lax.rsqrt(jnp.mean(x * x, axis=-1, keepdims=True) + NORM_EPS) * g_ref[...]
    h_ref[...] = (y * (1.0 + mod_ref[:, D_MODEL:2 * D_MODEL]) + mod_ref[:, 0:D_MODEL]).astype(BF16)

    def chunk(c):
        return jnp.dot(h_ref[...], w_ref[:, c * CB:(c + 1) * CB], preferred_element_type=F32)

    def put(c, val):
        o_ref[:, c * CB:(c + 1) * CB] = val.astype(o_ref.dtype)

    for c in (COL_HY_X1, COL_HY_X2, COL_HY_V, COL_PL_U):
        put(c, chunk(c))
    for c in (COL_HY_Z, COL_AT_Z, COL_PL_Z):
        put(c, _silu(chunk(c)))
    for c in range(COL_MG, D_IN // CB):
        put(c, jax.nn.sigmoid(chunk(c)))

    mean_mat = _group_mean_matrix()
    q = _group_rms_norm(chunk(COL_Q), gq_ref[...], mean_mat)
    k = _group_rms_norm(chunk(COL_K), gk_ref[...], mean_mat)
    v = chunk(COL_V)
    for dst_ref, val in ((refs[1], k), (refs[2], v)) if kv_slots else ():
        for s in range(val.shape[0] // seq):
            rows = val[s * seq:(s + 1) * seq]
            if kv_slots == "all":
                dst_ref[s, 0] = rows
                dst_ref[s, 1:] = jnp.zeros((dst_ref.shape[1] - 1,) + rows.shape, F32)
            else:
                dst_ref[s] = rows
    if rope:
        q = _rotate(q, cos_ref[...], sin_ref[...])
        k = _rotate(k, cos_ref[...], sin_ref[...])
    put(COL_Q, q * Q_SCALE)
    put(COL_K, k)
    put(COL_V, v)


def _in_proj(x, mod, mod_row0, rows_per_seq, norm_g, w_in_bf16, layer, gq, gk, rope, seq, keep_kv, kv_prev,
             tm=512):
    t = x.shape[0]
    blocks_per_seq = rows_per_seq // tm
    vec = pl.BlockSpec((1, D_AT), lambda i: (0, 0))
    in_specs = [pl.BlockSpec((tm, D_MODEL), lambda i: (i, 0)),
                pl.BlockSpec((None, 1, 3 * D_MODEL), lambda i: (mod_row0 + i // blocks_per_seq, 0, 0)),
                pl.BlockSpec((1, D_MODEL), lambda i: (0, 0)),
                pl.BlockSpec((None, D_MODEL, D_IN), lambda i: (layer, 0, 0), pipeline_mode=pl.Buffered(1)),
                vec, vec]
    args = [x, mod, norm_g, w_in_bf16, gq, gk]
    if rope is not None:
        per_seq = seq // tm
        tab = pl.BlockSpec((tm, D_AT), lambda i: (i % per_seq, 0))
        in_specs += [tab, tab]
        args += list(rope)
    out_specs = [pl.BlockSpec((tm, D_IN), lambda i: (i, 0))]
    out_shape = [jax.ShapeDtypeStruct((t, D_IN), BF16)]
    kv_slots, aliases = None, {}
    if keep_kv:
        spb = tm // seq
        shape = jax.ShapeDtypeStruct((t // seq, DEPTH, seq, D_AT), F32)
        out_shape += [shape, shape]
        if kv_prev is None:
            assert layer == 0
            kv_slots = "all"
            out_specs += [pl.BlockSpec((spb, DEPTH, seq, D_AT), lambda i: (i, 0, 0, 0))] * 2
        else:
            kv_slots = "own"
            out_specs += [pl.BlockSpec((spb, None, seq, D_AT), lambda i: (i, layer, 0, 0))] * 2
            aliases = {len(args): 1, len(args) + 1: 2}
            in_specs += [pl.BlockSpec(memory_space=pl.ANY)] * 2
            args += list(kv_prev)
    out = pl.pallas_call(
        functools.partial(_in_kernel, rope=rope is not None, kv_slots=kv_slots, seq=seq),
        grid=(t // tm,),
        in_specs=in_specs,
        out_specs=out_specs,
        out_shape=out_shape,
        input_output_aliases=aliases,
        scratch_shapes=[pltpu.VMEM((tm, D_MODEL), BF16)],
        compiler_params=_params("parallel"),
        name="in_proj_latent" if rope is not None else "in_proj_ctx",
    )(*args)
    return (out[0], (out[1], out[2])) if keep_kv else (out[0], None)


TABLE_ROWS = 32


def _dft_kernel(ce_ref, se_ref, co_ref, so_ref, cot_ref, sot_ref, bc_ref, bs_ref, oc_ref, os_ref, *, seq):
    half = seq // 2
    col = lax.broadcasted_iota(jnp.int32, (1, half), 1)
    base = lax.broadcasted_iota(jnp.int32, (half // TABLE_ROWS, 1), 0) * TABLE_ROWS
    off = lax.broadcasted_iota(jnp.int32, (TABLE_ROWS, 1), 0)

    def trig(c_ref, s_ref, slot, product):
        ang = (product & (2 * seq - 1)).astype(F32) * (math.pi / seq)
        c_ref[slot] = jnp.cos(ang)
        s_ref[slot] = jnp.sin(ang)

    trig(bc_ref, bs_ref, 0, 2 * base * col)
    trig(bc_ref, bs_ref, 1, base * (2 * col + 1))
    trig(bc_ref, bs_ref, 2, (2 * base + 1) * col)
    trig(oc_ref, os_ref, 0, 2 * off * col)
    trig(oc_ref, os_ref, 1, off * (2 * col + 1))
    outs = ((ce_ref, se_ref, 0, 0), (co_ref, so_ref, 1, 1), (cot_ref, sot_ref, 2, 0))

    def block(i, carry):
        rows = pl.ds(pl.multiple_of(i * TABLE_ROWS, TABLE_ROWS), TABLE_ROWS)
        for c_ref, s_ref, b, o in outs:
            bc = bc_ref[b, pl.ds(i, 1), :]
            bs = bs_ref[b, pl.ds(i, 1), :]
            oc = oc_ref[o]
            osn = os_ref[o]
            c_ref[rows, :] = (bc * oc - bs * osn).astype(BF16)
            s_ref[rows, :] = (bs * oc + bc * osn).astype(BF16)
        return carry

    lax.fori_loop(0, half // TABLE_ROWS, block, 0)


def _dft_tables(seq):
    half = seq // 2
    shape = jax.ShapeDtypeStruct((half, half), BF16)
    return pl.pallas_call(
        functools.partial(_dft_kernel, seq=seq),
        out_shape=[shape] * 6,
        scratch_shapes=[pltpu.VMEM((3, half // TABLE_ROWS, half), F32), pltpu.VMEM((3, half // TABLE_ROWS, half), F32),
                        pltpu.VMEM((2, TABLE_ROWS, half), F32), pltpu.VMEM((2, TABLE_ROWS, half), F32)],
        compiler_params=pltpu.CompilerParams(vmem_limit_bytes=VMEM_LIMIT_BYTES),
        name=f"dft_tables_{seq}",
    )()


FILTER_CB = 256


def _filter_kernel(w1t_ref, w1c_ref, w1s_ref, b1_ref, w2_ref, b2_ref, fq_ref, w3f_ref, w3b_ref,
                   ce_ref, se_ref, co_ref, so_ref, alo_ref, blo_ref, ahi_ref, bhi_ref, mid_ref, h_ref, *, seq):
    half = seq // 2
    row = lax.broadcasted_iota(jnp.int32, (half, 1), 0)

    @pl.when(pl.program_id(0) == 0)
    def _():
        band = lax.broadcasted_iota(jnp.int32, (HY_BANDS, 1), 0).astype(F32)
        f = 1e-4 + band * ((HY_BANDS - 1 - 1e-4) / (HY_BANDS - 1))
        for p in range(2):
            pos = (2 * lax.broadcasted_iota(jnp.int32, (1, half), 1) + p).astype(F32)
            fw = f * (pos * (2.0 * math.pi / seq))
            pre = (w1t_ref[...] * (pos * (1.0 / (seq - 1))) + _fdot(w1c_ref[...], jnp.cos(fw))
                   - _fdot(w1s_ref[...], jnp.sin(fw)) + b1_ref[...])
            h = jnp.sin(fq_ref[:, 0:1] * pre)
            h = jnp.sin(fq_ref[:, 1:2] * (_fdot(w2_ref[...], h) + b2_ref[...]))
            h_ref[p] = h.T

    chan = lax.broadcasted_iota(jnp.int32, (1, FILTER_CB), 1) + (pl.program_id(0) % (D_HY // FILTER_CB)) * FILTER_CB
    max_decay = math.log(HY_TARGET) / HY_FAST_DECAY
    min_decay = math.log(HY_TARGET) / HY_SLOW_DECAY
    deltas = jnp.abs(min_decay + chan.astype(F32) * ((max_decay - min_decay) / (D_HY - 1)))

    def taps(p):
        decay = jnp.exp((2 * row + p).astype(F32) * (-1.0 / (seq - 1)) * deltas)
        hf = _fdot(h_ref[p], w3f_ref[...]) * decay
        hb = _fdot(h_ref[p], w3b_ref[...]) * decay
        if p == 0:
            hb = jnp.where(row == 0, 0.0, hb)
        return hf, hb

    hf0, hb0 = taps(0)
    hf1, hb1 = taps(1)
    inv = 1.0 / (jnp.sum(jnp.abs(hf0) + jnp.abs(hf1) + jnp.abs(hb0) + jnp.abs(hb1), axis=0, keepdims=True) + NORM_EPS)
    hs0, hs1 = (hf0 + hb0) * inv, (hf1 + hb1) * inv
    hd0, hd1 = (hf0 - hb0) * inv, (hf1 - hb1) * inv
    c_even = _bdot(ce_ref[...], hs0)
    c_odd = _bdot(co_ref[...], hs1)
    s_even = _bdot(se_ref[...], hd0)
    s_odd = _bdot(so_ref[...], hd1)
    wk = jnp.where(row == 0, 0.5 / seq, 1.0 / seq)
    alo_ref[...] = wk * (c_even + c_odd)
    ahi_ref[...] = wk * (c_even - c_odd)
    blo_ref[...] = -wk * (s_even + s_odd)
    bhi_ref[...] = wk * (s_even - s_odd)
    alt = (1 - 2 * (row & 1)).astype(F32)
    mid_ref[0:1, :] = jnp.sum(alt * hs0, axis=0, keepdims=True) * (1.0 / seq)
    mid_ref[1:2, :] = jnp.sum(alt * hd1, axis=0, keepdims=True) * (-1.0 / seq)


def _hyena_filters(seq, w1, b1, w2, b2, w3, freq, tables):
    nc = HY_ORDER * D_HY
    nblk = nc // FILTER_CB
    half = seq // 2
    small = lambda a: pl.BlockSpec(a.shape, lambda j: (0,) * a.ndim)
    w1t, w1c, w1s = w1[0:1].T, w1[1:1 + HY_BANDS].T, w1[1 + HY_BANDS:].T
    b1 = b1.reshape(HY_FFN, 1)
    b2 = b2.reshape(HY_FFN, 1)
    w2 = w2.T
    freq = freq.T
    out_blk = pl.BlockSpec((half, FILTER_CB), lambda j: (0, j))
    spec_shape = jax.ShapeDtypeStruct((half, nc), F32)
    return pl.pallas_call(
        functools.partial(_filter_kernel, seq=seq),
        grid=(nblk,),
        in_specs=[small(w1t), small(w1c), small(w1s), small(b1), small(w2), small(b2), small(freq),
                  pl.BlockSpec((HY_FFN, FILTER_CB), lambda j: (0, j)),
                  pl.BlockSpec((HY_FFN, FILTER_CB), lambda j: (0, nblk + j))] + [_const_spec((half, half))] * 4,
        out_specs=[out_blk] * 4 + [pl.BlockSpec((2, FILTER_CB), lambda j: (0, j))],
        out_shape=[spec_shape] * 4 + [jax.ShapeDtypeStruct((2, nc), F32)],
        scratch_shapes=[pltpu.VMEM((2, half, HY_FFN), F32)],
        compiler_params=_params("arbitrary"),
        name=f"hyena_filters_{seq}",
    )(w1t, w1c, w1s, b1, w2, b2, freq, w3, w3, *tables[:4])


def _hyena_kernel(*refs, seq, rb):
    x1_ref, x2_ref, v_ref, z_ref = refs[0:4]
    conv_w, conv_c = refs[4:7], refs[7:10]
    spectra = (refs[10:16], refs[16:22])
    ce_ref, se_ref, co_ref, so_ref, cot_ref, sot_ref = refs[22:28]
    o_ref, u_ref, g_ref, ub_ref, pq_ref = refs[28:33]
    half = seq // 2
    slabs = u_ref.shape[0]
    pos = lax.broadcasted_iota(jnp.int32, (seq, 1), 0)
    row = lax.broadcasted_iota(jnp.int32, (half, 1), 0)
    alt = (1 - 2 * (row & 1)).astype(F32)

    def short_conv(x_ref, w_ref, c_ref):
        x = x_ref[...].astype(F32)
        prev = jnp.where(pos == 0, 0.0, pltpu.roll(x, 1, 0))
        nxt = jnp.where(pos == seq - 1, 0.0, pltpu.roll(x, seq - 1, 0))
        return prev * w_ref[0:1, :] + x * w_ref[1:2, :] + nxt * w_ref[2:3, :] + c_ref[...]

    def put_u(rows, val):
        for s in range(slabs):
            u_ref[s, rows, :] = val[:, s * LANES:(s + 1) * LANES]

    def get_u(rows):
        return jnp.concatenate([u_ref[s, rows, :] for s in range(slabs)], axis=1)

    def long_conv(alo_ref, blo_ref, ahi_ref, bhi_ref, mid_ref, skip_ref, finish):
        for p in range(2):
            ub_ref[p] = get_u(pl.ds(p, half, stride=2)).astype(BF16)
        c_mid = jnp.sum(get_u(pl.ds(0, half, stride=2)) * alt, axis=0, keepdims=True)
        s_mid = jnp.sum(get_u(pl.ds(1, half, stride=2)) * alt, axis=0, keepdims=True)
        p_mid = c_mid * mid_ref[0:1, :] + s_mid * mid_ref[1:2, :]
        q_mid = s_mid * mid_ref[0:1, :] - c_mid * mid_ref[1:2, :]
        for r in range(0, half, rb):
            rows = pl.ds(r, rb)
            c_even = jnp.dot(ce_ref[rows, :], ub_ref[0], preferred_element_type=F32)
            c_odd = jnp.dot(co_ref[rows, :], ub_ref[1], preferred_element_type=F32)
            s_even = jnp.dot(se_ref[rows, :], ub_ref[0], preferred_element_type=F32)
            s_odd = jnp.dot(so_ref[rows, :], ub_ref[1], preferred_element_type=F32)
            c_lo, c_hi = c_even + c_odd, c_even - c_odd
            s_lo, s_hi = s_even + s_odd, s_odd - s_even
            alo, blo, ahi, bhi = alo_ref[rows, :], blo_ref[rows, :], ahi_ref[rows, :], bhi_ref[rows, :]
            p_lo, q_lo = c_lo * alo + s_lo * blo, s_lo * alo - c_lo * blo
            p_hi, q_hi = c_hi * ahi + s_hi * bhi, s_hi * ahi - c_hi * bhi
            pq_ref[0, rows, :] = (p_lo + p_hi).astype(BF16)
            pq_ref[1, rows, :] = (q_lo - q_hi).astype(BF16)
            pq_ref[2, rows, :] = (p_lo - p_hi).astype(BF16)
            pq_ref[3, rows, :] = (q_lo + q_hi).astype(BF16)
        for r in range(0, half, rb):
            rows = pl.ds(r, rb)
            even, odd = pl.ds(2 * r, rb, stride=2), pl.ds(2 * r + 1, rb, stride=2)
            ye = (jnp.dot(ce_ref[rows, :], pq_ref[0], preferred_element_type=F32)
                  + jnp.dot(se_ref[rows, :], pq_ref[1], preferred_element_type=F32)
                  + alt[r:r + rb] * p_mid + get_u(even) * skip_ref[...])
            yo = (jnp.dot(cot_ref[rows, :], pq_ref[2], preferred_element_type=F32)
                  + jnp.dot(sot_ref[rows, :], pq_ref[3], preferred_element_type=F32)
                  + alt[r:r + rb] * q_mid + get_u(odd) * skip_ref[...])
            put_u(even, ye)
            put_u(odd, yo)
            both = pl.ds(2 * r, 2 * rb)
            finish(both, g_ref[both, :] * get_u(both))

    def keep(rows, y):
        put_u(rows, y)

    def emit(rows, y):
        o_ref[rows, :] = (y * z_ref[rows, :].astype(F32)).astype(o_ref.dtype)

    put_u(pl.ds(0, seq), short_conv(v_ref, conv_w[2], conv_c[2]))
    g_ref[...] = short_conv(x1_ref, conv_w[0], conv_c[0])
    long_conv(*spectra[0], keep)
    g_ref[...] = short_conv(x2_ref, conv_w[1], conv_c[1])
    long_conv(*spectra[1], emit)


def _hyena_branch(act, batch, seq, conv_w, conv_b, hy_bias, filters, tables):
    cb = 256 if seq > 512 else D_HY
    half = seq // 2
    rb = min(half, 512)
    nj = D_HY // cb
    t = batch * seq
    conv_b = conv_b.reshape(1, 3 * D_HY)
    skip = hy_bias.reshape(1, HY_ORDER * D_HY)
    alo, blo, ahi, bhi, mid = filters

    def act_spec(col512):
        off = col512 * (CB // cb)
        return pl.BlockSpec((seq, cb), lambda j, b: (b, off + j))

    def chan_spec(rows, part):
        off = part * nj
        return pl.BlockSpec((rows, cb), lambda j, b: (0, off + j), pipeline_mode=pl.Buffered(1))

    in_specs = [act_spec(COL_HY_X1), act_spec(COL_HY_X2), act_spec(COL_HY_V), act_spec(COL_HY_Z)]
    in_specs += [chan_spec(3, part) for part in range(3)] + [chan_spec(1, part) for part in range(3)]
    args = [act] * 4 + [conv_w] * 3 + [conv_b] * 3
    for order in range(HY_ORDER):
        in_specs += [chan_spec(half, order)] * 4 + [chan_spec(2, order), chan_spec(1, order)]
        args += [alo, blo, ahi, bhi, mid, skip]
    in_specs += [_const_spec((half, half))] * 6
    args += list(tables)
    return pl.pallas_call(
        functools.partial(_hyena_kernel, seq=seq, rb=rb),
        grid=(nj, batch),
        in_specs=in_specs,
        out_specs=pl.BlockSpec((seq, cb), lambda j, b: (b, j)),
        out_shape=jax.ShapeDtypeStruct((t, D_HY), BF16),
        scratch_shapes=[pltpu.VMEM((cb // LANES, seq, LANES), F32), pltpu.VMEM((seq, cb), F32),
                        pltpu.VMEM((2, half, cb), BF16), pltpu.VMEM((4, half, cb), BF16)],
        compiler_params=_params("parallel", "parallel"),
        name=f"hyena_{seq}",
    )(*args)


def _rope_kernel(cos_ref, sin_ref, c_ref, s_ref, *, seq):
    lane = lax.broadcasted_iota(jnp.int32, (1, D_AT), 1)
    dim = lane & (AT_DH - 1)
    pair = ((dim & (AT_DH // 2 - 1)) >> 1).astype(F32)
    inv = jnp.exp(pair * (-(2.0 / (AT_DH // 2)) * math.log(ROPE_BASE)))
    ang = lax.broadcasted_iota(jnp.int32, (GRID_W, 1), 0).astype(F32) * inv
    c_ref[...] = jnp.cos(ang)
    s_ref[...] = jnp.where((lane & 1) == 0, -jnp.sin(ang), jnp.sin(ang))
    by_row = dim < AT_DH // 2

    def grid_row(r, carry):
        rows = pl.ds(pl.multiple_of(r * GRID_W, GRID_W), GRID_W)
        cos_ref[rows, :] = jnp.where(by_row, c_ref[pl.ds(r, 1), :], c_ref[...])
        sin_ref[rows, :] = jnp.where(by_row, s_ref[pl.ds(r, 1), :], s_ref[...])
        return carry

    lax.fori_loop(0, seq // GRID_W, grid_row, 0)


def _rope_tables(seq):
    assert seq // GRID_W <= GRID_W
    shape = jax.ShapeDtypeStruct((seq, D_AT), F32)
    return pl.pallas_call(
        functools.partial(_rope_kernel, seq=seq),
        out_shape=[shape, shape],
        scratch_shapes=[pltpu.VMEM((GRID_W, D_AT), F32), pltpu.VMEM((GRID_W, D_AT), F32)],
        compiler_params=pltpu.CompilerParams(vmem_limit_bytes=VMEM_LIMIT_BYTES),
        name="rope_tables",
    )()


def _nt_dot(a, b):
    return lax.dot_general(a, b, (((1,), (1,)), ((), ())), preferred_element_type=F32)


ONES_ROWS = 8


def _diff_lambda(lq_ref, lk_ref, lam_init):
    lqk = lq_ref[...] * lk_ref[...]
    return (jnp.exp(jnp.sum(lqk[0:1, :], axis=-1, keepdims=True))
            - jnp.exp(jnp.sum(lqk[1:2, :], axis=-1, keepdims=True)) + lam_init)


def _attn_ctx_kernel(q_ref, k_ref, v_ref, z_ref, lq_ref, lk_ref, sg_ref, o_ref, vt_ref, *, lam_init, seq):
    lam = _diff_lambda(lq_ref, lk_ref, lam_init)
    for n in range(q_ref.shape[0] // seq):
        rows = slice(n * seq, (n + 1) * seq)
        vt_ref[n, 0:D_AT, :] = v_ref[rows, :].astype(F32).T.astype(BF16)
        vt_ref[n, D_AT:D_AT + ONES_ROWS, :] = jnp.ones((ONES_ROWS, seq), BF16)
        ones = vt_ref[n, D_AT:D_AT + ONES_ROWS, :]

        for h in range(AT_HEADS):
            hv = slice(h * AT_DV, (h + 1) * AT_DV)

            def unnormalised(c):
                cols = slice(h * AT_DV + c * AT_DH, h * AT_DV + (c + 1) * AT_DH)
                s = _nt_dot(k_ref[rows, cols], q_ref[rows, cols])
                e = jnp.exp2(s - jnp.max(s, axis=0, keepdims=True)).astype(BF16)
                o = jnp.dot(vt_ref[n, hv, :], e, preferred_element_type=F32)
                den = jnp.dot(ones, e, preferred_element_type=F32)
                return o, den[0:1, :]

            o1, den1 = unnormalised(0)
            o2, den2 = unnormalised(1)
            o = o1 * (1.0 / den1) - o2 * (lam / den2)
            o = o * lax.rsqrt(jnp.mean(o * o, axis=0, keepdims=True) + NORM_EPS)
            o = o.T * sg_ref[...] * (1.0 - lam_init)
            o_ref[rows, hv] = (o * z_ref[rows, hv].astype(F32)).astype(o_ref.dtype)


SCORE_BOUND = 60.0


def _score_bound_kernel(gq_ref, gk_ref, ck_ref, o_ref):
    layer, b = pl.program_id(0), pl.program_id(1)
    ck = ck_ref[...]
    cached = jnp.dot((ck * ck).astype(BF16), _group_mean_matrix(), preferred_element_type=F32) * AT_DH
    gq = gq_ref[pl.ds(layer, 1), :]
    gk = gk_ref[pl.ds(layer, 1), :]
    k_sq = jnp.maximum(jnp.max(cached), AT_DH * jnp.max(gk * gk))
    q_sq = AT_DH * jnp.max(gq * gq) * (Q_SCALE * Q_SCALE)
    o_ref[layer, b] = 1.05 * q_sq * k_sq


def _score_bounds(q_norm_g, k_norm_g, ck):
    batch, depth, past, _ = ck.shape
    return pl.pallas_call(
        _score_bound_kernel,
        grid=(depth, batch),
        in_specs=[pl.BlockSpec(q_norm_g.shape, lambda l, b: (0, 0)), pl.BlockSpec(k_norm_g.shape, lambda l, b: (0, 0)),
                  pl.BlockSpec((None, None, past, D_AT), lambda l, b: (b, l, 0, 0))],
        out_specs=pl.BlockSpec(memory_space=pltpu.SMEM),
        out_shape=jax.ShapeDtypeStruct((depth, batch), F32),
        compiler_params=_params("arbitrary", "arbitrary"),
        name="score_bounds",
    )(q_norm_g, k_norm_g, ck)


def _attn_latent_kernel(q_ref, k_ref, v_ref, z_ref, lq_ref, lk_ref, sg_ref, bound_ref, ck_ref, cv_ref,
                        o_ref, *, lam_init, layer):
    lam = _diff_lambda(lq_ref, lk_ref, lam_init)
    small_scores = bound_ref[layer, pl.program_id(0)] <= SCORE_BOUND * SCORE_BOUND

    def scores(h, c):
        cols = slice(h * AT_DV + c * AT_DH, h * AT_DV + (c + 1) * AT_DH)
        q = q_ref[:, cols]
        return _nt_dot(q, k_ref[:, cols]), _nt_dot(q, ck_ref[:, cols].astype(BF16))

    def shifted_parts(s, sc):
        m = jnp.maximum(jnp.max(s, axis=-1, keepdims=True), jnp.max(sc, axis=-1, keepdims=True))
        e = jnp.exp2(s - m)
        ec = jnp.exp2(sc - m)
        return e, ec, jnp.sum(e, axis=-1, keepdims=True) + jnp.sum(ec, axis=-1, keepdims=True)

    def raw_parts(s, sc):
        e = jnp.exp2(s)
        ec = jnp.exp2(sc)
        return e, ec, jnp.sum(e, axis=-1, keepdims=True) + jnp.sum(ec, axis=-1, keepdims=True)

    def finish(h, part1, part2):
        e1, ec1, den1 = part1
        e2, ec2, den2 = part2
        ratio = lam * den1 / den2
        hv = slice(h * AT_DV, (h + 1) * AT_DV)
        o = (jnp.dot((e1 - e2 * ratio).astype(BF16), v_ref[:, hv], preferred_element_type=F32)
             + jnp.dot((ec1 - ec2 * ratio).astype(BF16), cv_ref[:, hv].astype(BF16),
                       preferred_element_type=F32)) * (1.0 / den1)
        o = o * lax.rsqrt(jnp.mean(o * o, axis=-1, keepdims=True) + NORM_EPS) * sg_ref[...]
        o_ref[:, hv] = (o * (1.0 - lam_init) * z_ref[:, hv].astype(F32)).astype(o_ref.dtype)

    def attend(softmax_parts):
        units = [(h, c) for h in range(AT_HEADS) for c in range(2)]
        ahead = [scores(*units[0]), scores(*units[1])]
        parts = [None, None]
        for i, (h, c) in enumerate(units):
            current = ahead.pop(0)
            if i + 2 < len(units):
                ahead.append(scores(*units[i + 2]))
            parts[c] = softmax_parts(*current)
            if c == 1:
                finish(h, parts[0], parts[1])

    pl.when(small_scores)(lambda: attend(raw_parts))
    pl.when(jnp.logical_not(small_scores))(lambda: attend(shifted_parts))


def _attention(act, batch, seq, lam_q, lam_k, subln_g, lam_init, ctx, layer):
    t = batch * seq
    small = lambda a: pl.BlockSpec(a.shape, lambda *_: (0,) * a.ndim)
    params = [lam_q, lam_k, subln_g]
    if ctx is None:
        rows = max(seq, CTX_ATTN_BLOCK_ROWS)
        col = lambda c: pl.BlockSpec((rows, CB), lambda i: (i, c))
        return pl.pallas_call(
            functools.partial(_attn_ctx_kernel, lam_init=lam_init, seq=seq),
            grid=(t // rows,),
            in_specs=[col(COL_Q), col(COL_K), col(COL_V), col(COL_AT_Z)] + [small(a) for a in params],
            out_specs=pl.BlockSpec((rows, D_AT), lambda i: (i, 0)),
            out_shape=jax.ShapeDtypeStruct((t, D_AT), BF16),
            scratch_shapes=[pltpu.VMEM((rows // seq, D_AT + ONES_ROWS, seq), BF16)],
            compiler_params=_params("parallel"),
            name=f"diff_attn_{seq}",
        )(act, act, act, act, *params)
    tq = 512
    nq = seq // tq
    ck, cv, bounds = ctx
    cspec = pl.BlockSpec((None, None, ck.shape[2], D_AT), lambda b, i: (b, layer, 0, 0))
    return pl.pallas_call(
        functools.partial(_attn_latent_kernel, lam_init=lam_init, layer=layer),
        grid=(batch, nq),
        in_specs=[pl.BlockSpec((tq, CB), lambda b, i: (b * nq + i, COL_Q)),
                  pl.BlockSpec((seq, CB), lambda b, i: (b, COL_K)),
                  pl.BlockSpec((seq, CB), lambda b, i: (b, COL_V)),
                  pl.BlockSpec((tq, CB), lambda b, i: (b * nq + i, COL_AT_Z))]
                 + [small(a) for a in params] + [pl.BlockSpec(memory_space=pltpu.SMEM), cspec, cspec],
        out_specs=pl.BlockSpec((tq, D_AT), lambda b, i: (b * nq + i, 0)),
        out_shape=jax.ShapeDtypeStruct((t, D_AT), BF16),
        compiler_params=_params("parallel", "parallel"),
        name=f"diff_attn_{seq}",
    )(act, act, act, act, *params, bounds, ck, cv)


def _pool_kernel(u_ref, z_ref, w_ref, sc_ref, o_ref, *, seq):
    padded = seq + 2 * POOL_PAD
    t = lax.broadcasted_iota(jnp.int32, (seq, 1), 0)
    zeros = jnp.zeros((POOL_PAD, POOL_GC), F32)
    for g, win in enumerate(POOL_WINDOWS):
        cols = slice(g * POOL_GC, (g + 1) * POOL_GC)
        count = (jnp.minimum(t + win // 2, seq) - jnp.maximum(t - win // 2, 0)).astype(F32)
        centred = []
        for r in range(0, u_ref.shape[0], seq):
            u = u_ref[r:r + seq, cols].astype(F32)
            s = jnp.concatenate([zeros, u, zeros], axis=0)
            s = s + pltpu.roll(s, 1, 0)
            half = 1
            while 2 * half < win:
                s = pltpu.roll(s, padded - half, 0) + pltpu.roll(s, half, 0)
                half *= 2
            centred.append(s[POOL_PAD:POOL_PAD + seq] / count - u)
        m = centred[0] if len(centred) == 1 else jnp.concatenate(centred, axis=0)
        y = jnp.dot(m.astype(BF16), w_ref[g].astype(BF16), preferred_element_type=F32)
        o_ref[:, cols] = (y * sc_ref[:, cols] * z_ref[:, cols].astype(F32)).astype(o_ref.dtype)


def _pool_branch(proj, batch, seq, pool_w, pool_scale):
    t = batch * seq
    rows = max(seq, POOL_BLOCK_ROWS)
    return pl.pallas_call(
        functools.partial(_pool_kernel, seq=seq),
        grid=(t // rows,),
        in_specs=[pl.BlockSpec((rows, CB), lambda b: (b, COL_PL_U)),
                  pl.BlockSpec((rows, CB), lambda b: (b, COL_PL_Z)),
                  pl.BlockSpec(pool_w.shape, lambda b: (0, 0, 0)),
                  pl.BlockSpec((1, D_PL), lambda b: (0, 0))],
        out_specs=pl.BlockSpec((rows, D_PL), lambda b: (b, 0)),
        out_shape=jax.ShapeDtypeStruct((t, D_PL), BF16),
        compiler_params=_params("parallel"),
        name=f"pool_{seq}",
    )(proj, proj, pool_w, pool_scale)


def _merge_kernel(x_ref, mod_ref, yh_ref, ya_ref, yp_ref, g0_ref, g1_ref, g2_ref,
                  wh_ref, wa_ref, wp_ref, wo_ref, o_ref):
    merged = (g0_ref[...].astype(F32) * jnp.dot(yh_ref[...], wh_ref[...], preferred_element_type=F32)
              + g1_ref[...].astype(F32) * jnp.dot(ya_ref[...], wa_ref[...], preferred_element_type=F32)
              + g2_ref[...].astype(F32) * jnp.dot(yp_ref[...], wp_ref[...], preferred_element_type=F32))
    out = jnp.dot(merged.astype(BF16), wo_ref[...], preferred_element_type=F32)
    o_ref[...] = x_ref[...] + mod_ref[:, 2 * D_MODEL:3 * D_MODEL] * out


def _merge(x, mod, mod_row0, rows_per_seq, act, y_hy, y_at, y_pl, w_hy_o, w_at_o, w_pl_o, w_out, tm=1024):
    t = x.shape[0]
    blocks_per_seq = rows_per_seq // tm
    d = D_MODEL
    branch = pl.BlockSpec((tm, CB), lambda i: (i, 0))
    gate = lambda g: pl.BlockSpec((tm, d), lambda i: (i, COL_MG // 2 + g))
    return pl.pallas_call(
        _merge_kernel,
        grid=(t // tm,),
        in_specs=[pl.BlockSpec((tm, d), lambda i: (i, 0)),
                  pl.BlockSpec((None, 1, 3 * d), lambda i: (mod_row0 + i // blocks_per_seq, 0, 0)),
                  branch, branch, branch, gate(0), gate(1), gate(2),
                  _const_spec((CB, d)), _const_spec((CB, d)), _const_spec((CB, d)), _const_spec((d, d))],
        out_specs=pl.BlockSpec((tm, d), lambda i: (i, 0)),
        out_shape=jax.ShapeDtypeStruct((t, d), F32),
        compiler_params=_params("parallel"),
        name="merge",
    )(x, mod, y_hy, y_at, y_pl, act, act, act, w_hy_o, w_at_o, w_pl_o, w_out)


def _trunk_layer(x, batch, seq, mod, mod_row0, p, filters, dft, rope, ctx, layer, lam_init, kv_prev=None):
    mod_rows = seq if mod_row0 else batch * seq
    act, kv = _in_proj(x, mod, mod_row0, mod_rows, p["norm_g"], p["w_in"], layer, p["gq"], p["gk"], rope, seq,
                       keep_kv=rope is None, kv_prev=kv_prev)
    y_hy = _hyena_branch(act, batch, seq, p["hy_conv_w"], p["hy_conv_b"], p["hy_bias"], filters, dft)
    y_at = _attention(act, batch, seq, p["lam_q"], p["lam_k"], p["subln_g"], lam_init, ctx, layer)
    y_pl = _pool_branch(act, batch, seq, p["pool_w"], p["pool_scale"])
    out = _merge(x, mod, mod_row0, mod_rows, act, y_hy, y_at, y_pl,
                 p["w_hy_o"], p["w_at_o"], p["w_pl_o"], p["w_out"])
    return out, kv


def kernel(x_prompt, x_sample, cache_k, cache_v, c, c_ctx, norm_g, w_ada, b_ada, w_in, hy_conv_w, hy_conv_b,
           hy_f_w1, hy_f_b1, hy_f_w2, hy_f_b2, hy_f_w3, hy_freq, hy_bias, q_norm_g, k_norm_g, lam_q, lam_k,
           subln_g, pool_w, pool_scale, w_hy_o, w_at_o, w_pl_o, w_out):
    batch, seq, d = x_prompt.shape
    dec_batch, dec_seq, _ = x_sample.shape
    past = cache_k.shape[2]
    assert d == D_MODEL and dec_batch + 1 <= COND_ROWS and dec_seq % GRID_W == 0

    cond = jnp.concatenate([c_ctx[None, :], c, jnp.zeros((COND_ROWS - 1 - dec_batch, d), F32)], axis=0)
    mod = _modulation(cond, w_ada, b_ada).reshape(DEPTH, COND_ROWS, 1, 3 * d)
    dft = {s: _dft_tables(s) for s in (seq, dec_seq)}
    rope = _rope_tables(dec_seq)
    ctx_k = cache_k.reshape(dec_batch, DEPTH, past, D_AT)
    ctx = (ctx_k, cache_v.reshape(dec_batch, DEPTH, past, D_AT), _score_bounds(q_norm_g, k_norm_g, ctx_k))

    w_in_bf16 = w_in.astype(BF16)
    yp = x_prompt.reshape(batch * seq, d)
    ys = x_sample.reshape(dec_batch * dec_seq, d)
    kv = None
    for l in range(DEPTH):
        p = dict(norm_g=norm_g[l][None, :], w_in=w_in_bf16,
                 hy_conv_w=hy_conv_w[l], hy_conv_b=hy_conv_b[l], hy_bias=hy_bias[l],
                 gq=jnp.tile(q_norm_g[l], D_AT // AT_DH)[None, :], gk=jnp.tile(k_norm_g[l], D_AT // AT_DH)[None, :],
                 lam_q=lam_q[l], lam_k=lam_k[l], subln_g=subln_g[l][None, :],
                 pool_w=pool_w[l], pool_scale=pool_scale[l][None, :],
                 w_hy_o=w_hy_o[l].astype(BF16), w_at_o=w_at_o[l].astype(BF16),
                 w_pl_o=w_pl_o[l].astype(BF16), w_out=w_out[l].astype(BF16))
        filters = {s: _hyena_filters(s, hy_f_w1[l], hy_f_b1[l], hy_f_w2[l], hy_f_b2[l], hy_f_w3[l], hy_freq[l],
                                     dft[s]) for s in (seq, dec_seq)}
        lam_init = 0.8 - 0.6 * math.exp(-0.3 * l)
        yp, kv = _trunk_layer(yp, batch, seq, mod[l], 0, p, filters[seq], dft[seq], None, None, l, lam_init, kv)
        ys, _ = _trunk_layer(ys, dec_batch, dec_seq, mod[l], 1, p, filters[dec_seq], dft[dec_seq], rope, ctx, l,
                             lam_init)
    return (yp.reshape(batch, seq, d), ys.reshape(dec_batch, dec_seq, d),
            kv[0].reshape(batch, DEPTH, seq, AT_HEADS, 2, AT_DH), kv[1].reshape(batch, DEPTH, seq, AT_HEADS, AT_DV))
```

```python
import functools
import math

import jax
import jax.numpy as jnp
from jax import lax
from jax.experimental import pallas as pl
from jax.experimental.pallas import tpu as pltpu

F32 = jnp.float32
BF16 = jnp.bfloat16

D_MODEL = 1024
DEPTH = 2
GRID_W = 64
GRID_W_LOG2 = GRID_W.bit_length() - 1
assert GRID_W == 1 << GRID_W_LOG2
D_HY = 512
HY_ORDER = 2
HY_EMB = 33
HY_BANDS = (HY_EMB - 1) // 2
HY_FFN = 64
HY_FAST_DECAY = 0.3
HY_SLOW_DECAY = 1.5
HY_TARGET = 1e-2
AT_HEADS = 4
AT_DH = 64
AT_DV = 2 * AT_DH
D_AT = AT_HEADS * AT_DV
ROPE_BASE = 10000.0
Q_SCALE = AT_DH ** -0.5 * math.log2(math.e)
D_PL = 512
POOL_WINDOWS = (2, 4, 8, 16)
POOL_GC = D_PL // 4
POOL_PAD = 16
POOL_BLOCK_ROWS = 2048
CTX_ATTN_BLOCK_ROWS = 1024
NORM_EPS = 1e-6
D_IN = 8192
COND_ROWS = 8

CB = 512
COL_HY_X1, COL_HY_X2, COL_HY_V, COL_HY_Z = 0, 1, 2, 3
COL_Q, COL_K, COL_V, COL_AT_Z, COL_PL_U, COL_PL_Z = 4, 5, 6, 7, 8, 9
COL_MG = 10

VMEM_LIMIT_BYTES = 56 * 1024 * 1024
LANES = 128


def _params(*sem):
    return pltpu.CompilerParams(dimension_semantics=sem, vmem_limit_bytes=VMEM_LIMIT_BYTES)


def _const_spec(shape):
    return pl.BlockSpec(shape, lambda *_: (0,) * len(shape), pipeline_mode=pl.Buffered(1))


def _silu(x):
    return x * jax.nn.sigmoid(x)


def _bdot(a, b):
    return jnp.dot(a.astype(BF16), b.astype(BF16), preferred_element_type=F32)


def _split(x):
    hi = x.astype(BF16)
    return hi, (x - hi.astype(F32)).astype(BF16)


def _fdot(a, b):
    a_hi, a_lo = _split(a)
    b_hi, b_lo = _split(b)
    dot = functools.partial(jnp.dot, preferred_element_type=F32)
    return dot(a_hi, b_hi) + (dot(a_hi, b_lo) + dot(a_lo, b_hi))


def _mod_kernel(cond_ref, w_ref, b_ref, o_ref):
    o_ref[...] = _fdot(_silu(cond_ref[...]), w_ref[...]) + b_ref[...]


def _modulation(cond, w_ada, b_ada):
    d = D_MODEL
    return pl.pallas_call(
        _mod_kernel,
        grid=(DEPTH, 3),
        in_specs=[
            pl.BlockSpec((COND_ROWS, d), lambda l, n: (0, 0)),
            pl.BlockSpec((None, d, d), lambda l, n: (l, 0, n)),
            pl.BlockSpec((None, 1, d), lambda l, n: (l, 0, n)),
        ],
        out_specs=pl.BlockSpec((None, COND_ROWS, d), lambda l, n: (l, 0, n)),
        out_shape=jax.ShapeDtypeStruct((DEPTH, COND_ROWS, 3 * d), F32),
        compiler_params=_params("parallel", "parallel"),
        name="modulation",
    )(cond, w_ada, b_ada.reshape(DEPTH, 1, 3 * d))


def _group_mean_matrix():
    r = lax.broadcasted_iota(jnp.int32, (D_AT, D_AT), 0) // AT_DH
    c = lax.broadcasted_iota(jnp.int32, (D_AT, D_AT), 1) // AT_DH
    return jnp.where(r == c, 1.0 / AT_DH, 0.0).astype(BF16)


def _group_rms_norm(x, g, mean_mat):
    ms = jnp.dot((x * x).astype(BF16), mean_mat, preferred_element_type=F32)
    return x * lax.rsqrt(ms + NORM_EPS) * g


def _rotate(x, cos, sin_signed):
    lane = lax.broadcasted_iota(jnp.int32, (1, D_AT), 1)
    partner = jnp.where((lane & 1) == 0, pltpu.roll(x, D_AT - 1, 1), pltpu.roll(x, 1, 1))
    return x * cos + partner * sin_signed


def _in_kernel(*refs, rope, kv_slots, seq):
    x_ref, mod_ref, g_ref, w_ref, gq_ref, gk_ref = refs[:6]
    refs = refs[6:]
    if rope:
        cos_ref, sin_ref = refs[:2]
        refs = refs[2:]
    if kv_slots == "own":
        refs = refs[2:]
    o_ref = refs[0]
    h_ref = refs[-1]

    x = x_ref[...]
    y = x * lax.rsqrt(jnp.mean(x * x, axis=-1, keepdims=True) + NORM_EPS) * g_ref[...]
    h_ref[...] = (y * (1.0 + mod_ref[:, D_MODEL:2 * D_MODEL]) + mod_ref[:, 0:D_MODEL]).astype(BF16)

    def chunk(c):
        return jnp.dot(h_ref[...], w_ref[:, c * CB:(c + 1) * CB], preferred_element_type=F32)

    def put(c, val):
        o_ref[:, c * CB:(c + 1) * CB] = val.astype(o_ref.dtype)

    for c in (COL_HY_X1, COL_HY_X2, COL_HY_V, COL_PL_U):
        put(c, chunk(c))
    for c in (COL_HY_Z, COL_AT_Z, COL_PL_Z):
        put(c, _silu(chunk(c)))
    for c in range(COL_MG, D_IN // CB):
        put(c, jax.nn.sigmoid(chunk(c)))

    mean_mat = _group_mean_matrix()
    q = _group_rms_norm(chunk(COL_Q), gq_ref[...], mean_mat)
    k = _group_rms_norm(chunk(COL_K), gk_ref[...], mean_mat)
    v = chunk(COL_V)
    for dst_ref, val in ((refs[1], k), (refs[2], v)) if kv_slots else ():
        for s in range(val.shape[0] // seq):
            rows = val[s * seq:(s + 1) * seq]
            if kv_slots == "all":
                dst_ref[s, 0] = rows
                dst_ref[s, 1:] = jnp.zeros((dst_ref.shape[1] - 1,) + rows.shape, F32)
            else:
                dst_ref[s] = rows
    if rope:
        q = _rotate(q, cos_ref[...], sin_ref[...])
        k = _rotate(k, cos_ref[...], sin_ref[...])
    put(COL_Q, q * Q_SCALE)
    put(COL_K, k)
    put(COL_V, v)


def _in_proj(x, mod, mod_row0, rows_per_seq, norm_g, w_in_bf16, layer, gq, gk, rope, seq, keep_kv, kv_prev,
             tm=512):
    t = x.shape[0]
    blocks_per_seq = rows_per_seq // tm
    vec = pl.BlockSpec((1, D_AT), lambda i: (0, 0))
    in_specs = [pl.BlockSpec((tm, D_MODEL), lambda i: (i, 0)),
                pl.BlockSpec((None, 1, 3 * D_MODEL), lambda i: (mod_row0 + i // blocks_per_seq, 0, 0)),
                pl.BlockSpec((1, D_MODEL), lambda i: (0, 0)),
                pl.BlockSpec((None, D_MODEL, D_IN), lambda i: (layer, 0, 0), pipeline_mode=pl.Buffered(1)),
                vec, vec]
    args = [x, mod, norm_g, w_in_bf16, gq, gk]
    if rope is not None:
        per_seq = seq // tm
        tab = pl.BlockSpec((tm, D_AT), lambda i: (i % per_seq, 0))
        in_specs += [tab, tab]
        args += list(rope)
    out_specs = [pl.BlockSpec((tm, D_IN), lambda i: (i, 0))]
    out_shape = [jax.ShapeDtypeStruct((t, D_IN), BF16)]
    kv_slots, aliases = None, {}
    if keep_kv:
        spb = tm // seq
        shape = jax.ShapeDtypeStruct((t // seq, DEPTH, seq, D_AT), F32)
        out_shape += [shape, shape]
        if kv_prev is None:
            assert layer == 0
            kv_slots = "all"
            out_specs += [pl.BlockSpec((spb, DEPTH, seq, D_AT), lambda i: (i, 0, 0, 0))] * 2
        else:
            kv_slots = "own"
            out_specs += [pl.BlockSpec((spb, None, seq, D_AT), lambda i: (i, layer, 0, 0))] * 2
            aliases = {len(args): 1, len(args) + 1: 2}
            in_specs += [pl.BlockSpec(memory_space=pl.ANY)] * 2
            args += list(kv_prev)
    out = pl.pallas_call(
        functools.partial(_in_kernel, rope=rope is not None, kv_slots=kv_slots, seq=seq),
        grid=(t // tm,),
        in_specs=in_specs,
        out_specs=out_specs,
        out_shape=out_shape,
        input_output_aliases=aliases,
        scratch_shapes=[pltpu.VMEM((tm, D_MODEL), BF16)],
        compiler_params=_params("parallel"),
        name="in_proj_latent" if rope is not None else "in_proj_ctx",
    )(*args)
    return (out[0], (out[1], out[2])) if keep_kv else (out[0], None)


TABLE_ROWS = 32


def _dft_kernel(ce_ref, se_ref, co_ref, so_ref, cot_ref, sot_ref, bc_ref, bs_ref, oc_ref, os_ref, *, seq):
    half = seq // 2
    col = lax.broadcasted_iota(jnp.int32, (1, half), 1)
    base = lax.broadcasted_iota(jnp.int32, (half // TABLE_ROWS, 1), 0) * TABLE_ROWS
    off = lax.broadcasted_iota(jnp.int32, (TABLE_ROWS, 1), 0)

    def trig(c_ref, s_ref, slot, product):
        ang = (product & (2 * seq - 1)).astype(F32) * (math.pi / seq)
        c_ref[slot] = jnp.cos(ang)
        s_ref[slot] = jnp.sin(ang)

    trig(bc_ref, bs_ref, 0, 2 * base * col)
    trig(bc_ref, bs_ref, 1, base * (2 * col + 1))
    trig(bc_ref, bs_ref, 2, (2 * base + 1) * col)
    trig(oc_ref, os_ref, 0, 2 * off * col)
    trig(oc_ref, os_ref, 1, off * (2 * col + 1))
    outs = ((ce_ref, se_ref, 0, 0), (co_ref, so_ref, 1, 1), (cot_ref, sot_ref, 2, 0))

    def block(i, carry):
        rows = pl.ds(pl.multiple_of(i * TABLE_ROWS, TABLE_ROWS), TABLE_ROWS)
        for c_ref, s_ref, b, o in outs:
            bc = bc_ref[b, pl.ds(i, 1), :]
            bs = bs_ref[b, pl.ds(i, 1), :]
            oc = oc_ref[o]
            osn = os_ref[o]
            c_ref[rows, :] = (bc * oc - bs * osn).astype(BF16)
            s_ref[rows, :] = (bs * oc + bc * osn).astype(BF16)
        return carry

    lax.fori_loop(0, half // TABLE_ROWS, block, 0)


def _dft_tables(seq):
    half = seq // 2
    shape = jax.ShapeDtypeStruct((half, half), BF16)
    return pl.pallas_call(
        functools.partial(_dft_kernel, seq=seq),
        out_shape=[shape] * 6,
        scratch_shapes=[pltpu.VMEM((3, half // TABLE_ROWS, half), F32), pltpu.VMEM((3, half // TABLE_ROWS, half), F32),
                        pltpu.VMEM((2, TABLE_ROWS, half), F32), pltpu.VMEM((2, TABLE_ROWS, half), F32)],
        compiler_params=pltpu.CompilerParams(vmem_limit_bytes=VMEM_LIMIT_BYTES),
        name=f"dft_tables_{seq}",
    )()


FILTER_CB = 256


def _filter_kernel(w1t_ref, w1c_ref, w1s_ref, b1_ref, w2_ref, b2_ref, fq_ref, w3f_ref, w3b_ref,
                   ce_ref, se_ref, co_ref, so_ref, alo_ref, blo_ref, ahi_ref, bhi_ref, mid_ref, h_ref, *, seq):
    half = seq // 2
    row = lax.broadcasted_iota(jnp.int32, (half, 1), 0)

    @pl.when(pl.program_id(0) == 0)
    def _():
        band = lax.broadcasted_iota(jnp.int32, (HY_BANDS, 1), 0).astype(F32)
        f = 1e-4 + band * ((HY_BANDS - 1 - 1e-4) / (HY_BANDS - 1))
        for p in range(2):
            pos = (2 * lax.broadcasted_iota(jnp.int32, (1, half), 1) + p).astype(F32)
            fw = f * (pos * (2.0 * math.pi / seq))
            pre = (w1t_ref[...] * (pos * (1.0 / (seq - 1))) + _fdot(w1c_ref[...], jnp.cos(fw))
                   - _fdot(w1s_ref[...], jnp.sin(fw)) + b1_ref[...])
            h = jnp.sin(fq_ref[:, 0:1] * pre)
            h = jnp.sin(fq_ref[:, 1:2] * (_fdot(w2_ref[...], h) + b2_ref[...]))
            h_ref[p] = h.T

    chan = lax.broadcasted_iota(jnp.int32, (1, FILTER_CB), 1) + (pl.program_id(0) % (D_HY // FILTER_CB)) * FILTER_CB
    max_decay = math.log(HY_TARGET) / HY_FAST_DECAY
    min_decay = math.log(HY_TARGET) / HY_SLOW_DECAY
    deltas = jnp.abs(min_decay + chan.astype(F32) * ((max_decay - min_decay) / (D_HY - 1)))

    def taps(p):
        decay = jnp.exp((2 * row + p).astype(F32) * (-1.0 / (seq - 1)) * deltas)
        hf = _fdot(h_ref[p], w3f_ref[...]) * decay
        hb = _fdot(h_ref[p], w3b_ref[...]) * decay
        if p == 0:
            hb = jnp.where(row == 0, 0.0, hb)
        return hf, hb

    hf0, hb0 = taps(0)
    hf1, hb1 = taps(1)
    inv = 1.0 / (jnp.sum(jnp.abs(hf0) + jnp.abs(hf1) + jnp.abs(hb0) + jnp.abs(hb1), axis=0, keepdims=True) + NORM_EPS)
    hs0, hs1 = (hf0 + hb0) * inv, (hf1 + hb1) * inv
    hd0, hd1 = (hf0 - hb0) * inv, (hf1 - hb1) * inv
    c_even = _bdot(ce_ref[...], hs0)
    c_odd = _bdot(co_ref[...], hs1)
    s_even = _bdot(se_ref[...], hd0)
    s_odd = _bdot(so_ref[...], hd1)
    wk = jnp.where(row == 0, 0.5 / seq, 1.0 / seq)
    alo_ref[...] = wk * (c_even + c_odd)
    ahi_ref[...] = wk * (c_even - c_odd)
    blo_ref[...] = -wk * (s_even + s_odd)
    bhi_ref[...] = wk * (s_even - s_odd)
    alt = (1 - 2 * (row & 1)).astype(F32)
    mid_ref[0:1, :] = jnp.sum(alt * hs0, axis=0, keepdims=True) * (1.0 / seq)
    mid_ref[1:2, :] = jnp.sum(alt * hd1, axis=0, keepdims=True) * (-1.0 / seq)


def _hyena_filters(seq, w1, b1, w2, b2, w3, freq, tables):
    nc = HY_ORDER * D_HY
    nblk = nc // FILTER_CB
    half = seq // 2
    small = lambda a: pl.BlockSpec(a.shape, lambda j: (0,) * a.ndim)
    w1t, w1c, w1s = w1[0:1].T, w1[1:1 + HY_BANDS].T, w1[1 + HY_BANDS:].T
    b1 = b1.reshape(HY_FFN, 1)
    b2 = b2.reshape(HY_FFN, 1)
    w2 = w2.T
    freq = freq.T
    out_blk = pl.BlockSpec((half, FILTER_CB), lambda j: (0, j))
    spec_shape = jax.ShapeDtypeStruct((half, nc), F32)
    return pl.pallas_call(
        functools.partial(_filter_kernel, seq=seq),
        grid=(nblk,),
        in_specs=[small(w1t), small(w1c), small(w1s), small(b1), small(w2), small(b2), small(freq),
                  pl.BlockSpec((HY_FFN, FILTER_CB), lambda j: (0, j)),
                  pl.BlockSpec((HY_FFN, FILTER_CB), lambda j: (0, nblk + j))] + [_const_spec((half, half))] * 4,
        out_specs=[out_blk] * 4 + [pl.BlockSpec((2, FILTER_CB), lambda j: (0, j))],
        out_shape=[spec_shape] * 4 + [jax.ShapeDtypeStruct((2, nc), F32)],
        scratch_shapes=[pltpu.VMEM((2, half, HY_FFN), F32)],
        compiler_params=_params("arbitrary"),
        name=f"hyena_filters_{seq}",
    )(w1t, w1c, w1s, b1, w2, b2, freq, w3, w3, *tables[:4])


def _hyena_kernel(*refs, seq, rb):
    x1_ref, x2_ref, v_ref, z_ref = refs[0:4]
    conv_w, conv_c = refs[4:7], refs[7:10]
    spectra = (refs[10:16], refs[16:22])
    ce_ref, se_ref, co_ref, so_ref, cot_ref, sot_ref = refs[22:28]
    o_ref, u_ref, g_ref, ub_ref, pq_ref = refs[28:33]
    half = seq // 2
    slabs = u_ref.shape[0]
    pos = lax.broadcasted_iota(jnp.int32, (seq, 1), 0)
    row = lax.broadcasted_iota(jnp.int32, (half, 1), 0)
    alt = (1 - 2 * (row & 1)).astype(F32)

    def short_conv(x_ref, w_ref, c_ref):
        x = x_ref[...].astype(F32)
        prev = jnp.where(pos == 0, 0.0, pltpu.roll(x, 1, 0))
        nxt = jnp.where(pos == seq - 1, 0.0, pltpu.roll(x, seq - 1, 0))
        return prev * w_ref[0:1, :] + x * w_ref[1:2, :] + nxt * w_ref[2:3, :] + c_ref[...]

    def put_u(rows, val):
        for s in range(slabs):
            u_ref[s, rows, :] = val[:, s * LANES:(s + 1) * LANES]

    def get_u(rows):
        return jnp.concatenate([u_ref[s, rows, :] for s in range(slabs)], axis=1)

    def long_conv(alo_ref, blo_ref, ahi_ref, bhi_ref, mid_ref, skip_ref, finish):
        for p in range(2):
            ub_ref[p] = get_u(pl.ds(p, half, stride=2)).astype(BF16)
        c_mid = jnp.sum(get_u(pl.ds(0, half, stride=2)) * alt, axis=0, keepdims=True)
        s_mid = jnp.sum(get_u(pl.ds(1, half, stride=2)) * alt, axis=0, keepdims=True)
        p_mid = c_mid * mid_ref[0:1, :] + s_mid * mid_ref[1:2, :]
        q_mid = s_mid * mid_ref[0:1, :] - c_mid * mid_ref[1:2, :]
        for r in range(0, half, rb):
            rows = pl.ds(r, rb)
            c_even = jnp.dot(ce_ref[rows, :], ub_ref[0], preferred_element_type=F32)
            c_odd = jnp.dot(co_ref[rows, :], ub_ref[1], preferred_element_type=F32)
            s_even = jnp.dot(se_ref[rows, :], ub_ref[0], preferred_element_type=F32)
            s_odd = jnp.dot(so_ref[rows, :], ub_ref[1], preferred_element_type=F32)
            c_lo, c_hi = c_even + c_odd, c_even - c_odd
            s_lo, s_hi = s_even + s_odd, s_odd - s_even
            alo, blo, ahi, bhi = alo_ref[rows, :], blo_ref[rows, :], ahi_ref[rows, :], bhi_ref[rows, :]
            p_lo, q_lo = c_lo * alo + s_lo * blo, s_lo * alo - c_lo * blo
            p_hi, q_hi = c_hi * ahi + s_hi * bhi, s_hi * ahi - c_hi * bhi
            pq_ref[0, rows, :] = (p_lo + p_hi).astype(BF16)
            pq_ref[1, rows, :] = (q_lo - q_hi).astype(BF16)
            pq_ref[2, rows, :] = (p_lo - p_hi).astype(BF16)
            pq_ref[3, rows, :] = (q_lo + q_hi).astype(BF16)
        for r in range(0, half, rb):
            rows = pl.ds(r, rb)
            even, odd = pl.ds(2 * r, rb, stride=2), pl.ds(2 * r + 1, rb, stride=2)
            ye = (jnp.dot(ce_ref[rows, :], pq_ref[0], preferred_element_type=F32)
                  + jnp.dot(se_ref[rows, :], pq_ref[1], preferred_element_type=F32)
                  + alt[r:r + rb] * p_mid + get_u(even) * skip_ref[...])
            yo = (jnp.dot(cot_ref[rows, :], pq_ref[2], preferred_element_type=F32)
                  + jnp.dot(sot_ref[rows, :], pq_ref[3], preferred_element_type=F32)
                  + alt[r:r + rb] * q_mid + get_u(odd) * skip_ref[...])
            put_u(even, ye)
            put_u(odd, yo)
            both = pl.ds(2 * r, 2 * rb)
            finish(both, g_ref[both, :] * get_u(both))

    def keep(rows, y):
        put_u(rows, y)

    def emit(rows, y):
        o_ref[rows, :] = (y * z_ref[rows, :].astype(F32)).astype(o_ref.dtype)

    put_u(pl.ds(0, seq), short_conv(v_ref, conv_w[2], conv_c[2]))
    g_ref[...] = short_conv(x1_ref, conv_w[0], conv_c[0])
    long_conv(*spectra[0], keep)
    g_ref[...] = short_conv(x2_ref, conv_w[1], conv_c[1])
    long_conv(*spectra[1], emit)


def _hyena_branch(act, batch, seq, conv_w, conv_b, hy_bias, filters, tables):
    cb = 256 if seq > 512 else D_HY
    half = seq // 2
    rb = min(half, 512)
    nj = D_HY // cb
    t = batch * seq
    conv_b = conv_b.reshape(1, 3 * D_HY)
    skip = hy_bias.reshape(1, HY_ORDER * D_HY)
    alo, blo, ahi, bhi, mid = filters

    def act_spec(col512):
        off = col512 * (CB // cb)
        return pl.BlockSpec((seq, cb), lambda j, b: (b, off + j))

    def chan_spec(rows, part):
        off = part * nj
        return pl.BlockSpec((rows, cb), lambda j, b: (0, off + j), pipeline_mode=pl.Buffered(1))

    in_specs = [act_spec(COL_HY_X1), act_spec(COL_HY_X2), act_spec(COL_HY_V), act_spec(COL_HY_Z)]
    in_specs += [chan_spec(3, part) for part in range(3)] + [chan_spec(1, part) for part in range(3)]
    args = [act] * 4 + [conv_w] * 3 + [conv_b] * 3
    for order in range(HY_ORDER):
        in_specs += [chan_spec(half, order)] * 4 + [chan_spec(2, order), chan_spec(1, order)]
        args += [alo, blo, ahi, bhi, mid, skip]
    in_specs += [_const_spec((half, half))] * 6
    args += list(tables)
    return pl.pallas_call(
        functools.partial(_hyena_kernel, seq=seq, rb=rb),
        grid=(nj, batch),
        in_specs=in_specs,
        out_specs=pl.BlockSpec((seq, cb), lambda j, b: (b, j)),
        out_shape=jax.ShapeDtypeStruct((t, D_HY), BF16),
        scratch_shapes=[pltpu.VMEM((cb // LANES, seq, LANES), F32), pltpu.VMEM((seq, cb), F32),
                        pltpu.VMEM((2, half, cb), BF16), pltpu.VMEM((4, half, cb), BF16)],
        compiler_params=_params("parallel", "parallel"),
        name=f"hyena_{seq}",
    )(*args)


def _rope_kernel(cos_ref, sin_ref, c_ref, s_ref, *, seq):
    lane = lax.broadcasted_iota(jnp.int32, (1, D_AT), 1)
    dim = lane & (AT_DH - 1)
    pair = ((dim & (AT_DH // 2 - 1)) >> 1).astype(F32)
    inv = jnp.exp(pair * (-(2.0 / (AT_DH // 2)) * math.log(ROPE_BASE)))
    ang = lax.broadcasted_iota(jnp.int32, (GRID_W, 1), 0).astype(F32) * inv
    c_ref[...] = jnp.cos(ang)
    s_ref[...] = jnp.where((lane & 1) == 0, -jnp.sin(ang), jnp.sin(ang))
    by_row = dim < AT_DH // 2

    def grid_row(r, carry):
        rows = pl.ds(pl.multiple_of(r * GRID_W, GRID_W), GRID_W)
        cos_ref[rows, :] = jnp.where(by_row, c_ref[pl.ds(r, 1), :], c_ref[...])
        sin_ref[rows, :] = jnp.where(by_row, s_ref[pl.ds(r, 1), :], s_ref[...])
        return carry

    lax.fori_loop(0, seq // GRID_W, grid_row, 0)


def _rope_tables(seq):
    assert seq // GRID_W <= GRID_W
    shape = jax.ShapeDtypeStruct((seq, D_AT), F32)
    return pl.pallas_call(
        functools.partial(_rope_kernel, seq=seq),
        out_shape=[shape, shape],
        scratch_shapes=[pltpu.VMEM((GRID_W, D_AT), F32), pltpu.VMEM((GRID_W, D_AT), F32)],
        compiler_params=pltpu.CompilerParams(vmem_limit_bytes=VMEM_LIMIT_BYTES),
        name="rope_tables",
    )()


def _nt_dot(a, b):
    return lax.dot_general(a, b, (((1,), (1,)), ((), ())), preferred_element_type=F32)


ONES_ROWS = 8


def _diff_lambda(lq_ref, lk_ref, lam_init):
    lqk = lq_ref[...] * lk_ref[...]
    return (jnp.exp(jnp.sum(lqk[0:1, :], axis=-1, keepdims=True))
            - jnp.exp(jnp.sum(lqk[1:2, :], axis=-1, keepdims=True)) + lam_init)


def _attn_ctx_kernel(q_ref, k_ref, v_ref, z_ref, lq_ref, lk_ref, sg_ref, gq_ref, gk_ref, o_ref, vt_ref,
                     *, lam_init, seq):
    lam = _diff_lambda(lq_ref, lk_ref, lam_init)
    gq, gk = gq_ref[...], gk_ref[...]
    bound = 1.05 * (AT_DH * jnp.max(gq * gq) * (Q_SCALE * Q_SCALE)) * (AT_DH * jnp.max(gk * gk))
    small_scores = bound <= SCORE_BOUND * SCORE_BOUND

    def attend(shift):
        for n in range(q_ref.shape[0] // seq):
            rows = slice(n * seq, (n + 1) * seq)
            vt_ref[n, 0:D_AT, :] = v_ref[rows, :].astype(F32).T.astype(BF16)
            vt_ref[n, D_AT:D_AT + ONES_ROWS, :] = jnp.ones((ONES_ROWS, seq), BF16)
            ones = vt_ref[n, D_AT:D_AT + ONES_ROWS, :]

            for h in range(AT_HEADS):
                hv = slice(h * AT_DV, (h + 1) * AT_DV)

                def unnormalised(c):
                    cols = slice(h * AT_DV + c * AT_DH, h * AT_DV + (c + 1) * AT_DH)
                    s = _nt_dot(k_ref[rows, cols], q_ref[rows, cols])
                    if shift:
                        s = s - jnp.max(s, axis=0, keepdims=True)
                    e = jnp.exp2(s).astype(BF16)
                    o = jnp.dot(vt_ref[n, hv, :], e, preferred_element_type=F32)
                    den = jnp.dot(ones, e, preferred_element_type=F32)
                    return o, den[0:1, :]

                o1, den1 = unnormalised(0)
                o2, den2 = unnormalised(1)
                o = o1 * (1.0 / den1) - o2 * (lam / den2)
                o = o * lax.rsqrt(jnp.mean(o * o, axis=0, keepdims=True) + NORM_EPS)
                o = o.T * sg_ref[...] * (1.0 - lam_init)
                o_ref[rows, hv] = (o * z_ref[rows, hv].astype(F32)).astype(o_ref.dtype)

    pl.when(small_scores)(lambda: attend(False))
    pl.when(jnp.logical_not(small_scores))(lambda: attend(True))


SCORE_BOUND = 60.0


def _score_bound_kernel(gq_ref, gk_ref, ck_ref, o_ref):
    layer, b = pl.program_id(0), pl.program_id(1)
    ck = ck_ref[...]
    cached = jnp.dot((ck * ck).astype(BF16), _group_mean_matrix(), preferred_element_type=F32) * AT_DH
    gq = gq_ref[pl.ds(layer, 1), :]
    gk = gk_ref[pl.ds(layer, 1), :]
    k_sq = jnp.maximum(jnp.max(cached), AT_DH * jnp.max(gk * gk))
    q_sq = AT_DH * jnp.max(gq * gq) * (Q_SCALE * Q_SCALE)
    o_ref[layer, b] = 1.05 * q_sq * k_sq


def _score_bounds(q_norm_g, k_norm_g, ck):
    batch, depth, past, _ = ck.shape
    return pl.pallas_call(
        _score_bound_kernel,
        grid=(depth, batch),
        in_specs=[pl.BlockSpec(q_norm_g.shape, lambda l, b: (0, 0)), pl.BlockSpec(k_norm_g.shape, lambda l, b: (0, 0)),
                  pl.BlockSpec((None, None, past, D_AT), lambda l, b: (b, l, 0, 0))],
        out_specs=pl.BlockSpec(memory_space=pltpu.SMEM),
        out_shape=jax.ShapeDtypeStruct((depth, batch), F32),
        compiler_params=_params("arbitrary", "arbitrary"),
        name="score_bounds",
    )(q_norm_g, k_norm_g, ck)


def _attn_latent_kernel(q_ref, k_ref, v_ref, z_ref, lq_ref, lk_ref, sg_ref, bound_ref, ck_ref, cv_ref,
                        o_ref, *, lam_init, layer):
    lam = _diff_lambda(lq_ref, lk_ref, lam_init)
    small_scores = bound_ref[layer, pl.program_id(0)] <= SCORE_BOUND * SCORE_BOUND

    def scores(h, c):
        cols = slice(h * AT_DV + c * AT_DH, h * AT_DV + (c + 1) * AT_DH)
        q = q_ref[:, cols]
        return _nt_dot(q, k_ref[:, cols]), _nt_dot(q, ck_ref[:, cols].astype(BF16))

    def shifted_parts(s, sc):
        m = jnp.maximum(jnp.max(s, axis=-1, keepdims=True), jnp.max(sc, axis=-1, keepdims=True))
        e = jnp.exp2(s - m)
        ec = jnp.exp2(sc - m)
        return e, ec, jnp.sum(e, axis=-1, keepdims=True) + jnp.sum(ec, axis=-1, keepdims=True)

    def raw_parts(s, sc):
        e = jnp.exp2(s)
        ec = jnp.exp2(sc)
        return e, ec, jnp.sum(e, axis=-1, keepdims=True) + jnp.sum(ec, axis=-1, keepdims=True)

    def finish(h, part1, part2):
        e1, ec1, den1 = part1
        e2, ec2, den2 = part2
        ratio = lam * den1 / den2
        hv = slice(h * AT_DV, (h + 1) * AT_DV)
        o = (jnp.dot((e1 - e2 * ratio).astype(BF16), v_ref[:, hv], preferred_element_type=F32)
             + jnp.dot((ec1 - ec2 * ratio).astype(BF16), cv_ref[:, hv].astype(BF16),
                       preferred_element_type=F32)) * (1.0 / den1)
        o = o * lax.rsqrt(jnp.mean(o * o, axis=-1, keepdims=True) + NORM_EPS) * sg_ref[...]
        o_ref[:, hv] = (o * (1.0 - lam_init) * z_ref[:, hv].astype(F32)).astype(o_ref.dtype)

    def attend(softmax_parts):
        units = [(h, c) for h in range(AT_HEADS) for c in range(2)]
        ahead = [scores(*units[0]), scores(*units[1])]
        parts = [None, None]
        for i, (h, c) in enumerate(units):
            current = ahead.pop(0)
            if i + 2 < len(units):
                ahead.append(scores(*units[i + 2]))
            parts[c] = softmax_parts(*current)
            if c == 1:
                finish(h, parts[0], parts[1])

    pl.when(small_scores)(lambda: attend(raw_parts))
    pl.when(jnp.logical_not(small_scores))(lambda: attend(shifted_parts))


def _attention(act, batch, seq, lam_q, lam_k, subln_g, gq, gk, lam_init, ctx, layer):
    t = batch * seq
    small = lambda a: pl.BlockSpec(a.shape, lambda *_: (0,) * a.ndim)
    params = [lam_q, lam_k, subln_g]
    if ctx is None:
        rows = max(seq, CTX_ATTN_BLOCK_ROWS)
        col = lambda c: pl.BlockSpec((rows, CB), lambda i: (i, c))
        return pl.pallas_call(
            functools.partial(_attn_ctx_kernel, lam_init=lam_init, seq=seq),
            grid=(t // rows,),
            in_specs=[col(COL_Q), col(COL_K), col(COL_V), col(COL_AT_Z)] + [small(a) for a in params + [gq, gk]],
            out_specs=pl.BlockSpec((rows, D_AT), lambda i: (i, 0)),
            out_shape=jax.ShapeDtypeStruct((t, D_AT), BF16),
            scratch_shapes=[pltpu.VMEM((rows // seq, D_AT + ONES_ROWS, seq), BF16)],
            compiler_params=_params("parallel"),
            name=f"diff_attn_{seq}",
        )(act, act, act, act, *params, gq, gk)
    tq = 512
    nq = seq // tq
    ck, cv, bounds = ctx
    cspec = pl.BlockSpec((None, None, ck.shape[2], D_AT), lambda b, i: (b, layer, 0, 0))
    return pl.pallas_call(
        functools.partial(_attn_latent_kernel, lam_init=lam_init, layer=layer),
        grid=(batch, nq),
        in_specs=[pl.BlockSpec((tq, CB), lambda b, i: (b * nq + i, COL_Q)),
                  pl.BlockSpec((seq, CB), lambda b, i: (b, COL_K)),
                  pl.BlockSpec((seq, CB), lambda b, i: (b, COL_V)),
                  pl.BlockSpec((tq, CB), lambda b, i: (b * nq + i, COL_AT_Z))]
                 + [small(a) for a in params] + [pl.BlockSpec(memory_space=pltpu.SMEM), cspec, cspec],
        out_specs=pl.BlockSpec((tq, D_AT), lambda b, i: (b * nq + i, 0)),
        out_shape=jax.ShapeDtypeStruct((t, D_AT), BF16),
        compiler_params=_params("parallel", "parallel"),
        name=f"diff_attn_{seq}",
    )(act, act, act, act, *params, bounds, ck, cv)


def _pool_kernel(u_ref, z_ref, w_ref, sc_ref, o_ref, *, seq):
    padded = seq + 2 * POOL_PAD
    t = lax.broadcasted_iota(jnp.int32, (seq, 1), 0)
    zeros = jnp.zeros((POOL_PAD, POOL_GC), F32)
    for g, win in enumerate(POOL_WINDOWS):
        cols = slice(g * POOL_GC, (g + 1) * POOL_GC)
        count = (jnp.minimum(t + win // 2, seq) - jnp.maximum(t - win // 2, 0)).astype(F32)
        centred = []
        for r in range(0, u_ref.shape[0], seq):
            u = u_ref[r:r + seq, cols].astype(F32)
            s = jnp.concatenate([zeros, u, zeros], axis=0)
            s = s + pltpu.roll(s, 1, 0)
            half = 1
            while 2 * half < win:
                s = pltpu.roll(s, padded - half, 0) + pltpu.roll(s, half, 0)
                half *= 2
            centred.append(s[POOL_PAD:POOL_PAD + seq] / count - u)
        m = centred[0] if len(centred) == 1 else jnp.concatenate(centred, axis=0)
        y = jnp.dot(m.astype(BF16), w_ref[g].astype(BF16), preferred_element_type=F32)
        o_ref[:, cols] = (y * sc_ref[:, cols] * z_ref[:, cols].astype(F32)).astype(o_ref.dtype)


def _pool_branch(proj, batch, seq, pool_w, pool_scale):
    t = batch * seq
    rows = max(seq, POOL_BLOCK_ROWS)
    return pl.pallas_call(
        functools.partial(_pool_kernel, seq=seq),
        grid=(t // rows,),
        in_specs=[pl.BlockSpec((rows, CB), lambda b: (b, COL_PL_U)),
                  pl.BlockSpec((rows, CB), lambda b: (b, COL_PL_Z)),
                  pl.BlockSpec(pool_w.shape, lambda b: (0, 0, 0)),
                  pl.BlockSpec((1, D_PL), lambda b: (0, 0))],
        out_specs=pl.BlockSpec((rows, D_PL), lambda b: (b, 0)),
        out_shape=jax.ShapeDtypeStruct((t, D_PL), BF16),
        compiler_params=_params("parallel"),
        name=f"pool_{seq}",
    )(proj, proj, pool_w, pool_scale)


def _merge_kernel(x_ref, mod_ref, yh_ref, ya_ref, yp_ref, g0_ref, g1_ref, g2_ref,
                  wh_ref, wa_ref, wp_ref, wo_ref, o_ref):
    merged = (g0_ref[...].astype(F32) * jnp.dot(yh_ref[...], wh_ref[...], preferred_element_type=F32)
              + g1_ref[...].astype(F32) * jnp.dot(ya_ref[...], wa_ref[...], preferred_element_type=F32)
              + g2_ref[...].astype(F32) * jnp.dot(yp_ref[...], wp_ref[...], preferred_element_type=F32))
    out = jnp.dot(merged.astype(BF16), wo_ref[...], preferred_element_type=F32)
    o_ref[...] = x_ref[...] + mod_ref[:, 2 * D_MODEL:3 * D_MODEL] * out


def _merge(x, mod, mod_row0, rows_per_seq, act, y_hy, y_at, y_pl, w_hy_o, w_at_o, w_pl_o, w_out, tm=1024):
    t = x.shape[0]
    blocks_per_seq = rows_per_seq // tm
    d = D_MODEL
    branch = pl.BlockSpec((tm, CB), lambda i: (i, 0))
    gate = lambda g: pl.BlockSpec((tm, d), lambda i: (i, COL_MG // 2 + g))
    return pl.pallas_call(
        _merge_kernel,
        grid=(t // tm,),
        in_specs=[pl.BlockSpec((tm, d), lambda i: (i, 0)),
                  pl.BlockSpec((None, 1, 3 * d), lambda i: (mod_row0 + i // blocks_per_seq, 0, 0)),
                  branch, branch, branch, gate(0), gate(1), gate(2),
                  _const_spec((CB, d)), _const_spec((CB, d)), _const_spec((CB, d)), _const_spec((d, d))],
        out_specs=pl.BlockSpec((tm, d), lambda i: (i, 0)),
        out_shape=jax.ShapeDtypeStruct((t, d), F32),
        compiler_params=_params("parallel"),
        name="merge",
    )(x, mod, y_hy, y_at, y_pl, act, act, act, w_hy_o, w_at_o, w_pl_o, w_out)


def _trunk_layer(x, batch, seq, mod, mod_row0, p, filters, dft, rope, ctx, layer, lam_init, kv_prev=None):
    mod_rows = seq if mod_row0 else batch * seq
    act, kv = _in_proj(x, mod, mod_row0, mod_rows, p["norm_g"], p["w_in"], layer, p["gq"], p["gk"], rope, seq,
                       keep_kv=rope is None, kv_prev=kv_prev)
    y_hy = _hyena_branch(act, batch, seq, p["hy_conv_w"], p["hy_conv_b"], p["hy_bias"], filters, dft)
    y_at = _attention(act, batch, seq, p["lam_q"], p["lam_k"], p["subln_g"], p["gq"], p["gk"], lam_init, ctx, layer)
    y_pl = _pool_branch(act, batch, seq, p["pool_w"], p["pool_scale"])
    out = _merge(x, mod, mod_row0, mod_rows, act, y_hy, y_at, y_pl,
                 p["w_hy_o"], p["w_at_o"], p["w_pl_o"], p["w_out"])
    return out, kv


def kernel(x_prompt, x_sample, cache_k, cache_v, c, c_ctx, norm_g, w_ada, b_ada, w_in, hy_conv_w, hy_conv_b,
           hy_f_w1, hy_f_b1, hy_f_w2, hy_f_b2, hy_f_w3, hy_freq, hy_bias, q_norm_g, k_norm_g, lam_q, lam_k,
           subln_g, pool_w, pool_scale, w_hy_o, w_at_o, w_pl_o, w_out):
    batch, seq, d = x_prompt.shape
    dec_batch, dec_seq, _ = x_sample.shape
    past = cache_k.shape[2]
    assert d == D_MODEL and dec_batch + 1 <= COND_ROWS and dec_seq % GRID_W == 0

    cond = jnp.concatenate([c_ctx[None, :], c, jnp.zeros((COND_ROWS - 1 - dec_batch, d), F32)], axis=0)
    mod = _modulation(cond, w_ada, b_ada).reshape(DEPTH, COND_ROWS, 1, 3 * d)
    dft = {s: _dft_tables(s) for s in (seq, dec_seq)}
    rope = _rope_tables(dec_seq)
    ctx_k = cache_k.reshape(dec_batch, DEPTH, past, D_AT)
    ctx = (ctx_k, cache_v.reshape(dec_batch, DEPTH, past, D_AT), _score_bounds(q_norm_g, k_norm_g, ctx_k))

    w_in_bf16 = w_in.astype(BF16)
    yp = x_prompt.reshape(batch * seq, d)
    ys = x_sample.reshape(dec_batch * dec_seq, d)
    kv = None
    for l in range(DEPTH):
        p = dict(norm_g=norm_g[l][None, :], w_in=w_in_bf16,
                 hy_conv_w=hy_conv_w[l], hy_conv_b=hy_conv_b[l], hy_bias=hy_bias[l],
                 gq=jnp.tile(q_norm_g[l], D_AT // AT_DH)[None, :], gk=jnp.tile(k_norm_g[l], D_AT // AT_DH)[None, :],
                 lam_q=lam_q[l], lam_k=lam_k[l], subln_g=subln_g[l][None, :],
                 pool_w=pool_w[l], pool_scale=pool_scale[l][None, :],
                 w_hy_o=w_hy_o[l].astype(BF16), w_at_o=w_at_o[l].astype(BF16),
                 w_pl_o=w_pl_o[l].astype(BF16), w_out=w_out[l].astype(BF16))
        filters = {s: _hyena_filters(s, hy_f_w1[l], hy_f_b1[l], hy_f_w2[l], hy_f_b2[l], hy_f_w3[l], hy_freq[l],
                                     dft[s]) for s in (seq, dec_seq)}
        lam_init = 0.8 - 0.6 * math.exp(-0.3 * l)
        yp, kv = _trunk_layer(yp, batch, seq, mod[l], 0, p, filters[seq], dft[seq], None, None, l, lam_init, kv)
        ys, _ = _trunk_layer(ys, dec_batch, dec_seq, mod[l], 1, p, filters[dec_seq], dft[dec_seq], rope, ctx, l,
                             lam_init)
    return (yp.reshape(batch, seq, d), ys.reshape(dec_batch, dec_seq, d),
            kv[0].reshape(batch, DEPTH, seq, AT_HEADS, 2, AT_DH), kv[1].reshape(batch, DEPTH, seq, AT_HEADS, AT_DV))
```
